```python
import jax, jax.numpy as jnp
from jax import lax
import numpy as np

D_MODEL = 1024
BATCH = 4
SEQ = 8192
DEPTH = 1

HEAD = 64
RW_HEADS = 8
RW_DIM = RW_HEADS * HEAD
FOX_HEADS = 8
FOX_DIM = FOX_HEADS * HEAD
MIX_DIM = RW_DIM + FOX_DIM
DECAY_LORA = 64
AAA_LORA = 64
GATE_LORA = 128
RW_SPLITS = (RW_DIM, DECAY_LORA, RW_DIM, RW_DIM, AAA_LORA, GATE_LORA)
RW_COLS = sum(RW_SPLITS)
FOX_SPLITS = (FOX_DIM, FOX_DIM, FOX_DIM, FOX_HEADS)
FOX_COLS = sum(FOX_SPLITS)
IN_COLS = RW_COLS + FOX_COLS
Q_BLOCK = 128
N_GROUPS = 8
EXPERTS_PER_GROUP = 8
N_EXPERTS = N_GROUPS * EXPERTS_PER_GROUP
TOP_K = 2
D_EXPERT = D_MODEL // 2
ROW_BLOCK = 128
NORM_EPS = 1e-6
GN_EPS = 64e-5

kernel_name = "hymba_rwkv7_fox_hmoe"


def rms_norm(x, w, eps=NORM_EPS):
    xf = x.astype(jnp.float32)
    y = xf * lax.rsqrt(jnp.mean(xf * xf, axis=-1, keepdims=True) + eps)
    return (y * w.astype(jnp.float32)).astype(x.dtype)


def split_cols(p, sizes):
    idx = np.cumsum(sizes)[:-1].tolist()
    return jnp.split(p, idx, axis=-1)


def token_shift(p, mu):
    prev = jnp.pad(p, ((0, 0), (1, 0), (0, 0)))[:, :-1]
    return p + (prev - p) * mu


def rwkv7_scan(r, w, k, v, a, b):
    B_, T_, H_, N_ = r.shape

    def step(S, inp):
        r_t, w_t, k_t, v_t, a_t, b_t = inp
        sa = jnp.einsum('bhvk,bhk->bhv', S, a_t)
        S = S * w_t[:, :, None, :] + sa[..., None] * b_t[:, :, None, :] + v_t[..., None] * k_t[:, :, None, :]
        return S, jnp.einsum('bhvk,bhk->bhv', S, r_t)

    xs = tuple(jnp.moveaxis(z, 1, 0) for z in (r, w, k, v, a, b))
    S0 = jnp.zeros((B_, H_, N_, N_), jnp.float32)
    _, y = lax.scan(step, S0, xs)
    return jnp.moveaxis(y, 0, 1)


def rwkv7_mixer(p_rw, mu, w0, w2, a0, a2, g2, k_k, k_a, r_k, ln_w, ln_b):
    B_, T_, _ = p_rw.shape
    ps = token_shift(p_rw.astype(jnp.float32), mu.astype(jnp.float32))
    r, pw, k, v, pa, pg = split_cols(ps, RW_SPLITS)
    w = -jax.nn.softplus(-(w0 + jnp.tanh(pw) @ w2)) - 0.5
    decay = jnp.exp(-jnp.exp(w))
    a = jax.nn.sigmoid(a0 + pa @ a2)
    g = jax.nn.sigmoid(pg) @ g2
    heads = lambda z: z.reshape(B_, T_, RW_HEADS, HEAD)
    kk = heads(k * k_k)
    kk = kk / jnp.maximum(jnp.sqrt(jnp.sum(kk * kk, axis=-1, keepdims=True)), 1e-12)
    k = k * (1.0 + (a - 1.0) * k_a)
    r_h, k_h, v_h, a_h, w_h = heads(r), heads(k), heads(v), heads(a), heads(decay)
    y = rwkv7_scan(r_h, w_h, k_h, v_h, -kk, kk * a_h)
    mean = jnp.mean(y, axis=-1, keepdims=True)
    var = jnp.mean(jnp.square(y - mean), axis=-1, keepdims=True)
    y = ((y - mean) * lax.rsqrt(var + GN_EPS)).reshape(B_, T_, RW_DIM) * ln_w + ln_b
    bonus = jnp.sum(r_h * k_h * r_k, axis=-1, keepdims=True) * v_h
    return (y + bonus.reshape(B_, T_, RW_DIM)) * g


def fox_attention(q, k, v, log_f):
    T_ = q.shape[1]
    c = jnp.cumsum(log_f, axis=1).transpose(0, 2, 1)
    scale = HEAD ** -0.5
    outs = []
    for blk in range(T_ // Q_BLOCK):
        q0, q1 = blk * Q_BLOCK, (blk + 1) * Q_BLOCK
        s = jnp.einsum('bqhd,bkhd->bhqk', q[:, q0:q1], k[:, :q1]).astype(jnp.float32) * scale
        s = s + c[:, :, q0:q1, None] - c[:, :, None, :q1]
        causal = jnp.arange(q1)[None, :] <= jnp.arange(q0, q1)[:, None]
        s = jnp.where(causal, s, -jnp.inf)
        p = jax.nn.softmax(s, axis=-1)
        outs.append(jnp.einsum('bhqk,bkhd->bqhd', p.astype(v.dtype), v[:, :q1]))
    return jnp.concatenate(outs, axis=1)


def fox_mixer(p_fox, b_f, qn_w, kn_w):
    B_, T_, _ = p_fox.shape
    q, k, v, f = split_cols(p_fox, FOX_SPLITS)
    heads = lambda z: z.reshape(B_, T_, FOX_HEADS, HEAD)
    q = rms_norm(heads(q), qn_w)
    k = rms_norm(heads(k), kn_w)
    log_f = jax.nn.log_sigmoid((f + b_f).astype(jnp.float32))
    return fox_attention(q, k, heads(v), log_f).reshape(B_, T_, FOX_DIM)


def hierarchical_moe(h, rg_w, rg_b, re_w, re_b, w_gate, w_up, w_down):
    B_, T_, D_ = h.shape
    xt = h.reshape(-1, D_)
    M = xt.shape[0]
    g_logits = (xt @ rg_w + rg_b).astype(jnp.float32)
    g_prob = jax.nn.softmax(g_logits, axis=-1)
    g_sel = jnp.argmax(g_logits, axis=-1)
    g_gate = jnp.take_along_axis(g_prob, g_sel[:, None], axis=-1)
    e_logits = (xt @ re_w + re_b).astype(jnp.float32).reshape(M, N_GROUPS, EXPERTS_PER_GROUP)
    e_logits = jnp.take_along_axis(e_logits, g_sel[:, None, None], axis=1)[:, 0]
    top_p, top_i = lax.top_k(jax.nn.softmax(e_logits, axis=-1), TOP_K)
    gates = g_gate * top_p / jnp.sum(top_p, axis=-1, keepdims=True)
    expert_id = (g_sel[:, None] * EXPERTS_PER_GROUP + top_i).astype(jnp.int32)
    A = M * TOP_K
    flat_id = expert_id.reshape(A)
    flat_tok = jnp.repeat(jnp.arange(M, dtype=jnp.int32), TOP_K)
    order = jnp.argsort(flat_id)
    sid = flat_id[order]
    counts = jnp.bincount(flat_id, length=N_EXPERTS)
    start = jnp.cumsum(counts) - counts
    padded = (counts + ROW_BLOCK - 1) // ROW_BLOCK * ROW_BLOCK
    pad_end = jnp.cumsum(padded)
    pad_start = pad_end - padded
    dest_sorted = (pad_start[sid] + jnp.arange(A, dtype=jnp.int32) - start[sid]).astype(jnp.int32)
    n_blocks = (A + N_EXPERTS * (ROW_BLOCK - 1) + ROW_BLOCK - 1) // ROW_BLOCK
    buf_tok = jnp.zeros((n_blocks * ROW_BLOCK,), jnp.int32).at[dest_sorted].set(flat_tok[order])
    block_expert = jnp.minimum(
        jnp.searchsorted(pad_end, jnp.arange(n_blocks, dtype=jnp.int32) * ROW_BLOCK, side='right'),
        N_EXPERTS - 1)
    xb = xt[buf_tok].reshape(n_blocks, ROW_BLOCK, D_)

    def expert_block(args):
        xblk, e = args
        hid = jax.nn.silu(xblk @ w_gate[e]) * (xblk @ w_up[e])
        return hid @ w_down[e]

    yb = lax.map(expert_block, (xb, block_expert)).reshape(-1, D_)
    dest = jnp.zeros((A,), jnp.int32).at[order].set(dest_sorted)
    y = jnp.sum(yb[dest].reshape(M, TOP_K, D_) * gates[..., None].astype(yb.dtype), axis=1)
    return y.reshape(B_, T_, D_)


def setup_inputs(seed: int = 0) -> dict:
    key = jax.random.key(seed)
    ks = jax.random.split(key, 32)
    n = lambda i, shape, s: jax.random.normal(ks[i], shape, jnp.float32) * s
    L = DEPTH
    return {
        "x": n(0, (BATCH, SEQ, D_MODEL), 1.0),
        "norm_mix_w": 1.0 + n(1, (L, D_MODEL), 0.05),
        "w_in": n(2, (L, D_MODEL, IN_COLS), D_MODEL ** -0.5),
        "mu_shift": jax.random.uniform(ks[3], (L, RW_COLS), jnp.float32),
        "rw_w0": jax.random.uniform(ks[4], (L, RW_DIM), jnp.float32, minval=-4.0, maxval=1.0),
        "rw_w2": n(5, (L, DECAY_LORA, RW_DIM), 0.1 * DECAY_LORA ** -0.5),
        "rw_a0": n(6, (L, RW_DIM), 0.1),
        "rw_a2": n(7, (L, AAA_LORA, RW_DIM), AAA_LORA ** -0.5),
        "rw_g2": n(8, (L, GATE_LORA, RW_DIM), GATE_LORA ** -0.5),
        "rw_k_k": 0.85 + n(9, (L, RW_DIM), 0.05),
        "rw_k_a": 1.0 + n(10, (L, RW_DIM), 0.05),
        "rw_r_k": n(11, (L, RW_HEADS, HEAD), 0.1),
        "rw_ln_w": 1.0 + n(12, (L, RW_DIM), 0.05),
        "rw_ln_b": n(13, (L, RW_DIM), 0.02),
        "fox_b_f": jax.random.uniform(ks[14], (L, FOX_HEADS), jnp.float32, minval=1.0, maxval=4.0),
        "fox_q_norm_w": 1.0 + n(15, (L, HEAD), 0.05),
        "fox_k_norm_w": 1.0 + n(16, (L, HEAD), 0.05),
        "w_out": n(17, (L, MIX_DIM, D_MODEL), MIX_DIM ** -0.5),
        "norm_ffn_w": 1.0 + n(18, (L, D_MODEL), 0.05),
        "router_group_w": n(19, (L, D_MODEL, N_GROUPS), D_MODEL ** -0.5),
        "router_group_b": n(20, (L, N_GROUPS), 0.01),
        "router_expert_w": n(21, (L, D_MODEL, N_EXPERTS), D_MODEL ** -0.5),
        "router_expert_b": n(22, (L, N_EXPERTS), 0.01),
        "exp_w_gate": n(23, (L, N_EXPERTS, D_MODEL, D_EXPERT), D_MODEL ** -0.5),
        "exp_w_up": n(24, (L, N_EXPERTS, D_MODEL, D_EXPERT), D_MODEL ** -0.5),
        "exp_w_down": n(25, (L, N_EXPERTS, D_EXPERT, D_MODEL), D_EXPERT ** -0.5),
    }


def reference(x, norm_mix_w, w_in, mu_shift, rw_w0, rw_w2, rw_a0, rw_a2, rw_g2, rw_k_k, rw_k_a,
              rw_r_k, rw_ln_w, rw_ln_b, fox_b_f, fox_q_norm_w, fox_k_norm_w, w_out, norm_ffn_w,
              router_group_w, router_group_b, router_expert_w, router_expert_b,
              exp_w_gate, exp_w_up, exp_w_down):
    for l in range(DEPTH):
        h = rms_norm(x, norm_mix_w[l])
        p = h @ w_in[l]
        y_rw = rwkv7_mixer(p[..., :RW_COLS], mu_shift[l], rw_w0[l], rw_w2[l], rw_a0[l], rw_a2[l],
                           rw_g2[l], rw_k_k[l], rw_k_a[l], rw_r_k[l], rw_ln_w[l], rw_ln_b[l])
        y_fox = fox_mixer(p[..., RW_COLS:], fox_b_f[l], fox_q_norm_w[l], fox_k_norm_w[l])
        y_mix = jnp.concatenate([y_rw.astype(x.dtype), y_fox.astype(x.dtype)], axis=-1)
        x = x + y_mix @ w_out[l]
        h = rms_norm(x, norm_ffn_w[l])
        x = x + hierarchical_moe(h, router_group_w[l], router_group_b[l], router_expert_w[l],
                                 router_expert_b[l], exp_w_gate[l], exp_w_up[l], exp_w_down[l]).astype(x.dtype)
    return x
```

```python
import functools

import jax
import jax.numpy as jnp
import numpy as np
from jax import lax
from jax.experimental import pallas as pl
from jax.experimental.pallas import tpu as pltpu

F32, BF16, I32 = jnp.float32, jnp.bfloat16, jnp.int32

HEAD = 64
CHUNK = 64
N_GROUPS = 8
EXPERTS_PER_GROUP = 8
N_EXPERTS = N_GROUPS * EXPERTS_PER_GROUP
ROUTER_LANES = 128
NORM_EPS = 1e-6
GN_EPS = 64e-5
NEG_BIG = -1e30
VMEM_LIMIT = 56 * 1024 * 1024


def _cparams(sem):
    return pltpu.CompilerParams(dimension_semantics=sem, vmem_limit_bytes=VMEM_LIMIT)


def _dot(a, b):
    return jnp.dot(a, b, preferred_element_type=F32)


def _dot_nt(a, b):
    return lax.dot_general(a, b, (((1,), (1,)), ((), ())), preferred_element_type=F32)


def _dot_tn(a, b):
    return lax.dot_general(a, b, (((0,), (0,)), ((), ())), preferred_element_type=F32)


def _split3(x):
    hi = x.astype(BF16)
    r1 = x - hi.astype(F32)
    mid = r1.astype(BF16)
    lo = (r1 - mid.astype(F32)).astype(BF16)
    return hi, mid, lo


def _sigmoid(z):
    return 1.0 / (1.0 + jnp.exp(-z))


def _head_sum(x, n_heads):
    lane_head = lax.broadcasted_iota(I32, x.shape, 1) // HEAD
    out = jnp.zeros_like(x)
    for h in range(n_heads):
        m = lane_head == h
        s = jnp.sum(jnp.where(m, x, 0.0), axis=-1, keepdims=True)
        out = jnp.where(m, s, out)
    return out


def _in_proj_kernel(x_ref, nw_ref, wrw_ref, wqkv_ref, wf_ref, prw_ref, pqkv_ref, pf_ref):
    x = x_ref[...]
    h = x * lax.rsqrt(jnp.mean(x * x, axis=-1, keepdims=True) + NORM_EPS) * nw_ref[...]
    hb = h.astype(BF16)
    prw_ref[...] = _dot(hb, wrw_ref[...])
    pqkv_ref[...] = _dot(hb, wqkv_ref[...])
    pf_ref[...] = _dot(hb, wf_ref[...])


def _in_proj(x2, norm_w, w_rw, w_qkv, w_f):
    M, D = x2.shape
    tm = min(512, M)
    n_rw, n_qkv, n_f = w_rw.shape[1], w_qkv.shape[1], w_f.shape[1]
    full = lambda shape: pl.BlockSpec(shape, lambda i: (0, 0))
    return pl.pallas_call(
        _in_proj_kernel,
        grid=(M // tm,),
        in_specs=[pl.BlockSpec((tm, D), lambda i: (i, 0)), full((1, D)),
                  full((D, n_rw)), full((D, n_qkv)), full((D, n_f))],
        out_specs=[pl.BlockSpec((tm, n_rw), lambda i: (i, 0)),
                   pl.BlockSpec((tm, n_qkv), lambda i: (i, 0)),
                   pl.BlockSpec((tm, n_f), lambda i: (i, 0))],
        out_shape=[jax.ShapeDtypeStruct((M, n_rw), F32),
                   jax.ShapeDtypeStruct((M, n_qkv), F32),
                   jax.ShapeDtypeStruct((M, n_f), F32)],
        compiler_params=_cparams(("parallel",)),
        name="in_proj",
    )(x2, norm_w, w_rw, w_qkv, w_f)


def _rw_prep_kernel(n_heads, p_ref, pprev_ref, mu_ref, w0_ref, w2_ref, a0_ref, a2_ref, g2_ref,
                    kk_ref, ka_ref, rk_ref,
                    rt_ref, at_ref, bt_ref, kt_ref, bb_ref, kb_ref, v_ref, dl_ref, bonus_ref, g_ref):
    W = n_heads * HEAD
    j = pl.program_id(1)
    p = p_ref[0]
    tt = p.shape[0]
    last_prev = jnp.where(j > 0, pprev_ref[0, 7:8, :], 0.0)
    row = lax.broadcasted_iota(I32, p.shape, 0)
    prev = jnp.where(row == 0, last_prev, pltpu.roll(p, 1, axis=0))
    ps = p + (prev - p) * mu_ref[...]
    r, k, v = ps[:, 0:W], ps[:, W:2 * W], ps[:, 2 * W:3 * W]
    o = 3 * W
    pw, pa, pg = ps[:, o:o + 64], ps[:, o + 64:o + 128], ps[:, o + 128:o + 256]

    z = w0_ref[...] + _dot(jnp.tanh(pw).astype(BF16), w2_ref[...])
    lw = (-np.exp(-0.5)).astype(np.float32) * _sigmoid(z)
    a_sig = _sigmoid(a0_ref[...] + _dot(pa.astype(BF16), a2_ref[...]))
    g_ref[0] = _dot(_sigmoid(pg).astype(BF16), g2_ref[...])

    kk = k * kk_ref[...]
    kk = kk / jnp.maximum(jnp.sqrt(_head_sum(kk * kk, n_heads)), 1e-12)
    km = k * (1.0 + (a_sig - 1.0) * ka_ref[...])
    a_vec = -kk
    b_vec = kk * a_sig
    bonus_ref[0] = _head_sum(r * km * rk_ref[...], n_heads) * v
    v_ref[0] = v.astype(BF16)

    ti = lax.broadcasted_iota(I32, (tt, tt), 0)
    si = lax.broadcasted_iota(I32, (tt, tt), 1)
    same = (ti // CHUNK) == (si // CHUNK)
    tri = jnp.where(same & (ti >= si), 1.0, 0.0).astype(BF16)
    ones = jnp.where(same, 1.0, 0.0).astype(BF16)
    nck = tt // CHUNK
    ci = lax.broadcasted_iota(I32, (nck, tt), 0)
    cs = lax.broadcasted_iota(I32, (nck, tt), 1)
    sel = jnp.where(ci == cs // CHUNK, 1.0, 0.0).astype(BF16)
    hi, mid, lo = _split3(lw)
    cum = _dot(tri, hi) + _dot(tri, mid) + _dot(tri, lo)
    tot = _dot(ones, hi) + _dot(ones, mid) + _dot(ones, lo)
    dl_ref[0] = jnp.exp(_dot(sel, hi) + _dot(sel, mid) + _dot(sel, lo))

    e_in = jnp.exp(cum)
    e_ex = jnp.exp(cum - lw)
    e_inv = jnp.exp(-cum)
    e_bar = jnp.exp(tot - cum)
    rt_ref[0] = (r * e_in).astype(BF16)
    at_ref[0] = (a_vec * e_ex).astype(BF16)
    bt_ref[0] = (b_vec * e_inv).astype(BF16)
    kt_ref[0] = (km * e_inv).astype(BF16)
    bb_ref[0] = (b_vec * e_bar).astype(BF16)
    kb_ref[0] = (km * e_bar).astype(BF16)


def _rw_prep(p_rw, mu, w0, w2, a0, a2, g2, k_k, k_a, r_k, n_heads):
    B, T, P = p_rw.shape
    W = n_heads * HEAD
    tt = min(512, T)
    row = lambda a: pl.BlockSpec(a.shape, lambda b, j: (0, 0))
    tile = lambda w: pl.BlockSpec((1, tt, w), lambda b, j: (b, j, 0))
    bf = jax.ShapeDtypeStruct((B, T, W), BF16)
    f32 = jax.ShapeDtypeStruct((B, T, W), F32)
    return pl.pallas_call(
        functools.partial(_rw_prep_kernel, n_heads),
        grid=(B, T // tt),
        in_specs=[tile(P),
                  pl.BlockSpec((1, 8, P), lambda b, j: (b, jnp.maximum(j * (tt // 8) - 1, 0), 0)),
                  row(mu), row(w0), row(w2), row(a0), row(a2), row(g2), row(k_k), row(k_a), row(r_k)],
        out_specs=[tile(W)] * 7 + [pl.BlockSpec((1, tt // CHUNK, W), lambda b, j: (b, j, 0)), tile(W), tile(W)],
        out_shape=[bf] * 7 + [jax.ShapeDtypeStruct((B, T // CHUNK, W), F32), f32, f32],
        compiler_params=_cparams(("parallel", "parallel")),
        name="rw_prep",
    )(p_rw, p_rw, mu, w0, w2, a0, a2, g2, k_k, k_a, r_k)


def _rw_scan_kernel(n_heads, rt_ref, at_ref, bt_ref, kt_ref, bb_ref, kb_ref, v_ref, dl_ref,
                    bonus_ref, g_ref, lnw_ref, lnb_ref, y_ref, s_ref):
    C = CHUNK
    tt = rt_ref.shape[1]

    @pl.when(pl.program_id(1) == 0)
    def _():
        s_ref[...] = jnp.zeros_like(s_ref)

    ri = lax.broadcasted_iota(I32, (C, C), 0)
    ci = lax.broadcasted_iota(I32, (C, C), 1)
    strict, incl, eye = ri > ci, ri >= ci, ri == ci

    def chunk(c, carry):
        r0 = pl.multiple_of(c * C, C)
        rows = pl.ds(r0, C)
        dl = dl_ref[0, pl.ds(c, 1), :]
        for h in range(n_heads):
            ls = slice(h * HEAD, (h + 1) * HEAD)
            Rt, At, Bt, Kt = rt_ref[0, rows, ls], at_ref[0, rows, ls], bt_ref[0, rows, ls], kt_ref[0, rows, ls]
            Bb, Kb, V = bb_ref[0, rows, ls], kb_ref[0, rows, ls], v_ref[0, rows, ls]
            Lab = jnp.where(strict, _dot_nt(At, Bt), 0.0)
            Lak = jnp.where(strict, _dot_nt(At, Kt), 0.0).astype(BF16)
            Mrb = jnp.where(incl, _dot_nt(Rt, Bt), 0.0).astype(BF16)
            Mrk = jnp.where(incl, _dot_nt(Rt, Kt), 0.0).astype(BF16)
            P = jnp.where(eye, 1.0, Lab)
            Lp = Lab
            for _ in range(5):
                Lpb = Lp.astype(BF16)
                Lp = _dot(Lpb, Lpb)
                P = P + _dot(P.astype(BF16), Lp.astype(BF16))
            Pb = P.astype(BF16)
            Ah = _dot(Pb, At)
            U0 = _dot(Pb, _dot(Lak, V).astype(BF16))
            Ahb, U0b = Ah.astype(BF16), U0.astype(BF16)
            Rh = Rt.astype(F32) + _dot(Mrb, Ahb)
            Y0 = _dot(Mrb, U0b) + _dot(Mrk, V)
            GT = _dot_tn(Ahb, Bb)
            HT = _dot_tn(U0b, Bb) + _dot_tn(V, Kb)
            S0 = s_ref[h]
            S0b = S0.astype(BF16)
            y = _dot_nt(Rh.astype(BF16), S0b) + Y0
            s_ref[h] = S0 * dl[:, ls] + _dot(S0b, GT.astype(BF16)) + HT
            mean = jnp.mean(y, axis=-1, keepdims=True)
            yc = y - mean
            var = jnp.mean(yc * yc, axis=-1, keepdims=True)
            yn = yc * lax.rsqrt(var + GN_EPS) * lnw_ref[:, ls] + lnb_ref[:, ls]
            y_ref[0, rows, ls] = (yn + bonus_ref[0, rows, ls]) * g_ref[0, rows, ls]
        return carry

    lax.fori_loop(0, tt // C, chunk, 0)


def _rw_scan(rt, at, bt, kt, bb, kb, v, dl, bonus, g, ln_w, ln_b, n_heads):
    B, T, W = rt.shape
    tt = min(512, T)
    tile = pl.BlockSpec((1, tt, W), lambda b, j: (b, j, 0))
    row = pl.BlockSpec((1, W), lambda b, j: (0, 0))
    return pl.pallas_call(
        functools.partial(_rw_scan_kernel, n_heads),
        grid=(B, T // tt),
        in_specs=[tile] * 7 + [pl.BlockSpec((1, tt // CHUNK, W), lambda b, j: (b, j, 0)), tile, tile, row, row],
        out_specs=tile,
        out_shape=jax.ShapeDtypeStruct((B, T, W), F32),
        scratch_shapes=[pltpu.VMEM((n_heads, HEAD, HEAD), F32)],
        compiler_params=_cparams(("parallel", "arbitrary")),
        name="rw_scan",
    )(rt, at, bt, kt, bb, kb, v, dl, bonus, g, ln_w, ln_b)


def _fox_prep_kernel(n_heads, qkv_ref, f_ref, bf_ref, qnw_ref, knw_ref,
                     q_ref, k_ref, v_ref, c_ref, ct_ref, carry_ref, carry_t_ref):
    W = n_heads * HEAD
    tt = qkv_ref.shape[1]

    @pl.when(pl.program_id(1) == 0)
    def _():
        carry_ref[...] = jnp.zeros_like(carry_ref)
        carry_t_ref[...] = jnp.zeros_like(carry_t_ref)

    qkv = qkv_ref[0]
    q, k, v = qkv[:, 0:W], qkv[:, W:2 * W], qkv[:, 2 * W:3 * W]
    inv_n = 1.0 / HEAD
    qn = q * lax.rsqrt(_head_sum(q * q, n_heads) * inv_n + NORM_EPS) * qnw_ref[...]
    kn = k * lax.rsqrt(_head_sum(k * k, n_heads) * inv_n + NORM_EPS) * knw_ref[...]
    q_ref[0] = (qn * (HEAD ** -0.5)).astype(BF16)
    k_ref[0] = kn.astype(BF16)
    v_ref[0] = v.astype(BF16)

    zf = f_ref[0] + bf_ref[...]
    logf = jnp.minimum(zf, 0.0) - jnp.log(1.0 + jnp.exp(-jnp.abs(zf)))
    ti = lax.broadcasted_iota(I32, (tt, tt), 0)
    si = lax.broadcasted_iota(I32, (tt, tt), 1)
    tri = jnp.where(ti >= si, 1.0, 0.0).astype(BF16)
    tri_t = jnp.where(ti <= si, 1.0, 0.0).astype(BF16)
    hi, mid, lo = _split3(logf)
    c = carry_ref[...] + _dot(tri, hi) + _dot(tri, mid) + _dot(tri, lo)
    ct = carry_t_ref[...][:, 0:1] + (_dot_tn(hi, tri_t) + _dot_tn(mid, tri_t) + _dot_tn(lo, tri_t))[0:8, :]
    c_ref[0] = c
    ct_ref[0] = ct
    carry_ref[...] = c[tt - 1:tt, :]
    carry_t_ref[...] = jnp.broadcast_to(ct[:, tt - 1:tt], carry_t_ref.shape)


def _fox_prep(qkv, f, b_f, qn_w, kn_w, n_heads):
    B, T, _ = qkv.shape
    W = n_heads * HEAD
    tt = min(256, T)
    row = lambda a: pl.BlockSpec(a.shape, lambda b, j: (0, 0))
    bf = jax.ShapeDtypeStruct((B, T, W), BF16)
    return pl.pallas_call(
        functools.partial(_fox_prep_kernel, n_heads),
        grid=(B, T // tt),
        in_specs=[pl.BlockSpec((1, tt, 3 * W), lambda b, j: (b, j, 0)),
                  pl.BlockSpec((1, tt, 128), lambda b, j: (b, j, 0)),
                  row(b_f), row(qn_w), row(kn_w)],
        out_specs=[pl.BlockSpec((1, tt, W), lambda b, j: (b, j, 0))] * 3
        + [pl.BlockSpec((1, tt, 128), lambda b, j: (b, j, 0)),
           pl.BlockSpec((1, 8, tt), lambda b, j: (b, 0, j))],
        out_shape=[bf, bf, bf, jax.ShapeDtypeStruct((B, T, 128), F32), jax.ShapeDtypeStruct((B, 8, T), F32)],
        scratch_shapes=[pltpu.VMEM((1, 128), F32), pltpu.VMEM((8, 128), F32)],
        compiler_params=_cparams(("parallel", "arbitrary")),
        name="fox_prep",
    )(qkv, f, b_f, qn_w, kn_w)


def _fox_attn_kernel(q_ref, k_ref, v_ref, c_ref, ct_ref, o_ref):
    pair, i = pl.program_id(1), pl.program_id(2)
    tq = q_ref.shape[1]
    q = q_ref[0]
    c_all = c_ref[0]
    lane = lax.broadcasted_iota(I32, (tq, 128), 1)
    ri = lax.broadcasted_iota(I32, (tq, tq), 0)
    ci = lax.broadcasted_iota(I32, (tq, tq), 1)
    outs = []
    for hh in range(2):
        hg = pair * 2 + hh
        qh = jnp.where((lane // HEAD) == hh, q, jnp.zeros_like(q))
        cq = jnp.sum(jnp.where(lane == hg, c_all, 0.0), axis=-1, keepdims=True)

        def step(j, carry, diag, qh=qh, cq=cq, hg=hg):
            m, l, acc = carry
            k0 = pl.multiple_of(j * tq, tq)
            kj = k_ref[0, pl.ds(k0, tq), :]
            vj = v_ref[0, pl.ds(k0, tq), :]
            ck = ct_ref[0, hg, pl.ds(j, 1), :]
            s = _dot_nt(qh, kj) + (cq - ck)
            if diag:
                s = jnp.where(ci <= ri, s, -jnp.inf)
            m_new = jnp.maximum(m, jnp.max(s, axis=-1, keepdims=True))
            alpha = jnp.exp(m - m_new)
            p = jnp.exp(s - m_new)
            l = alpha * l + jnp.sum(p, axis=-1, keepdims=True)
            acc = alpha * acc + _dot(p.astype(BF16), vj)
            return m_new, l, acc

        init = (jnp.full((tq, 1), NEG_BIG, F32), jnp.zeros((tq, 1), F32), jnp.zeros((tq, 128), F32))
        carry = lax.fori_loop(0, i, functools.partial(step, diag=False), init)
        _, l, acc = step(i, carry, True)
        outs.append(acc / l)
    o_ref[0] = jnp.where(lane < HEAD, outs[0], outs[1])


def _fox_attn(q, k, v, c, ct, n_heads):
    B, T, W = q.shape
    tq = min(256, T)
    nq = T // tq
    ct4 = ct.reshape(B, 8, nq, tq)
    return pl.pallas_call(
        _fox_attn_kernel,
        grid=(B, n_heads // 2, nq),
        in_specs=[pl.BlockSpec((1, tq, 128), lambda b, p, i: (b, i, p)),
                  pl.BlockSpec((1, T, 128), lambda b, p, i: (b, 0, p)),
                  pl.BlockSpec((1, T, 128), lambda b, p, i: (b, 0, p)),
                  pl.BlockSpec((1, tq, 128), lambda b, p, i: (b, i, 0)),
                  pl.BlockSpec((1, 8, nq, tq), lambda b, p, i: (b, 0, 0, 0))],
        out_specs=pl.BlockSpec((1, tq, 128), lambda b, p, i: (b, i, p)),
        out_shape=jax.ShapeDtypeStruct((B, T, W), F32),
        compiler_params=_cparams(("parallel", "parallel", "arbitrary")),
        name="fox_attn",
    )(q, k, v, c, ct4)


def _out_route_kernel(yrw_ref, yfox_ref, x_ref, wo_rw_ref, wo_fox_ref, nw_ref, wr_hi_ref, wr_lo_ref, br_ref,
                      x1_ref, h2_ref, eid_ref, gate_ref):
    x1 = (x_ref[...] + _dot(yrw_ref[...].astype(BF16), wo_rw_ref[...])
          + _dot(yfox_ref[...].astype(BF16), wo_fox_ref[...]))
    x1_ref[...] = x1
    h2 = x1 * lax.rsqrt(jnp.mean(x1 * x1, axis=-1, keepdims=True) + NORM_EPS) * nw_ref[...]
    h2_ref[...] = h2
    h_hi = h2.astype(BF16)
    h_lo = (h2 - h_hi.astype(F32)).astype(BF16)
    logits = (_dot(h_hi, wr_hi_ref[...]) + _dot(h_hi, wr_lo_ref[...]) + _dot(h_lo, wr_hi_ref[...])) + br_ref[...]

    lane_i = lax.broadcasted_iota(I32, logits.shape, 1)
    lane = lane_i.astype(F32)
    first = lambda mask: jnp.min(jnp.where(mask, lane, 1e9), axis=-1, keepdims=True)
    gl = jnp.where(lane_i < N_GROUPS, logits, -jnp.inf)
    gmax = jnp.max(gl, axis=-1, keepdims=True)
    g_sel = first(gl == gmax)
    g_gate = 1.0 / jnp.sum(jnp.exp(gl - gmax), axis=-1, keepdims=True)
    e_lane = lane_i - N_GROUPS
    lane_grp = jnp.right_shift(e_lane, 3).astype(F32)
    in_grp = (e_lane >= 0) & (e_lane < N_EXPERTS) & (lane_grp == g_sel)
    el = jnp.where(in_grp, logits, -jnp.inf)
    m1 = jnp.max(el, axis=-1, keepdims=True)
    i1 = first(el == m1)
    el2 = jnp.where(lane == i1, -jnp.inf, el)
    m2 = jnp.max(el2, axis=-1, keepdims=True)
    i2 = first(el2 == m2)
    e2 = jnp.exp(m2 - m1)
    g1 = g_gate / (1.0 + e2)
    g2 = g_gate * e2 / (1.0 + e2)
    eid_ref[...] = jnp.where(lane_i == 0, i1 - N_GROUPS, jnp.where(lane_i == 1, i2 - N_GROUPS, 0.0)).astype(I32)
    gate_ref[...] = jnp.where(lane_i == 0, g1, jnp.where(lane_i == 1, g2, 0.0))


def _out_route(y_rw, y_fox, x2, wo_rw, wo_fox, norm_w, wr_hi, wr_lo, b_r):
    M, D = x2.shape
    W = y_rw.shape[1]
    tm = min(256, M)
    full = lambda a: pl.BlockSpec(a.shape, lambda i: (0, 0))
    tile = lambda w: pl.BlockSpec((tm, w), lambda i: (i, 0))
    return pl.pallas_call(
        _out_route_kernel,
        grid=(M // tm,),
        in_specs=[tile(W), tile(W), tile(D), full(wo_rw), full(wo_fox), full(norm_w),
                  full(wr_hi), full(wr_lo), full(b_r)],
        out_specs=[tile(D), tile(D), tile(ROUTER_LANES), tile(ROUTER_LANES)],
        out_shape=[jax.ShapeDtypeStruct((M, D), F32), jax.ShapeDtypeStruct((M, D), F32),
                   jax.ShapeDtypeStruct((M, ROUTER_LANES), I32), jax.ShapeDtypeStruct((M, ROUTER_LANES), F32)],
        compiler_params=_cparams(("parallel",)),
        name="out_route",
    )(y_rw, y_fox, x2, wo_rw, wo_fox, norm_w, wr_hi, wr_lo, b_r)


def _rank_kernel(eid_ref, rank_ref, cnt_ref, carry_ref):
    @pl.when(pl.program_id(0) == 0)
    def _():
        carry_ref[...] = jnp.zeros_like(carry_ref)

    eid = eid_ref[...].astype(F32)
    tm = eid.shape[0]
    lane = lax.broadcasted_iota(I32, eid.shape, 1)
    lane_f = lane.astype(F32)
    pick = lambda l: jnp.sum(jnp.where(lane == l, eid, 0.0), axis=-1, keepdims=True)
    e0, e1 = pick(0), pick(1)
    oh0 = (lane_f == e0).astype(F32)
    oh1 = (lane_f == e1).astype(F32)
    both = oh0 + oh1
    ri = lax.broadcasted_iota(I32, (tm, tm), 0)
    ci = lax.broadcasted_iota(I32, (tm, tm), 1)
    before = _dot(jnp.where(ri > ci, 1.0, 0.0).astype(BF16), both.astype(BF16)) + carry_ref[...]
    r0 = jnp.sum(oh0 * before, axis=-1, keepdims=True)
    r1 = jnp.sum(oh1 * (before + oh0), axis=-1, keepdims=True)
    rank_ref[...] = jnp.where(lane == 0, r0, jnp.where(lane == 1, r1, 0.0)).astype(I32)
    total = carry_ref[...] + jnp.sum(both, axis=0, keepdims=True)
    carry_ref[...] = total
    cnt_ref[...] = jnp.broadcast_to(total, cnt_ref.shape).astype(I32)


def _rank(eid):
    M = eid.shape[0]
    tm = min(512, M)
    return pl.pallas_call(
        _rank_kernel,
        grid=(M // tm,),
        in_specs=[pl.BlockSpec((tm, ROUTER_LANES), lambda i: (i, 0))],
        out_specs=[pl.BlockSpec((tm, ROUTER_LANES), lambda i: (i, 0)),
                   pl.BlockSpec((8, ROUTER_LANES), lambda i: (0, 0))],
        out_shape=[jax.ShapeDtypeStruct((M, ROUTER_LANES), I32), jax.ShapeDtypeStruct((8, ROUTER_LANES), I32)],
        scratch_shapes=[pltpu.VMEM((1, ROUTER_LANES), F32)],
        compiler_params=_cparams(("arbitrary",)),
        name="rank",
    )(eid)


def _dispatch_kernel(dest_ref, h_hbm, xb_in_hbm, xb_hbm, sem):
    del xb_in_hbm
    tmd = dest_ref.shape[1] // 2
    base = pl.program_id(0) * tmd
    copy = lambda t, d: pltpu.make_async_copy(h_hbm.at[pl.ds(base + t, 1)], xb_hbm.at[pl.ds(d, 1)], sem)

    def start(t, carry):
        copy(t, dest_ref[0, 2 * t]).start()
        copy(t, dest_ref[0, 2 * t + 1]).start()
        return carry

    def wait(t, carry):
        copy(t, dest_ref[0, 2 * t]).wait()
        copy(t, dest_ref[0, 2 * t + 1]).wait()
        return carry

    lax.fori_loop(0, tmd, start, 0)
    lax.fori_loop(0, tmd, wait, 0)


def _dispatch(dest, h2, n_rows):
    M, D = h2.shape
    tmd = min(1024, M)
    dest3 = dest.reshape(M // tmd, 1, 2 * tmd)
    zeros = jnp.zeros((n_rows, D), F32)
    return pl.pallas_call(
        _dispatch_kernel,
        grid=(M // tmd,),
        in_specs=[pl.BlockSpec((None, 1, 2 * tmd), lambda i: (i, 0, 0), memory_space=pltpu.SMEM),
                  pl.BlockSpec(memory_space=pl.ANY), pl.BlockSpec(memory_space=pl.ANY)],
        out_specs=pl.BlockSpec(memory_space=pl.ANY),
        out_shape=jax.ShapeDtypeStruct((n_rows, D), F32),
        scratch_shapes=[pltpu.SemaphoreType.DMA(())],
        input_output_aliases={2: 0},
        compiler_params=_cparams(("arbitrary",)),
        name="dispatch",
    )(dest3, h2, zeros)


def _experts_kernel(be_ref, nused_ref, xb_ref, wg_ref, wu_ref, wd_ref, yb_ref, wg_bf, wu_bf, wd_bf):
    i = pl.program_id(0)
    prev = be_ref[jnp.maximum(i - 1, 0)]

    @pl.when((i == 0) | (be_ref[i] != prev))
    def _():
        wg_bf[...] = wg_ref[0].astype(BF16)
        wu_bf[...] = wu_ref[0].astype(BF16)
        wd_bf[...] = wd_ref[0].astype(BF16)

    @pl.when(i < nused_ref[0])
    def _():
        xb = xb_ref[...].astype(BF16)
        gate = _dot(xb, wg_bf[...])
        up = _dot(xb, wu_bf[...])
        hid = gate * _sigmoid(gate) * up
        yb_ref[...] = _dot(hid.astype(BF16), wd_bf[...])

    @pl.when(i >= nused_ref[0])
    def _():
        yb_ref[...] = jnp.zeros_like(yb_ref)


def _experts(block_expert, n_used, xb, w_gate, w_up, w_down, rb):
    n_rows, D = xb.shape
    E, _, F = w_gate.shape
    nb = n_rows // rb
    grid_spec = pltpu.PrefetchScalarGridSpec(
        num_scalar_prefetch=2,
        grid=(nb,),
        in_specs=[pl.BlockSpec((rb, D), lambda i, be, nu: (i, 0)),
                  pl.BlockSpec((1, D, F), lambda i, be, nu: (be[i], 0, 0)),
                  pl.BlockSpec((1, D, F), lambda i, be, nu: (be[i], 0, 0)),
                  pl.BlockSpec((1, F, D), lambda i, be, nu: (be[i], 0, 0))],
        out_specs=pl.BlockSpec((rb, D), lambda i, be, nu: (i, 0)),
        scratch_shapes=[pltpu.VMEM((D, F), BF16), pltpu.VMEM((D, F), BF16), pltpu.VMEM((F, D), BF16)],
    )
    return pl.pallas_call(
        _experts_kernel,
        grid_spec=grid_spec,
        out_shape=jax.ShapeDtypeStruct((n_rows, D), F32),
        compiler_params=_cparams(("arbitrary",)),
        name="experts",
    )(block_expert, n_used, xb, w_gate, w_up, w_down)


def _combine_kernel(dest_ref, gate_ref, x1_ref, yb_hbm, out_ref, buf, sem):
    tmc = x1_ref.shape[0]
    copy = lambda t, s: pltpu.make_async_copy(
        yb_hbm.at[pl.ds(dest_ref[0, 2 * t + s], 1)], buf.at[s, pl.ds(t, 1)], sem)

    def start(t, carry):
        copy(t, 0).start()
        copy(t, 1).start()
        return carry

    def wait(t, carry):
        copy(t, 0).wait()
        copy(t, 1).wait()
        return carry

    lax.fori_loop(0, tmc, start, 0)
    lax.fori_loop(0, tmc, wait, 0)
    gate = gate_ref[...]
    out_ref[...] = x1_ref[...] + gate[:, 0:1] * buf[0] + gate[:, 1:2] * buf[1]


def _combine(dest, gates, x1, yb):
    M, D = x1.shape
    tmc = min(256, M)
    dest3 = dest.reshape(M // tmc, 1, 2 * tmc)
    return pl.pallas_call(
        _combine_kernel,
        grid=(M // tmc,),
        in_specs=[pl.BlockSpec((None, 1, 2 * tmc), lambda i: (i, 0, 0), memory_space=pltpu.SMEM),
                  pl.BlockSpec((tmc, ROUTER_LANES), lambda i: (i, 0)),
                  pl.BlockSpec((tmc, D), lambda i: (i, 0)),
                  pl.BlockSpec(memory_space=pl.ANY)],
        out_specs=pl.BlockSpec((tmc, D), lambda i: (i, 0)),
        out_shape=jax.ShapeDtypeStruct((M, D), F32),
        scratch_shapes=[pltpu.VMEM((2, tmc, D), F32), pltpu.SemaphoreType.DMA(())],
        compiler_params=_cparams(("arbitrary",)),
        name="combine",
    )(dest3, gates, x1, yb)


def _mixer(x2, B, T, norm_w, w_in, mu, w0, w2, a0, a2, g2, k_k, k_a, r_k, ln_w, ln_b, b_f, qn_w, kn_w):
    rw_heads = w0.shape[0] // HEAD
    fox_heads = b_f.shape[0]
    Wr, Wf = rw_heads * HEAD, fox_heads * HEAD
    lora = w2.shape[0] + a2.shape[0] + g2.shape[0]
    rw_cols = 3 * Wr + lora
    o_w, o_k, o_v, o_a = Wr, Wr + w2.shape[0], 2 * Wr + w2.shape[0], 3 * Wr + w2.shape[0]
    perm = np.concatenate([np.arange(0, Wr), np.arange(o_k, o_k + Wr), np.arange(o_v, o_v + Wr),
                           np.arange(o_w, o_w + w2.shape[0]), np.arange(o_a, rw_cols)])
    w_rw = w_in[:, :rw_cols][:, perm].astype(BF16)
    w_qkv = w_in[:, rw_cols:rw_cols + 3 * Wf].astype(BF16)
    w_f = jnp.pad(w_in[:, rw_cols + 3 * Wf:], ((0, 0), (0, 128 - fox_heads))).astype(BF16)
    p_rw, p_qkv, p_f = _in_proj(x2, norm_w[None, :], w_rw, w_qkv, w_f)

    row = lambda a: a.reshape(1, -1)
    ops = _rw_prep(p_rw.reshape(B, T, rw_cols), row(mu[perm]), row(w0), w2.astype(BF16), row(a0),
                   a2.astype(BF16), g2.astype(BF16), row(k_k), row(k_a), row(r_k), rw_heads)
    y_rw = _rw_scan(*ops, row(ln_w), row(ln_b), rw_heads)

    tile_w = lambda w: row(jnp.tile(w, fox_heads))
    q, k, v, c, ct = _fox_prep(p_qkv.reshape(B, T, 3 * Wf), p_f.reshape(B, T, 128),
                               row(jnp.pad(b_f, (0, 128 - fox_heads))), tile_w(qn_w), tile_w(kn_w), fox_heads)
    y_fox = _fox_attn(q, k, v, c, ct, fox_heads)
    return y_rw.reshape(B * T, Wr), y_fox.reshape(B * T, Wf)


def _moe(y_rw, y_fox, x2, w_out, norm_w, rg_w, rg_b, re_w, re_b, w_gate, w_up, w_down):
    M, D = x2.shape
    Wr = y_rw.shape[1]
    pad = ROUTER_LANES - N_GROUPS - N_EXPERTS
    w_r = jnp.pad(jnp.concatenate([rg_w, re_w], axis=1), ((0, 0), (0, pad)))
    b_r = jnp.pad(jnp.concatenate([rg_b, re_b]), (0, pad))[None, :]
    wr_hi = w_r.astype(BF16)
    wr_lo = (w_r - wr_hi.astype(F32)).astype(BF16)
    x1, h2, eid, gates = _out_route(y_rw, y_fox, x2, w_out[:Wr].astype(BF16), w_out[Wr:].astype(BF16),
                                    norm_w[None, :], wr_hi, wr_lo, b_r)
    rank, counts = _rank(eid)

    rb = 256
    counts = counts[0, :N_EXPERTS]
    padded = (counts + rb - 1) // rb * rb
    pad_end = jnp.cumsum(padded)
    pad_start = pad_end - padded
    n_blocks = (2 * M + N_EXPERTS * (rb - 1) + rb - 1) // rb
    block_expert = jnp.minimum(
        jnp.searchsorted(pad_end, jnp.arange(n_blocks, dtype=I32) * rb, side='right'), N_EXPERTS - 1).astype(I32)
    n_used = (pad_end[-1:] // rb).astype(I32)
    dest = (pad_start[eid[:, :2]] + rank[:, :2]).astype(I32)

    xb = _dispatch(dest, h2, n_blocks * rb)
    yb = _experts(block_expert, n_used, xb, w_gate, w_up, w_down, rb)
    return _combine(dest, gates, x1, yb)


def kernel(x, norm_mix_w, w_in, mu_shift, rw_w0, rw_w2, rw_a0, rw_a2, rw_g2, rw_k_k, rw_k_a, rw_r_k, rw_ln_w, rw_ln_b, fox_b_f, fox_q_norm_w, fox_k_norm_w, w_out, norm_ffn_w, router_group_w, router_group_b, router_expert_w, router_expert_b, exp_w_gate, exp_w_up, exp_w_down):
    B, T, D = x.shape
    x2 = x.reshape(B * T, D)
    for l in range(w_in.shape[0]):
        y_rw, y_fox = _mixer(x2, B, T, norm_mix_w[l], w_in[l], mu_shift[l], rw_w0[l], rw_w2[l], rw_a0[l],
                             rw_a2[l], rw_g2[l], rw_k_k[l], rw_k_a[l], rw_r_k[l].reshape(-1), rw_ln_w[l],
                             rw_ln_b[l], fox_b_f[l], fox_q_norm_w[l], fox_k_norm_w[l])
        x2 = _moe(y_rw, y_fox, x2, w_out[l], norm_ffn_w[l], router_group_w[l], router_group_b[l],
                  router_expert_w[l], router_expert_b[l], exp_w_gate[l], exp_w_up[l], exp_w_down[l])
    return x2.reshape(B, T, D)
```

```python
import functools

import jax
import jax.numpy as jnp
import numpy as np
from jax import lax
from jax.experimental import pallas as pl
from jax.experimental.pallas import tpu as pltpu

F32, BF16, I32 = jnp.float32, jnp.bfloat16, jnp.int32

HEAD = 64
CHUNK = 64
GROUP = 4
GW = GROUP * HEAD
SCAN_BATCH = 2
N_GROUPS = 8
EXPERTS_PER_GROUP = 8
N_EXPERTS = N_GROUPS * EXPERTS_PER_GROUP
ROUTER_LANES = 128
ROW_BLOCK = 256
NORM_EPS = 1e-6
GN_EPS = 64e-5
NEG_BIG = -1e30
LOG2E = 1.4426950408889634
VMEM_LIMIT = 56 * 1024 * 1024


def _cparams(sem):
    return pltpu.CompilerParams(dimension_semantics=sem, vmem_limit_bytes=VMEM_LIMIT)


def _dot(a, b):
    return jnp.dot(a, b, preferred_element_type=F32)


def _dot_nt(a, b):
    return lax.dot_general(a, b, (((1,), (1,)), ((), ())), preferred_element_type=F32)


def _dot_tn(a, b):
    return lax.dot_general(a, b, (((0,), (0,)), ((), ())), preferred_element_type=F32)


def _split3(x):
    hi = x.astype(BF16)
    r1 = x - hi.astype(F32)
    mid = r1.astype(BF16)
    lo = (r1 - mid.astype(F32)).astype(BF16)
    return hi, mid, lo


def _sigmoid(z):
    return 1.0 / (1.0 + jnp.exp(-z))


def _head_sum(x, n_heads):
    lane_head = lax.broadcasted_iota(I32, x.shape, 1) // HEAD
    out = jnp.zeros_like(x)
    for h in range(n_heads):
        m = lane_head == h
        s = jnp.sum(jnp.where(m, x, 0.0), axis=-1, keepdims=True)
        out = jnp.where(m, s, out)
    return out


def _in_proj_kernel(x_ref, nw_ref, wrw_ref, wqkv_ref, wf_ref, prw_ref, pqkv_ref, pf_ref):
    x = x_ref[...]
    h = x * lax.rsqrt(jnp.mean(x * x, axis=-1, keepdims=True) + NORM_EPS) * nw_ref[...]
    hb = h.astype(BF16)
    prw_ref[...] = _dot(hb, wrw_ref[...])
    pqkv_ref[...] = _dot(hb, wqkv_ref[...])
    pf_ref[...] = _dot(hb, wf_ref[...])


def _in_proj(x2, norm_w, w_rw, w_qkv, w_f):
    M, D = x2.shape
    tm = min(512, M)
    n_rw, n_qkv, n_f = w_rw.shape[1], w_qkv.shape[1], w_f.shape[1]
    full = lambda shape: pl.BlockSpec(shape, lambda i: (0, 0))
    return pl.pallas_call(
        _in_proj_kernel,
        grid=(M // tm,),
        in_specs=[pl.BlockSpec((tm, D), lambda i: (i, 0)), full((1, D)),
                  full((D, n_rw)), full((D, n_qkv)), full((D, n_f))],
        out_specs=[pl.BlockSpec((tm, n_rw), lambda i: (i, 0)),
                   pl.BlockSpec((tm, n_qkv), lambda i: (i, 0)),
                   pl.BlockSpec((tm, n_f), lambda i: (i, 0))],
        out_shape=[jax.ShapeDtypeStruct((M, n_rw), F32),
                   jax.ShapeDtypeStruct((M, n_qkv), F32),
                   jax.ShapeDtypeStruct((M, n_f), F32)],
        compiler_params=_cparams(("parallel",)),
        name="in_proj",
    )(x2, norm_w, w_rw, w_qkv, w_f)


def _rw_prep_kernel(n_heads, p_ref, pprev_ref, mu_ref, w0_ref, w2_ref, a0_ref, a2_ref, g2_ref,
                    kk_ref, ka_ref, rk_ref,
                    rt_ref, at_ref, bt_ref, kt_ref, bb_ref, kb_ref, v_ref, dl_ref, bonus_ref, g_ref):
    W = n_heads * HEAD
    j = pl.program_id(1)
    p = p_ref[0]
    tt = p.shape[0]
    last_prev = jnp.where(j > 0, pprev_ref[0, 7:8, :], 0.0)
    row = lax.broadcasted_iota(I32, p.shape, 0)
    prev = jnp.where(row == 0, last_prev, pltpu.roll(p, 1, axis=0))
    ps = p + (prev - p) * mu_ref[...]
    r, k, v = ps[:, 0:W], ps[:, W:2 * W], ps[:, 2 * W:3 * W]
    o = 3 * W
    pw, pa, pg = ps[:, o:o + 64], ps[:, o + 64:o + 128], ps[:, o + 128:o + 256]

    z = w0_ref[...] + _dot(jnp.tanh(pw).astype(BF16), w2_ref[...])
    lw = (-np.exp(-0.5)).astype(np.float32) * _sigmoid(z)
    a_sig = _sigmoid(a0_ref[...] + _dot(pa.astype(BF16), a2_ref[...]))
    g_ref[0] = _dot(_sigmoid(pg).astype(BF16), g2_ref[...])

    kk = k * kk_ref[...]
    kk = kk / jnp.maximum(jnp.sqrt(_head_sum(kk * kk, n_heads)), 1e-12)
    km = k * (1.0 + (a_sig - 1.0) * ka_ref[...])
    a_vec = -kk
    b_vec = kk * a_sig
    bonus_ref[0] = _head_sum(r * km * rk_ref[...], n_heads) * v
    v_ref[0] = v.astype(BF16)

    ti = lax.broadcasted_iota(I32, (tt, tt), 0)
    si = lax.broadcasted_iota(I32, (tt, tt), 1)
    same = (ti // CHUNK) == (si // CHUNK)
    tri = jnp.where(same & (ti >= si), 1.0, 0.0).astype(BF16)
    ones = jnp.where(same, 1.0, 0.0).astype(BF16)
    nck = tt // CHUNK
    ci = lax.broadcasted_iota(I32, (nck, tt), 0)
    cs = lax.broadcasted_iota(I32, (nck, tt), 1)
    sel = jnp.where(ci == cs // CHUNK, 1.0, 0.0).astype(BF16)
    hi, mid, lo = _split3(lw)
    cum = _dot(tri, hi) + _dot(tri, mid) + _dot(tri, lo)
    tot = _dot(ones, hi) + _dot(ones, mid) + _dot(ones, lo)
    dl = jnp.exp(_dot(sel, hi) + _dot(sel, mid) + _dot(sel, lo))
    for ck in range(nck):
        dl_ref[0, ck] = dl[ck:ck + 1, :]

    e_in = jnp.exp(cum)
    e_ex = jnp.exp(cum - lw)
    e_inv = jnp.exp(-cum)
    e_bar = jnp.exp(tot - cum)
    rt_ref[0] = (r * e_in).astype(BF16)
    at_ref[0] = (a_vec * e_ex).astype(BF16)
    bt_ref[0] = (b_vec * e_inv).astype(BF16)
    kt_ref[0] = (km * e_inv).astype(BF16)
    bb_ref[0] = (b_vec * e_bar).astype(BF16)
    kb_ref[0] = (km * e_bar).astype(BF16)


def _rw_prep(p_rw, mu, w0, w2, a0, a2, g2, k_k, k_a, r_k, n_heads):
    B, T, P = p_rw.shape
    W = n_heads * HEAD
    tt = min(512, T)
    row = lambda a: pl.BlockSpec(a.shape, lambda b, j: (0, 0))
    tile = lambda w: pl.BlockSpec((1, tt, w), lambda b, j: (b, j, 0))
    bf = jax.ShapeDtypeStruct((B, T, W), BF16)
    f32 = jax.ShapeDtypeStruct((B, T, W), F32)
    return pl.pallas_call(
        functools.partial(_rw_prep_kernel, n_heads),
        grid=(B, T // tt),
        in_specs=[tile(P),
                  pl.BlockSpec((1, 8, P), lambda b, j: (b, jnp.maximum(j * (tt // 8) - 1, 0), 0)),
                  row(mu), row(w0), row(w2), row(a0), row(a2), row(g2), row(k_k), row(k_a), row(r_k)],
        out_specs=[tile(W)] * 7 + [pl.BlockSpec((1, tt // CHUNK, 1, W), lambda b, j: (b, j, 0, 0)), tile(W), tile(W)],
        out_shape=[bf] * 7 + [jax.ShapeDtypeStruct((B, T // CHUNK, 1, W), F32), f32, f32],
        compiler_params=_cparams(("parallel", "parallel")),
        name="rw_prep",
    )(p_rw, p_rw, mu, w0, w2, a0, a2, g2, k_k, k_a, r_k)


def _rw_scan_kernel(rt_ref, at_ref, bt_ref, kt_ref, bb_ref, kb_ref, v_ref, dl_ref,
                    bonus_ref, g_ref, lnw_ref, lnb_ref, y_ref, s_ref):
    C = CHUNK
    nb, tt, W = rt_ref.shape
    chains = [(b, g) for b in range(nb) for g in range(W // GW)]

    @pl.when(pl.program_id(1) == 0)
    def _():
        s_ref[...] = jnp.zeros_like(s_ref)

    ri = lax.broadcasted_iota(I32, (GW, GW), 0)
    ci = lax.broadcasted_iota(I32, (GW, GW), 1)
    tr, tc = ri & (C - 1), ci & (C - 1)
    strict, incl, eye = tr > tc, tr >= tc, ri == ci
    lane_head = lax.broadcasted_iota(I32, (C, GW), 1) // HEAD

    def bd(x):
        return jnp.concatenate([jnp.where(lane_head == h, x, jnp.zeros_like(x)) for h in range(GROUP)], axis=0)

    bf = lambda xs: [x.astype(BF16) for x in xs]

    def chunk(c, carry):
        rows = pl.ds(pl.multiple_of(c * C, C), C)
        ld = lambda ref: [bd(ref[b, rows, g * GW:(g + 1) * GW]) for b, g in chains]
        Rt, At, Bt, Kt, Bb, Kb, V = (ld(r) for r in (rt_ref, at_ref, bt_ref, kt_ref, bb_ref, kb_ref, v_ref))
        Lab = [jnp.where(strict, _dot_nt(a, b), 0.0) for a, b in zip(At, Bt)]
        Lak = bf([jnp.where(strict, _dot_nt(a, k), 0.0) for a, k in zip(At, Kt)])
        Mrb = bf([jnp.where(incl, _dot_nt(r, b), 0.0) for r, b in zip(Rt, Bt)])
        Mrk = bf([jnp.where(incl, _dot_nt(r, k), 0.0) for r, k in zip(Rt, Kt)])
        P = [jnp.where(eye, 1.0, l) for l in Lab]
        Lp = Lab
        for _ in range(5):
            Lpb = bf(Lp)
            Lp = [_dot(x, x) for x in Lpb]
            P = [p + _dot(pb, lb) for p, pb, lb in zip(P, bf(P), bf(Lp))]
        Pb = bf(P)
        Wk = bf([_dot(l, v) for l, v in zip(Lak, V)])
        Ah = [_dot(p, a) for p, a in zip(Pb, At)]
        U0 = [_dot(p, w) for p, w in zip(Pb, Wk)]
        Ahb, U0b = bf(Ah), bf(U0)
        Rh = bf([r.astype(F32) + _dot(m, a) for r, m, a in zip(Rt, Mrb, Ahb)])
        Y0 = [_dot(mb, u) + _dot(mk, v) for mb, u, mk, v in zip(Mrb, U0b, Mrk, V)]
        GT = bf([_dot_tn(a, b) for a, b in zip(Ahb, Bb)])
        HT = [_dot_tn(u, b) + _dot_tn(v, k) for u, b, v, k in zip(U0b, Bb, V, Kb)]
        S0 = [s_ref[i] for i in range(len(chains))]
        S0b = bf(S0)
        Y = [_dot_nt(r, s) + y0 for r, s, y0 in zip(Rh, S0b, Y0)]
        for i, (b, g) in enumerate(chains):
            ls = slice(g * GW, (g + 1) * GW)
            s_ref[i] = S0[i] * dl_ref[b, c, :, ls] + _dot(S0b[i], GT[i]) + HT[i]
            y = Y[i][0:C] + Y[i][C:2 * C] + Y[i][2 * C:3 * C] + Y[i][3 * C:4 * C]
            mean = _head_sum(y, GROUP) * (1.0 / HEAD)
            yc = y - mean
            var = _head_sum(yc * yc, GROUP) * (1.0 / HEAD)
            yn = yc * lax.rsqrt(var + GN_EPS) * lnw_ref[:, ls] + lnb_ref[:, ls]
            y_ref[b, rows, ls] = (yn + bonus_ref[b, rows, ls]) * g_ref[b, rows, ls]
        return carry

    lax.fori_loop(0, tt // C, chunk, 0)


def _rw_scan(rt, at, bt, kt, bb, kb, v, dl, bonus, g, ln_w, ln_b):
    B, T, W = rt.shape
    nb = min(SCAN_BATCH, B)
    tt = min(256, T)
    tile = pl.BlockSpec((nb, tt, W), lambda b, j: (b, j, 0))
    row = pl.BlockSpec((1, W), lambda b, j: (0, 0))
    return pl.pallas_call(
        _rw_scan_kernel,
        grid=(B // nb, T // tt),
        in_specs=[tile] * 7 + [pl.BlockSpec((nb, tt // CHUNK, 1, W), lambda b, j: (b, j, 0, 0)), tile, tile, row, row],
        out_specs=tile,
        out_shape=jax.ShapeDtypeStruct((B, T, W), F32),
        scratch_shapes=[pltpu.VMEM((nb * (W // GW), GW, GW), F32)],
        compiler_params=_cparams(("parallel", "arbitrary")),
        name="rw_scan",
    )(rt, at, bt, kt, bb, kb, v, dl, bonus, g, ln_w, ln_b)


def _fox_prep_kernel(n_heads, qkv_ref, f_ref, bf_ref, qnw_ref, knw_ref,
                     q_ref, k_ref, vt_ref, carry_ref):
    W = n_heads * HEAD
    tt = qkv_ref.shape[1]

    @pl.when(pl.program_id(1) == 0)
    def _():
        carry_ref[...] = jnp.zeros_like(carry_ref)

    qkv = qkv_ref[0]
    q, k, v = qkv[:, 0:W], qkv[:, W:2 * W], qkv[:, 2 * W:3 * W]
    inv_n = 1.0 / HEAD
    qn = q * lax.rsqrt(_head_sum(q * q, n_heads) * inv_n + NORM_EPS) * qnw_ref[...]
    kn = k * lax.rsqrt(_head_sum(k * k, n_heads) * inv_n + NORM_EPS) * knw_ref[...]
    qb = (qn * (HEAD ** -0.5 * LOG2E)).astype(BF16)
    kb = kn.astype(BF16)
    for p in range(n_heads // 2):
        vt_ref[0, p, 0] = v[:, p * 128:(p + 1) * 128].T.astype(BF16)

    zf = f_ref[0] + bf_ref[...]
    logf = jnp.minimum(zf, 0.0) - jnp.log(1.0 + jnp.exp(-jnp.abs(zf)))
    ti = lax.broadcasted_iota(I32, (tt, tt), 0)
    si = lax.broadcasted_iota(I32, (tt, tt), 1)
    tri = jnp.where(ti >= si, 1.0, 0.0).astype(BF16)
    hi, mid, lo = _split3(logf)
    c = carry_ref[...] + _dot(tri, hi) + _dot(tri, mid) + _dot(tri, lo)
    carry_ref[...] = c[tt - 1:tt, :]
    c_hi, c_mid, c_lo = _split3(c * (-LOG2E))

    src_w = lax.broadcasted_iota(I32, (W, 128), 0)
    dst_w = lax.broadcasted_iota(I32, (W, 128), 1)
    src_c = lax.broadcasted_iota(I32, (128, 128), 0)
    dst_c = lax.broadcasted_iota(I32, (128, 128), 1)
    lane = lax.broadcasted_iota(I32, (1, 128), 1)
    ones = jnp.where((lane >= HEAD) & (lane < HEAD + 3), 1.0, 0.0)
    for h in range(n_heads):
        sel_w = jnp.where((src_w == dst_w + h * HEAD) & (dst_w < HEAD), 1.0, 0.0).astype(BF16)
        sel_c = [jnp.where((src_c == h) & (dst_c == HEAD + j), 1.0, 0.0).astype(BF16) for j in range(3)]
        q_ref[0, h] = (_dot(qb, sel_w) + ones).astype(BF16)
        k_ref[0, h] = (_dot(kb, sel_w) + _dot(c_hi, sel_c[0]) + _dot(c_mid, sel_c[1])
                       + _dot(c_lo, sel_c[2])).astype(BF16)


def _fox_prep(qkv, f, b_f, qn_w, kn_w, n_heads):
    B, T, _ = qkv.shape
    W = n_heads * HEAD
    tt = min(256, T)
    row = lambda a: pl.BlockSpec(a.shape, lambda b, j: (0, 0))
    aug = jax.ShapeDtypeStruct((B, n_heads, T, 128), BF16)
    aug_spec = pl.BlockSpec((1, n_heads, tt, 128), lambda b, j: (b, 0, j, 0))
    return pl.pallas_call(
        functools.partial(_fox_prep_kernel, n_heads),
        grid=(B, T // tt),
        in_specs=[pl.BlockSpec((1, tt, 3 * W), lambda b, j: (b, j, 0)),
                  pl.BlockSpec((1, tt, 128), lambda b, j: (b, j, 0)),
                  row(b_f), row(qn_w), row(kn_w)],
        out_specs=[aug_spec, aug_spec,
                   pl.BlockSpec((1, n_heads // 2, 1, 128, tt), lambda b, j: (b, 0, j, 0, 0))],
        out_shape=[aug, aug, jax.ShapeDtypeStruct((B, n_heads // 2, T // tt, 128, tt), BF16)],
        scratch_shapes=[pltpu.VMEM((1, 128), F32)],
        compiler_params=_cparams(("parallel", "arbitrary")),
        name="fox_prep",
    )(qkv, f, b_f, qn_w, kn_w)


def _fox_attn_kernel(q_ref, k_ref, vt_ref, o_ref):
    i = pl.program_id(2)
    tq = q_ref.shape[2]
    sub = vt_ref.shape[4]
    q = [q_ref[0, hh] for hh in range(2)]
    ki = lax.broadcasted_iota(I32, (tq, tq), 0)
    qi = lax.broadcasted_iota(I32, (tq, tq), 1)

    def scores(j, diag=False):
        k0 = pl.multiple_of(j * tq, tq)
        s = [_dot_nt(k_ref[0, hh, pl.ds(k0, tq), :], q[hh]) for hh in range(2)]
        return [jnp.where(ki <= qi, x, -jnp.inf) for x in s] if diag else s

    def update(j, s, carry):
        out = []
        for hh in range(2):
            m, l, acc = carry[hh]
            m_new = jnp.maximum(m, jnp.max(s[hh], axis=0, keepdims=True))
            alpha = jnp.exp2(m - m_new)
            p = jnp.exp2(s[hh] - m_new)
            l = alpha * l + jnp.sum(p, axis=0, keepdims=True)
            p = p.astype(BF16)
            pv = sum(_dot(vt_ref[0, 0, j * (tq // sub) + u], p[u * sub:(u + 1) * sub]) for u in range(tq // sub))
            out.append((m_new, l, alpha * acc + pv))
        return tuple(out)

    def two_blocks(jj, carry):
        sa, sb = scores(2 * jj), scores(2 * jj + 1)
        return update(2 * jj + 1, sb, update(2 * jj, sa, carry))

    init = tuple((jnp.full((1, tq), NEG_BIG, F32), jnp.zeros((1, tq), F32), jnp.zeros((128, tq), F32))
                 for _ in range(2))
    carry = lax.fori_loop(0, i // 2, two_blocks, init)
    carry = lax.fori_loop(i - (i & 1), i, lambda j, c: update(j, scores(j), c), carry)
    (_, l0, acc0), (_, l1, acc1) = update(i, scores(i, diag=True), carry)
    row = lax.broadcasted_iota(I32, (128, tq), 0)
    o_ref[0] = jnp.where(row < HEAD, acc0 / l0, acc1 / l1).T


def _fox_attn(q_aug, k_aug, vt):
    B, H, T, _ = q_aug.shape
    tq = min(512, T)
    n_sub, sub = vt.shape[2], vt.shape[4]
    return pl.pallas_call(
        _fox_attn_kernel,
        grid=(B, H // 2, T // tq),
        in_specs=[pl.BlockSpec((1, 2, tq, 128), lambda b, p, i: (b, p, i, 0)),
                  pl.BlockSpec((1, 2, T, 128), lambda b, p, i: (b, p, 0, 0)),
                  pl.BlockSpec((1, 1, n_sub, 128, sub), lambda b, p, i: (b, p, 0, 0, 0))],
        out_specs=pl.BlockSpec((1, tq, 128), lambda b, p, i: (b, i, p)),
        out_shape=jax.ShapeDtypeStruct((B, T, H * HEAD), F32),
        compiler_params=_cparams(("parallel", "parallel", "arbitrary")),
        name="fox_attn",
    )(q_aug, k_aug, vt)


def _out_route_kernel(yrw_ref, yfox_ref, x_ref, wo_rw_ref, wo_fox_ref, nw_ref, wr_hi_ref, wr_lo_ref, br_ref,
                      x1_ref, h2_ref, eid_ref, gate_ref):
    x1 = (x_ref[...] + _dot(yrw_ref[...].astype(BF16), wo_rw_ref[...])
          + _dot(yfox_ref[...].astype(BF16), wo_fox_ref[...]))
    x1_ref[...] = x1
    h2 = x1 * lax.rsqrt(jnp.mean(x1 * x1, axis=-1, keepdims=True) + NORM_EPS) * nw_ref[...]
    h2_ref[...] = h2
    h_hi = h2.astype(BF16)
    h_lo = (h2 - h_hi.astype(F32)).astype(BF16)
    logits = (_dot(h_hi, wr_hi_ref[...]) + _dot(h_hi, wr_lo_ref[...]) + _dot(h_lo, wr_hi_ref[...])) + br_ref[...]

    lane_i = lax.broadcasted_iota(I32, logits.shape, 1)
    lane = lane_i.astype(F32)
    first = lambda mask: jnp.min(jnp.where(mask, lane, 1e9), axis=-1, keepdims=True)
    gl = jnp.where(lane_i < N_GROUPS, logits, -jnp.inf)
    gmax = jnp.max(gl, axis=-1, keepdims=True)
    g_sel = first(gl == gmax)
    g_gate = 1.0 / jnp.sum(jnp.exp(gl - gmax), axis=-1, keepdims=True)
    e_lane = lane_i - N_GROUPS
    lane_grp = jnp.right_shift(e_lane, 3).astype(F32)
    in_grp = (e_lane >= 0) & (e_lane < N_EXPERTS) & (lane_grp == g_sel)
    el = jnp.where(in_grp, logits, -jnp.inf)
    m1 = jnp.max(el, axis=-1, keepdims=True)
    i1 = first(el == m1)
    el2 = jnp.where(lane == i1, -jnp.inf, el)
    m2 = jnp.max(el2, axis=-1, keepdims=True)
    i2 = first(el2 == m2)
    e2 = jnp.exp(m2 - m1)
    g1 = g_gate / (1.0 + e2)
    g2 = g_gate * e2 / (1.0 + e2)
    eid_ref[...] = jnp.where(lane_i == 0, i1 - N_GROUPS, jnp.where(lane_i == 1, i2 - N_GROUPS, 0.0)).astype(I32)
    gate_ref[...] = jnp.where(lane_i == 0, g1, jnp.where(lane_i == 1, g2, 0.0))


def _out_route(y_rw, y_fox, x2, wo_rw, wo_fox, norm_w, wr_hi, wr_lo, b_r):
    M, D = x2.shape
    W = y_rw.shape[1]
    tm = min(256, M)
    full = lambda a: pl.BlockSpec(a.shape, lambda i: (0, 0))
    tile = lambda w: pl.BlockSpec((tm, w), lambda i: (i, 0))
    return pl.pallas_call(
        _out_route_kernel,
        grid=(M // tm,),
        in_specs=[tile(W), tile(W), tile(D), full(wo_rw), full(wo_fox), full(norm_w),
                  full(wr_hi), full(wr_lo), full(b_r)],
        out_specs=[tile(D), tile(D), tile(ROUTER_LANES), tile(ROUTER_LANES)],
        out_shape=[jax.ShapeDtypeStruct((M, D), F32), jax.ShapeDtypeStruct((M, D), F32),
                   jax.ShapeDtypeStruct((M, ROUTER_LANES), I32), jax.ShapeDtypeStruct((M, ROUTER_LANES), F32)],
        compiler_params=_cparams(("parallel",)),
        name="out_route",
    )(y_rw, y_fox, x2, wo_rw, wo_fox, norm_w, wr_hi, wr_lo, b_r)


def _rank_kernel(eid_ref, rank_ref, cnt_ref, carry_ref):
    @pl.when(pl.program_id(0) == 0)
    def _():
        carry_ref[...] = jnp.zeros_like(carry_ref)

    eid = eid_ref[...].astype(F32)
    tm = eid.shape[0]
    lane = lax.broadcasted_iota(I32, eid.shape, 1)
    lane_f = lane.astype(F32)
    pick = lambda l: jnp.sum(jnp.where(lane == l, eid, 0.0), axis=-1, keepdims=True)
    e0, e1 = pick(0), pick(1)
    oh0 = (lane_f == e0).astype(F32)
    oh1 = (lane_f == e1).astype(F32)
    both = oh0 + oh1
    ri = lax.broadcasted_iota(I32, (tm, tm), 0)
    ci = lax.broadcasted_iota(I32, (tm, tm), 1)
    before = _dot(jnp.where(ri > ci, 1.0, 0.0).astype(BF16), both.astype(BF16)) + carry_ref[...]
    r0 = jnp.sum(oh0 * before, axis=-1, keepdims=True)
    r1 = jnp.sum(oh1 * (before + oh0), axis=-1, keepdims=True)
    rank_ref[...] = jnp.where(lane == 0, r0, jnp.where(lane == 1, r1, 0.0)).astype(I32)
    total = carry_ref[...] + jnp.sum(both, axis=0, keepdims=True)
    carry_ref[...] = total
    cnt_ref[...] = jnp.broadcast_to(total, cnt_ref.shape).astype(I32)


def _rank(eid):
    M = eid.shape[0]
    tm = min(512, M)
    return pl.pallas_call(
        _rank_kernel,
        grid=(M // tm,),
        in_specs=[pl.BlockSpec((tm, ROUTER_LANES), lambda i: (i, 0))],
        out_specs=[pl.BlockSpec((tm, ROUTER_LANES), lambda i: (i, 0)),
                   pl.BlockSpec((8, ROUTER_LANES), lambda i: (0, 0))],
        out_shape=[jax.ShapeDtypeStruct((M, ROUTER_LANES), I32), jax.ShapeDtypeStruct((8, ROUTER_LANES), I32)],
        scratch_shapes=[pltpu.VMEM((1, ROUTER_LANES), F32)],
        compiler_params=_cparams(("arbitrary",)),
        name="rank",
    )(eid)


def _invert_kernel(dest_ref, tok_ref):
    def clear(r, carry):
        tok_ref[r] = 0
        return carry

    def put(a, carry):
        tok_ref[dest_ref[a]] = lax.shift_right_logical(a, 1)
        return carry

    lax.fori_loop(0, tok_ref.shape[0], clear, 0, unroll=8)
    lax.fori_loop(0, dest_ref.shape[0], put, 0, unroll=8)


def _invert(dest_flat, n_rows):
    smem = pl.BlockSpec(memory_space=pltpu.SMEM)
    return pl.pallas_call(
        _invert_kernel,
        in_specs=[smem],
        out_specs=smem,
        out_shape=jax.ShapeDtypeStruct((n_rows,), I32),
        name="invert",
    )(dest_flat)


def _gather_kernel(nused_ref, tok_ref, h_hbm, xb_ref, buf, sem):
    i = pl.program_id(0)
    rb = xb_ref.shape[0]
    copy = lambda r: pltpu.make_async_copy(h_hbm.at[pl.ds(tok_ref[0, r], 1)], buf.at[pl.ds(r, 1)], sem)

    @pl.when(i < nused_ref[0])
    def _():
        def start(r, carry):
            copy(r).start()
            return carry

        def wait(r, carry):
            copy(r).wait()
            return carry

        lax.fori_loop(0, rb, start, 0, unroll=8)
        lax.fori_loop(0, rb, wait, 0, unroll=8)
        xb_ref[...] = buf[...].astype(BF16)

    @pl.when(i >= nused_ref[0])
    def _():
        xb_ref[...] = jnp.zeros_like(xb_ref)


def _gather(n_used, tok, h2, rb):
    M, D = h2.shape
    nb = tok.shape[0] // rb
    grid_spec = pltpu.PrefetchScalarGridSpec(
        num_scalar_prefetch=1,
        grid=(nb,),
        in_specs=[pl.BlockSpec((None, 1, rb), lambda i, nu: (i, 0, 0), memory_space=pltpu.SMEM),
                  pl.BlockSpec(memory_space=pl.ANY)],
        out_specs=pl.BlockSpec((rb, D), lambda i, nu: (i, 0)),
        scratch_shapes=[pltpu.VMEM((rb, D), F32), pltpu.SemaphoreType.DMA(())],
    )
    return pl.pallas_call(
        _gather_kernel,
        grid_spec=grid_spec,
        out_shape=jax.ShapeDtypeStruct((nb * rb, D), BF16),
        compiler_params=_cparams(("arbitrary",)),
        name="gather",
    )(n_used, tok.reshape(nb, 1, rb), h2)


def _experts_kernel(be_ref, nused_ref, xb_ref, wg_ref, wu_ref, wd_ref, yb_ref, wg_bf, wu_bf, wd_bf):
    i = pl.program_id(0)
    prev = be_ref[jnp.maximum(i - 1, 0)]

    @pl.when((i == 0) | (be_ref[i] != prev))
    def _():
        wg_bf[...] = wg_ref[0].astype(BF16)
        wu_bf[...] = wu_ref[0].astype(BF16)
        wd_bf[...] = wd_ref[0].astype(BF16)

    @pl.when(i < nused_ref[0])
    def _():
        xb = xb_ref[...]
        gate = _dot(xb, wg_bf[...])
        up = _dot(xb, wu_bf[...])
        hid = gate * _sigmoid(gate) * up
        yb_ref[...] = _dot(hid.astype(BF16), wd_bf[...])

    @pl.when(i >= nused_ref[0])
    def _():
        yb_ref[...] = jnp.zeros_like(yb_ref)


def _experts(block_expert, n_used, xb, w_gate, w_up, w_down, rb):
    n_rows, D = xb.shape
    E, _, F = w_gate.shape
    nb = n_rows // rb
    grid_spec = pltpu.PrefetchScalarGridSpec(
        num_scalar_prefetch=2,
        grid=(nb,),
        in_specs=[pl.BlockSpec((rb, D), lambda i, be, nu: (i, 0)),
                  pl.BlockSpec((1, D, F), lambda i, be, nu: (be[i], 0, 0)),
                  pl.BlockSpec((1, D, F), lambda i, be, nu: (be[i], 0, 0)),
                  pl.BlockSpec((1, F, D), lambda i, be, nu: (be[i], 0, 0))],
        out_specs=pl.BlockSpec((rb, D), lambda i, be, nu: (i, 0)),
        scratch_shapes=[pltpu.VMEM((D, F), BF16), pltpu.VMEM((D, F), BF16), pltpu.VMEM((F, D), BF16)],
    )
    return pl.pallas_call(
        _experts_kernel,
        grid_spec=grid_spec,
        out_shape=jax.ShapeDtypeStruct((n_rows, D), F32),
        compiler_params=_cparams(("arbitrary",)),
        name="experts",
    )(block_expert, n_used, xb, w_gate, w_up, w_down)


def _combine_kernel(dest_ref, gate_ref, x1_ref, yb_hbm, out_ref, buf, sem):
    tmc = x1_ref.shape[0]
    copy = lambda t, s: pltpu.make_async_copy(
        yb_hbm.at[pl.ds(dest_ref[0, 2 * t + s], 1)], buf.at[s, pl.ds(t, 1)], sem)

    def start(t, carry):
        copy(t, 0).start()
        copy(t, 1).start()
        return carry

    def wait(t, carry):
        copy(t, 0).wait()
        copy(t, 1).wait()
        return carry

    lax.fori_loop(0, tmc, start, 0, unroll=8)
    lax.fori_loop(0, tmc, wait, 0, unroll=8)
    gate = gate_ref[...]
    out_ref[...] = x1_ref[...] + gate[:, 0:1] * buf[0] + gate[:, 1:2] * buf[1]


def _combine(dest, gates, x1, yb):
    M, D = x1.shape
    tmc = min(256, M)
    dest3 = dest.reshape(M // tmc, 1, 2 * tmc)
    return pl.pallas_call(
        _combine_kernel,
        grid=(M // tmc,),
        in_specs=[pl.BlockSpec((None, 1, 2 * tmc), lambda i: (i, 0, 0), memory_space=pltpu.SMEM),
                  pl.BlockSpec((tmc, ROUTER_LANES), lambda i: (i, 0)),
                  pl.BlockSpec((tmc, D), lambda i: (i, 0)),
                  pl.BlockSpec(memory_space=pl.ANY)],
        out_specs=pl.BlockSpec((tmc, D), lambda i: (i, 0)),
        out_shape=jax.ShapeDtypeStruct((M, D), F32),
        scratch_shapes=[pltpu.VMEM((2, tmc, D), F32), pltpu.SemaphoreType.DMA(())],
        compiler_params=_cparams(("arbitrary",)),
        name="combine",
    )(dest3, gates, x1, yb)


def _mixer(x2, B, T, norm_w, w_in, mu, w0, w2, a0, a2, g2, k_k, k_a, r_k, ln_w, ln_b, b_f, qn_w, kn_w):
    rw_heads = w0.shape[0] // HEAD
    fox_heads = b_f.shape[0]
    Wr, Wf = rw_heads * HEAD, fox_heads * HEAD
    lora = w2.shape[0] + a2.shape[0] + g2.shape[0]
    rw_cols = 3 * Wr + lora
    o_w, o_k, o_v, o_a = Wr, Wr + w2.shape[0], 2 * Wr + w2.shape[0], 3 * Wr + w2.shape[0]
    perm = np.concatenate([np.arange(0, Wr), np.arange(o_k, o_k + Wr), np.arange(o_v, o_v + Wr),
                           np.arange(o_w, o_w + w2.shape[0]), np.arange(o_a, rw_cols)])
    w_rw = w_in[:, :rw_cols][:, perm].astype(BF16)
    w_qkv = w_in[:, rw_cols:rw_cols + 3 * Wf].astype(BF16)
    w_f = jnp.pad(w_in[:, rw_cols + 3 * Wf:], ((0, 0), (0, 128 - fox_heads))).astype(BF16)
    p_rw, p_qkv, p_f = _in_proj(x2, norm_w[None, :], w_rw, w_qkv, w_f)

    row = lambda a: a.reshape(1, -1)
    ops = _rw_prep(p_rw.reshape(B, T, rw_cols), row(mu[perm]), row(w0), w2.astype(BF16), row(a0),
                   a2.astype(BF16), g2.astype(BF16), row(k_k), row(k_a), row(r_k), rw_heads)
    y_rw = _rw_scan(*ops, row(ln_w), row(ln_b))

    tile_w = lambda w: row(jnp.tile(w, fox_heads))
    q_aug, k_aug, vt = _fox_prep(p_qkv.reshape(B, T, 3 * Wf), p_f.reshape(B, T, 128),
                                 row(jnp.pad(b_f, (0, 128 - fox_heads))), tile_w(qn_w), tile_w(kn_w), fox_heads)
    y_fox = _fox_attn(q_aug, k_aug, vt)
    return y_rw.reshape(B * T, Wr), y_fox.reshape(B * T, Wf)


def _moe(y_rw, y_fox, x2, w_out, norm_w, rg_w, rg_b, re_w, re_b, w_gate, w_up, w_down):
    M, D = x2.shape
    Wr = y_rw.shape[1]
    pad = ROUTER_LANES - N_GROUPS - N_EXPERTS
    w_r = jnp.pad(jnp.concatenate([rg_w, re_w], axis=1), ((0, 0), (0, pad)))
    b_r = jnp.pad(jnp.concatenate([rg_b, re_b]), (0, pad))[None, :]
    wr_hi = w_r.astype(BF16)
    wr_lo = (w_r - wr_hi.astype(F32)).astype(BF16)
    x1, h2, eid, gates = _out_route(y_rw, y_fox, x2, w_out[:Wr].astype(BF16), w_out[Wr:].astype(BF16),
                                    norm_w[None, :], wr_hi, wr_lo, b_r)
    rank, counts = _rank(eid)

    rb = ROW_BLOCK
    counts = counts[0, :N_EXPERTS]
    padded = (counts + rb - 1) // rb * rb
    pad_end = jnp.cumsum(padded)
    pad_start = pad_end - padded
    n_blocks = (2 * M + N_EXPERTS * (rb - 1) + rb - 1) // rb
    block_start = jnp.arange(n_blocks, dtype=I32) * rb
    block_expert = jnp.minimum(jnp.sum(pad_end[None, :] <= block_start[:, None], axis=1), N_EXPERTS - 1).astype(I32)
    n_used = (pad_end[-1:] // rb).astype(I32)
    dest = (pad_start[eid[:, :2]] + rank[:, :2]).astype(I32)

    tok = _invert(dest.reshape(-1), n_blocks * rb)
    xb = _gather(n_used, tok, h2, rb)
    yb = _experts(block_expert, n_used, xb, w_gate, w_up, w_down, rb)
    return _combine(dest, gates, x1, yb)


def kernel(x, norm_mix_w, w_in, mu_shift, rw_w0, rw_w2, rw_a0, rw_a2, rw_g2, rw_k_k, rw_k_a, rw_r_k, rw_ln_w, rw_ln_b, fox_b_f, fox_q_norm_w, fox_k_norm_w, w_out, norm_ffn_w, router_group_w, router_group_b, router_expert_w, router_expert_b, exp_w_gate, exp_w_up, exp_w_down):
    B, T, D = x.shape
    x2 = x.reshape(B * T, D)
    for l in range(w_in.shape[0]):
        y_rw, y_fox = _mixer(x2, B, T, norm_mix_w[l], w_in[l], mu_shift[l], rw_w0[l], rw_w2[l], rw_a0[l],
                             rw_a2[l], rw_g2[l], rw_k_k[l], rw_k_a[l], rw_r_k[l].reshape(-1), rw_ln_w[l],
                             rw_ln_b[l], fox_b_f[l], fox_q_norm_w[l], fox_k_norm_w[l])
        x2 = _moe(y_rw, y_fox, x2, w_out[l], norm_ffn_w[l], router_group_w[l], router_group_b[l],
                  router_expert_w[l], router_expert_b[l], exp_w_gate[l], exp_w_up[l], exp_w_down[l])
    return x2.reshape(B, T, D)
```

```python
import functools

import jax
import jax.numpy as jnp
import numpy as np
from jax import lax
from jax.experimental import pallas as pl
from jax.experimental.pallas import tpu as pltpu

F32, BF16, I32 = jnp.float32, jnp.bfloat16, jnp.int32

HEAD = 64
CHUNK = 64
GROUP = 4
GW = GROUP * HEAD
SCAN_BATCH = 2
N_GROUPS = 8
EXPERTS_PER_GROUP = 8
N_EXPERTS = N_GROUPS * EXPERTS_PER_GROUP
ROUTER_LANES = 128
ROW_BLOCK = 256
NORM_EPS = 1e-6
GN_EPS = 64e-5
NEG_BIG = -1e30
LOG2E = 1.4426950408889634
VMEM_LIMIT = 56 * 1024 * 1024


def _cparams(sem):
    return pltpu.CompilerParams(dimension_semantics=sem, vmem_limit_bytes=VMEM_LIMIT)


def _dot(a, b):
    return jnp.dot(a, b, preferred_element_type=F32)


def _dot_nt(a, b):
    return lax.dot_general(a, b, (((1,), (1,)), ((), ())), preferred_element_type=F32)


def _dot_tn(a, b):
    return lax.dot_general(a, b, (((0,), (0,)), ((), ())), preferred_element_type=F32)


def _split3(x):
    hi = x.astype(BF16)
    r1 = x - hi.astype(F32)
    mid = r1.astype(BF16)
    lo = (r1 - mid.astype(F32)).astype(BF16)
    return hi, mid, lo


def _sigmoid(z):
    return 1.0 / (1.0 + jnp.exp(-z))


def _head_sum(x, n_heads):
    lane_head = lax.broadcasted_iota(I32, x.shape, 1) // HEAD
    out = jnp.zeros_like(x)
    for h in range(n_heads):
        m = lane_head == h
        s = jnp.sum(jnp.where(m, x, 0.0), axis=-1, keepdims=True)
        out = jnp.where(m, s, out)
    return out


def _in_proj_kernel(x_ref, nw_ref, wrw_ref, wqkv_ref, wf_ref, prw_ref, pqkv_ref, pf_ref):
    x = x_ref[...]
    h = x * lax.rsqrt(jnp.mean(x * x, axis=-1, keepdims=True) + NORM_EPS) * nw_ref[...]
    hb = h.astype(BF16)
    prw_ref[...] = _dot(hb, wrw_ref[...])
    pqkv_ref[...] = _dot(hb, wqkv_ref[...])
    pf_ref[...] = _dot(hb, wf_ref[...])


def _in_proj(x2, norm_w, w_rw, w_qkv, w_f):
    M, D = x2.shape
    tm = min(512, M)
    n_rw, n_qkv, n_f = w_rw.shape[1], w_qkv.shape[1], w_f.shape[1]
    full = lambda shape: pl.BlockSpec(shape, lambda i: (0, 0))
    return pl.pallas_call(
        _in_proj_kernel,
        grid=(M // tm,),
        in_specs=[pl.BlockSpec((tm, D), lambda i: (i, 0)), full((1, D)),
                  full((D, n_rw)), full((D, n_qkv)), full((D, n_f))],
        out_specs=[pl.BlockSpec((tm, n_rw), lambda i: (i, 0)),
                   pl.BlockSpec((tm, n_qkv), lambda i: (i, 0)),
                   pl.BlockSpec((tm, n_f), lambda i: (i, 0))],
        out_shape=[jax.ShapeDtypeStruct((M, n_rw), F32),
                   jax.ShapeDtypeStruct((M, n_qkv), F32),
                   jax.ShapeDtypeStruct((M, n_f), F32)],
        compiler_params=_cparams(("parallel",)),
        name="in_proj",
    )(x2, norm_w, w_rw, w_qkv, w_f)


def _rw_prep_kernel(n_heads, p_ref, pprev_ref, mu_ref, w0_ref, w2_ref, a0_ref, a2_ref, g2_ref,
                    kk_ref, ka_ref, rk_ref,
                    rt_ref, at_ref, bt_ref, kt_ref, bb_ref, kb_ref, v_ref, dl_ref, bonus_ref, g_ref):
    W = n_heads * HEAD
    j = pl.program_id(1)
    p = p_ref[0]
    tt = p.shape[0]
    last_prev = jnp.where(j > 0, pprev_ref[0, 7:8, :], 0.0)
    row = lax.broadcasted_iota(I32, p.shape, 0)
    prev = jnp.where(row == 0, last_prev, pltpu.roll(p, 1, axis=0))
    ps = p + (prev - p) * mu_ref[...]
    r, k, v = ps[:, 0:W], ps[:, W:2 * W], ps[:, 2 * W:3 * W]
    o = 3 * W
    pw, pa, pg = ps[:, o:o + 64], ps[:, o + 64:o + 128], ps[:, o + 128:o + 256]

    z = w0_ref[...] + _dot(jnp.tanh(pw).astype(BF16), w2_ref[...])
    lw = (-np.exp(-0.5)).astype(np.float32) * _sigmoid(z)
    a_sig = _sigmoid(a0_ref[...] + _dot(pa.astype(BF16), a2_ref[...]))
    g_ref[0] = _dot(_sigmoid(pg).astype(BF16), g2_ref[...])

    kk = k * kk_ref[...]
    kk = kk / jnp.maximum(jnp.sqrt(_head_sum(kk * kk, n_heads)), 1e-12)
    km = k * (1.0 + (a_sig - 1.0) * ka_ref[...])
    a_vec = -kk
    b_vec = kk * a_sig
    bonus_ref[0] = _head_sum(r * km * rk_ref[...], n_heads) * v
    v_ref[0] = v.astype(BF16)

    ti = lax.broadcasted_iota(I32, (tt, tt), 0)
    si = lax.broadcasted_iota(I32, (tt, tt), 1)
    same = (ti // CHUNK) == (si // CHUNK)
    tri = jnp.where(same & (ti >= si), 1.0, 0.0).astype(BF16)
    ones = jnp.where(same, 1.0, 0.0).astype(BF16)
    nck = tt // CHUNK
    ci = lax.broadcasted_iota(I32, (nck, tt), 0)
    cs = lax.broadcasted_iota(I32, (nck, tt), 1)
    sel = jnp.where(ci == cs // CHUNK, 1.0, 0.0).astype(BF16)
    hi, mid, lo = _split3(lw)
    cum = _dot(tri, hi) + _dot(tri, mid) + _dot(tri, lo)
    tot = _dot(ones, hi) + _dot(ones, mid) + _dot(ones, lo)
    dl = jnp.exp(_dot(sel, hi) + _dot(sel, mid) + _dot(sel, lo))
    for ck in range(nck):
        dl_ref[0, ck] = dl[ck:ck + 1, :]

    e_in = jnp.exp(cum)
    e_ex = jnp.exp(cum - lw)
    e_inv = jnp.exp(-cum)
    e_bar = jnp.exp(tot - cum)
    rt_ref[0] = (r * e_in).astype(BF16)
    at_ref[0] = (a_vec * e_ex).astype(BF16)
    bt_ref[0] = (b_vec * e_inv).astype(BF16)
    kt_ref[0] = (km * e_inv).astype(BF16)
    bb_ref[0] = (b_vec * e_bar).astype(BF16)
    kb_ref[0] = (km * e_bar).astype(BF16)


def _rw_prep(p_rw, mu, w0, w2, a0, a2, g2, k_k, k_a, r_k, n_heads):
    B, T, P = p_rw.shape
    W = n_heads * HEAD
    tt = min(512, T)
    row = lambda a: pl.BlockSpec(a.shape, lambda b, j: (0, 0))
    tile = lambda w: pl.BlockSpec((1, tt, w), lambda b, j: (b, j, 0))
    bf = jax.ShapeDtypeStruct((B, T, W), BF16)
    f32 = jax.ShapeDtypeStruct((B, T, W), F32)
    return pl.pallas_call(
        functools.partial(_rw_prep_kernel, n_heads),
        grid=(B, T // tt),
        in_specs=[tile(P),
                  pl.BlockSpec((1, 8, P), lambda b, j: (b, jnp.maximum(j * (tt // 8) - 1, 0), 0)),
                  row(mu), row(w0), row(w2), row(a0), row(a2), row(g2), row(k_k), row(k_a), row(r_k)],
        out_specs=[tile(W)] * 7 + [pl.BlockSpec((1, tt // CHUNK, 1, W), lambda b, j: (b, j, 0, 0)), tile(W), tile(W)],
        out_shape=[bf] * 7 + [jax.ShapeDtypeStruct((B, T // CHUNK, 1, W), F32), f32, f32],
        compiler_params=_cparams(("parallel", "parallel")),
        name="rw_prep",
    )(p_rw, p_rw, mu, w0, w2, a0, a2, g2, k_k, k_a, r_k)


def _rw_scan_kernel(rt_ref, at_ref, bt_ref, kt_ref, bb_ref, kb_ref, v_ref, dl_ref,
                    bonus_ref, g_ref, lnw_ref, lnb_ref, y_ref, s_ref):
    C = CHUNK
    nb, tt, W = rt_ref.shape
    chains = [(b, g) for b in range(nb) for g in range(W // GW)]

    @pl.when(pl.program_id(1) == 0)
    def _():
        s_ref[...] = jnp.zeros_like(s_ref)

    ri = lax.broadcasted_iota(I32, (GW, GW), 0)
    ci = lax.broadcasted_iota(I32, (GW, GW), 1)
    tr, tc = ri & (C - 1), ci & (C - 1)
    strict, incl, eye = tr > tc, tr >= tc, ri == ci
    lane_head = lax.broadcasted_iota(I32, (C, GW), 1) // HEAD

    def bd(x):
        return jnp.concatenate([jnp.where(lane_head == h, x, jnp.zeros_like(x)) for h in range(GROUP)], axis=0)

    bf = lambda xs: [x.astype(BF16) for x in xs]

    def chunk(c, carry):
        rows = pl.ds(pl.multiple_of(c * C, C), C)
        ld = lambda ref: [bd(ref[b, rows, g * GW:(g + 1) * GW]) for b, g in chains]
        Rt, At, Bt, Kt, Bb, Kb, V = (ld(r) for r in (rt_ref, at_ref, bt_ref, kt_ref, bb_ref, kb_ref, v_ref))
        Lab = [jnp.where(strict, _dot_nt(a, b), 0.0) for a, b in zip(At, Bt)]
        Lak = bf([jnp.where(strict, _dot_nt(a, k), 0.0) for a, k in zip(At, Kt)])
        Mrb = bf([jnp.where(incl, _dot_nt(r, b), 0.0) for r, b in zip(Rt, Bt)])
        Mrk = bf([jnp.where(incl, _dot_nt(r, k), 0.0) for r, k in zip(Rt, Kt)])
        P = [jnp.where(eye, 1.0, l) for l in Lab]
        Lp = Lab
        for _ in range(5):
            Lpb = bf(Lp)
            Lp = [_dot(x, x) for x in Lpb]
            P = [p + _dot(pb, lb) for p, pb, lb in zip(P, bf(P), bf(Lp))]
        Pb = bf(P)
        Wk = bf([_dot(l, v) for l, v in zip(Lak, V)])
        Ah = [_dot(p, a) for p, a in zip(Pb, At)]
        U0 = [_dot(p, w) for p, w in zip(Pb, Wk)]
        Ahb, U0b = bf(Ah), bf(U0)
        Rh = bf([r.astype(F32) + _dot(m, a) for r, m, a in zip(Rt, Mrb, Ahb)])
        Y0 = [_dot(mb, u) + _dot(mk, v) for mb, u, mk, v in zip(Mrb, U0b, Mrk, V)]
        GT = bf([_dot_tn(a, b) for a, b in zip(Ahb, Bb)])
        HT = [_dot_tn(u, b) + _dot_tn(v, k) for u, b, v, k in zip(U0b, Bb, V, Kb)]
        S0 = [s_ref[i] for i in range(len(chains))]
        S0b = bf(S0)
        Y = [_dot_nt(r, s) + y0 for r, s, y0 in zip(Rh, S0b, Y0)]
        for i, (b, g) in enumerate(chains):
            ls = slice(g * GW, (g + 1) * GW)
            s_ref[i] = S0[i] * dl_ref[b, c, :, ls] + _dot(S0b[i], GT[i]) + HT[i]
            y = Y[i][0:C] + Y[i][C:2 * C] + Y[i][2 * C:3 * C] + Y[i][3 * C:4 * C]
            mean = _head_sum(y, GROUP) * (1.0 / HEAD)
            yc = y - mean
            var = _head_sum(yc * yc, GROUP) * (1.0 / HEAD)
            yn = yc * lax.rsqrt(var + GN_EPS) * lnw_ref[:, ls] + lnb_ref[:, ls]
            y_ref[b, rows, ls] = (yn + bonus_ref[b, rows, ls]) * g_ref[b, rows, ls]
        return carry

    lax.fori_loop(0, tt // C, chunk, 0)


def _rw_scan(rt, at, bt, kt, bb, kb, v, dl, bonus, g, ln_w, ln_b):
    B, T, W = rt.shape
    nb = min(SCAN_BATCH, B)
    tt = min(256, T)
    tile = pl.BlockSpec((nb, tt, W), lambda b, j: (b, j, 0))
    row = pl.BlockSpec((1, W), lambda b, j: (0, 0))
    return pl.pallas_call(
        _rw_scan_kernel,
        grid=(B // nb, T // tt),
        in_specs=[tile] * 7 + [pl.BlockSpec((nb, tt // CHUNK, 1, W), lambda b, j: (b, j, 0, 0)), tile, tile, row, row],
        out_specs=tile,
        out_shape=jax.ShapeDtypeStruct((B, T, W), F32),
        scratch_shapes=[pltpu.VMEM((nb * (W // GW), GW, GW), F32)],
        compiler_params=_cparams(("parallel", "arbitrary")),
        name="rw_scan",
    )(rt, at, bt, kt, bb, kb, v, dl, bonus, g, ln_w, ln_b)


def _fox_prep_kernel(n_heads, qkv_ref, f_ref, bf_ref, qnw_ref, knw_ref,
                     q_ref, k_ref, vt_ref, carry_ref):
    W = n_heads * HEAD
    tt = qkv_ref.shape[1]

    @pl.when(pl.program_id(1) == 0)
    def _():
        carry_ref[...] = jnp.zeros_like(carry_ref)

    qkv = qkv_ref[0]
    q, k, v = qkv[:, 0:W], qkv[:, W:2 * W], qkv[:, 2 * W:3 * W]
    inv_n = 1.0 / HEAD
    qn = q * lax.rsqrt(_head_sum(q * q, n_heads) * inv_n + NORM_EPS) * qnw_ref[...]
    kn = k * lax.rsqrt(_head_sum(k * k, n_heads) * inv_n + NORM_EPS) * knw_ref[...]
    qb = (qn * (HEAD ** -0.5 * LOG2E)).astype(BF16)
    kb = kn.astype(BF16)
    for p in range(n_heads // 2):
        vt_ref[0, p, 0] = v[:, p * 128:(p + 1) * 128].T.astype(BF16)

    zf = f_ref[0] + bf_ref[...]
    logf = jnp.minimum(zf, 0.0) - jnp.log(1.0 + jnp.exp(-jnp.abs(zf)))
    ti = lax.broadcasted_iota(I32, (tt, tt), 0)
    si = lax.broadcasted_iota(I32, (tt, tt), 1)
    tri = jnp.where(ti >= si, 1.0, 0.0).astype(BF16)
    hi, mid, lo = _split3(logf)
    c = carry_ref[...] + _dot(tri, hi) + _dot(tri, mid) + _dot(tri, lo)
    carry_ref[...] = c[tt - 1:tt, :]
    c_hi, c_mid, c_lo = _split3(c * (-LOG2E))

    src_w = lax.broadcasted_iota(I32, (W, 128), 0)
    dst_w = lax.broadcasted_iota(I32, (W, 128), 1)
    src_c = lax.broadcasted_iota(I32, (128, 128), 0)
    dst_c = lax.broadcasted_iota(I32, (128, 128), 1)
    lane = lax.broadcasted_iota(I32, (1, 128), 1)
    ones = jnp.where((lane >= HEAD) & (lane < HEAD + 3), 1.0, 0.0)
    for h in range(n_heads):
        sel_w = jnp.where((src_w == dst_w + h * HEAD) & (dst_w < HEAD), 1.0, 0.0).astype(BF16)
        sel_c = [jnp.where((src_c == h) & (dst_c == HEAD + j), 1.0, 0.0).astype(BF16) for j in range(3)]
        q_ref[0, h] = (_dot(qb, sel_w) + ones).astype(BF16)
        k_ref[0, h] = (_dot(kb, sel_w) + _dot(c_hi, sel_c[0]) + _dot(c_mid, sel_c[1])
                       + _dot(c_lo, sel_c[2])).astype(BF16)


def _fox_prep(qkv, f, b_f, qn_w, kn_w, n_heads):
    B, T, _ = qkv.shape
    W = n_heads * HEAD
    tt = min(256, T)
    row = lambda a: pl.BlockSpec(a.shape, lambda b, j: (0, 0))
    aug = jax.ShapeDtypeStruct((B, n_heads, T, 128), BF16)
    aug_spec = pl.BlockSpec((1, n_heads, tt, 128), lambda b, j: (b, 0, j, 0))
    return pl.pallas_call(
        functools.partial(_fox_prep_kernel, n_heads),
        grid=(B, T // tt),
        in_specs=[pl.BlockSpec((1, tt, 3 * W), lambda b, j: (b, j, 0)),
                  pl.BlockSpec((1, tt, 128), lambda b, j: (b, j, 0)),
                  row(b_f), row(qn_w), row(kn_w)],
        out_specs=[aug_spec, aug_spec,
                   pl.BlockSpec((1, n_heads // 2, 1, 128, tt), lambda b, j: (b, 0, j, 0, 0))],
        out_shape=[aug, aug, jax.ShapeDtypeStruct((B, n_heads // 2, T // tt, 128, tt), BF16)],
        scratch_shapes=[pltpu.VMEM((1, 128), F32)],
        compiler_params=_cparams(("parallel", "arbitrary")),
        name="fox_prep",
    )(qkv, f, b_f, qn_w, kn_w)


def _fox_attn_kernel(q_ref, k_ref, vt_ref, o_ref):
    i = pl.program_id(2)
    tq = q_ref.shape[2]
    sub = vt_ref.shape[4]
    q = [q_ref[0, hh] for hh in range(2)]
    ki = lax.broadcasted_iota(I32, (tq, tq), 0)
    qi = lax.broadcasted_iota(I32, (tq, tq), 1)

    def scores(j, diag=False):
        k0 = pl.multiple_of(j * tq, tq)
        s = [_dot_nt(k_ref[0, hh, pl.ds(k0, tq), :], q[hh]) for hh in range(2)]
        return [jnp.where(ki <= qi, x, -jnp.inf) for x in s] if diag else s

    def update(j, s, carry):
        out = []
        for hh in range(2):
            m, l, acc = carry[hh]
            m_new = jnp.maximum(m, jnp.max(s[hh], axis=0, keepdims=True))
            alpha = jnp.exp2(m - m_new)
            p = jnp.exp2(s[hh] - m_new)
            l = alpha * l + jnp.sum(p, axis=0, keepdims=True)
            p = p.astype(BF16)
            pv = sum(_dot(vt_ref[0, 0, j * (tq // sub) + u], p[u * sub:(u + 1) * sub]) for u in range(tq // sub))
            out.append((m_new, l, alpha * acc + pv))
        return tuple(out)

    def two_blocks(jj, carry):
        sa, sb = scores(2 * jj), scores(2 * jj + 1)
        return update(2 * jj + 1, sb, update(2 * jj, sa, carry))

    init = tuple((jnp.full((1, tq), NEG_BIG, F32), jnp.zeros((1, tq), F32), jnp.zeros((128, tq), F32))
                 for _ in range(2))
    carry = lax.fori_loop(0, i // 2, two_blocks, init)
    carry = lax.fori_loop(i - (i & 1), i, lambda j, c: update(j, scores(j), c), carry)
    (_, l0, acc0), (_, l1, acc1) = update(i, scores(i, diag=True), carry)
    row = lax.broadcasted_iota(I32, (128, tq), 0)
    o_ref[0] = jnp.where(row < HEAD, acc0 / l0, acc1 / l1).T


def _fox_attn(q_aug, k_aug, vt):
    B, H, T, _ = q_aug.shape
    tq = min(512, T)
    n_sub, sub = vt.shape[2], vt.shape[4]
    return pl.pallas_call(
        _fox_attn_kernel,
        grid=(B, H // 2, T // tq),
        in_specs=[pl.BlockSpec((1, 2, tq, 128), lambda b, p, i: (b, p, i, 0)),
                  pl.BlockSpec((1, 2, T, 128), lambda b, p, i: (b, p, 0, 0)),
                  pl.BlockSpec((1, 1, n_sub, 128, sub), lambda b, p, i: (b, p, 0, 0, 0))],
        out_specs=pl.BlockSpec((1, tq, 128), lambda b, p, i: (b, i, p)),
        out_shape=jax.ShapeDtypeStruct((B, T, H * HEAD), F32),
        compiler_params=_cparams(("parallel", "parallel", "arbitrary")),
        name="fox_attn",
    )(q_aug, k_aug, vt)


def _out_route_kernel(yrw_ref, yfox_ref, x_ref, wo_rw_ref, wo_fox_ref, nw_ref, wr_hi_ref, wr_lo_ref, br_ref,
                      x1_ref, h2_ref, eid_ref, gate_ref):
    x1 = (x_ref[...] + _dot(yrw_ref[...].astype(BF16), wo_rw_ref[...])
          + _dot(yfox_ref[...].astype(BF16), wo_fox_ref[...]))
    x1_ref[...] = x1
    h2 = x1 * lax.rsqrt(jnp.mean(x1 * x1, axis=-1, keepdims=True) + NORM_EPS) * nw_ref[...]
    h2_ref[...] = h2
    h_hi = h2.astype(BF16)
    h_lo = (h2 - h_hi.astype(F32)).astype(BF16)
    logits = (_dot(h_hi, wr_hi_ref[...]) + _dot(h_hi, wr_lo_ref[...]) + _dot(h_lo, wr_hi_ref[...])) + br_ref[...]

    lane_i = lax.broadcasted_iota(I32, logits.shape, 1)
    lane = lane_i.astype(F32)
    first = lambda mask: jnp.min(jnp.where(mask, lane, 1e9), axis=-1, keepdims=True)
    gl = jnp.where(lane_i < N_GROUPS, logits, -jnp.inf)
    gmax = jnp.max(gl, axis=-1, keepdims=True)
    g_sel = first(gl == gmax)
    g_gate = 1.0 / jnp.sum(jnp.exp(gl - gmax), axis=-1, keepdims=True)
    e_lane = lane_i - N_GROUPS
    lane_grp = jnp.right_shift(e_lane, 3).astype(F32)
    in_grp = (e_lane >= 0) & (e_lane < N_EXPERTS) & (lane_grp == g_sel)
    el = jnp.where(in_grp, logits, -jnp.inf)
    m1 = jnp.max(el, axis=-1, keepdims=True)
    i1 = first(el == m1)
    el2 = jnp.where(lane == i1, -jnp.inf, el)
    m2 = jnp.max(el2, axis=-1, keepdims=True)
    i2 = first(el2 == m2)
    e2 = jnp.exp(m2 - m1)
    g1 = g_gate / (1.0 + e2)
    g2 = g_gate * e2 / (1.0 + e2)
    eid_ref[...] = jnp.where(lane_i == 0, i1 - N_GROUPS, jnp.where(lane_i == 1, i2 - N_GROUPS, 0.0)).astype(I32)
    gate_ref[...] = jnp.where(lane_i == 0, g1, jnp.where(lane_i == 1, g2, 0.0))


def _out_route(y_rw, y_fox, x2, wo_rw, wo_fox, norm_w, wr_hi, wr_lo, b_r):
    M, D = x2.shape
    W = y_rw.shape[1]
    tm = min(256, M)
    full = lambda a: pl.BlockSpec(a.shape, lambda i: (0, 0))
    tile = lambda w: pl.BlockSpec((tm, w), lambda i: (i, 0))
    return pl.pallas_call(
        _out_route_kernel,
        grid=(M // tm,),
        in_specs=[tile(W), tile(W), tile(D), full(wo_rw), full(wo_fox), full(norm_w),
                  full(wr_hi), full(wr_lo), full(b_r)],
        out_specs=[tile(D), tile(D), tile(ROUTER_LANES), tile(ROUTER_LANES)],
        out_shape=[jax.ShapeDtypeStruct((M, D), F32), jax.ShapeDtypeStruct((M, D), F32),
                   jax.ShapeDtypeStruct((M, ROUTER_LANES), I32), jax.ShapeDtypeStruct((M, ROUTER_LANES), F32)],
        compiler_params=_cparams(("parallel",)),
        name="out_route",
    )(y_rw, y_fox, x2, wo_rw, wo_fox, norm_w, wr_hi, wr_lo, b_r)


def _rank_kernel(eid_ref, rank_ref, cnt_ref, carry_ref):
    @pl.when(pl.program_id(0) == 0)
    def _():
        carry_ref[...] = jnp.zeros_like(carry_ref)

    eid = eid_ref[...].astype(F32)
    tm = eid.shape[0]
    lane = lax.broadcasted_iota(I32, eid.shape, 1)
    lane_f = lane.astype(F32)
    pick = lambda l: jnp.sum(jnp.where(lane == l, eid, 0.0), axis=-1, keepdims=True)
    e0, e1 = pick(0), pick(1)
    oh0 = (lane_f == e0).astype(F32)
    oh1 = (lane_f == e1).astype(F32)
    both = oh0 + oh1
    ri = lax.broadcasted_iota(I32, (tm, tm), 0)
    ci = lax.broadcasted_iota(I32, (tm, tm), 1)
    before = _dot(jnp.where(ri > ci, 1.0, 0.0).astype(BF16), both.astype(BF16)) + carry_ref[...]
    r0 = jnp.sum(oh0 * before, axis=-1, keepdims=True)
    r1 = jnp.sum(oh1 * (before + oh0), axis=-1, keepdims=True)
    rank_ref[...] = jnp.where(lane == 0, r0, jnp.where(lane == 1, r1, 0.0)).astype(I32)
    total = carry_ref[...] + jnp.sum(both, axis=0, keepdims=True)
    carry_ref[...] = total
    cnt_ref[...] = jnp.broadcast_to(total, cnt_ref.shape).astype(I32)


def _rank(eid):
    M = eid.shape[0]
    tm = min(512, M)
    return pl.pallas_call(
        _rank_kernel,
        grid=(M // tm,),
        in_specs=[pl.BlockSpec((tm, ROUTER_LANES), lambda i: (i, 0))],
        out_specs=[pl.BlockSpec((tm, ROUTER_LANES), lambda i: (i, 0)),
                   pl.BlockSpec((8, ROUTER_LANES), lambda i: (0, 0))],
        out_shape=[jax.ShapeDtypeStruct((M, ROUTER_LANES), I32), jax.ShapeDtypeStruct((8, ROUTER_LANES), I32)],
        scratch_shapes=[pltpu.VMEM((1, ROUTER_LANES), F32)],
        compiler_params=_cparams(("arbitrary",)),
        name="rank",
    )(eid)


def _dest_kernel(eid_ref, rank_ref, start_ref, dest_ref):
    eid = eid_ref[...].astype(F32)
    rank = rank_ref[...].astype(F32)
    lane = lax.broadcasted_iota(I32, eid.shape, 1)
    lane_f = lane.astype(F32)
    pick = lambda x, l: jnp.sum(jnp.where(lane == l, x, 0.0), axis=-1, keepdims=True)
    base = lambda e: jnp.sum(jnp.where(lane_f == e, start_ref[...], 0.0), axis=-1, keepdims=True)
    d0 = base(pick(eid, 0)) + pick(rank, 0)
    d1 = base(pick(eid, 1)) + pick(rank, 1)
    dest_ref[...] = jnp.where(lane == 0, d0, jnp.where(lane == 1, d1, 0.0)).astype(I32)


def _dest(eid, rank, pad_start_row):
    M = eid.shape[0]
    tm = min(1024, M)
    tile = pl.BlockSpec((tm, ROUTER_LANES), lambda i: (i, 0))
    return pl.pallas_call(
        _dest_kernel,
        grid=(M // tm,),
        in_specs=[tile, tile, pl.BlockSpec((1, ROUTER_LANES), lambda i: (0, 0))],
        out_specs=tile,
        out_shape=jax.ShapeDtypeStruct((M, ROUTER_LANES), I32),
        compiler_params=_cparams(("parallel",)),
        name="dest",
    )(eid, rank, pad_start_row)


def _invert_kernel(dest_ref, cnt_ref, start_ref, end_ref, tok_ref):
    def clear(r, carry):
        tok_ref[r] = 0
        return carry

    def clear_expert(e, carry):
        lax.fori_loop(start_ref[e] + cnt_ref[e], end_ref[e], clear, 0)
        return carry

    def put(a, carry):
        tok_ref[dest_ref[a]] = lax.shift_right_logical(a, 1)
        return carry

    lax.fori_loop(0, cnt_ref.shape[0], clear_expert, 0)
    lax.fori_loop(end_ref[end_ref.shape[0] - 1], tok_ref.shape[0], clear, 0)
    lax.fori_loop(0, dest_ref.shape[0], put, 0, unroll=8)


def _invert(dest_flat, counts, pad_start, pad_end, n_rows):
    smem = pl.BlockSpec(memory_space=pltpu.SMEM)
    return pl.pallas_call(
        _invert_kernel,
        in_specs=[smem] * 4,
        out_specs=smem,
        out_shape=jax.ShapeDtypeStruct((n_rows,), I32),
        name="invert",
    )(dest_flat, counts, pad_start, pad_end)


def _experts_kernel(be_ref, nused_ref, tok_ref, h_hbm, wg_ref, wu_ref, wd_ref, yb_ref,
                    xbuf, sem, wg_bf, wu_bf, wd_bf):
    i = pl.program_id(0)
    rb = yb_ref.shape[0]
    n_used = nused_ref[0]
    slot = i & 1

    def copy(blk, r, s):
        return pltpu.make_async_copy(h_hbm.at[pl.ds(tok_ref[blk * rb + r], 1)], xbuf.at[s, pl.ds(r, 1)], sem.at[s])

    def wait_block(blk, s):
        def body(r, carry):
            copy(blk, r, s).wait()
            return carry
        lax.fori_loop(0, rb, body, 0, unroll=8)

    @pl.when(i == 0)
    def _():
        def body(r, carry):
            copy(0, r, 0).start()
            return carry
        lax.fori_loop(0, rb, body, 0, unroll=8)

    prev = be_ref[jnp.maximum(i - 1, 0)]

    @pl.when((i == 0) | (be_ref[i] != prev))
    def _():
        wg_bf[...] = wg_ref[0].astype(BF16)
        wu_bf[...] = wu_ref[0].astype(BF16)
        wd_bf[...] = wd_ref[0].astype(BF16)

    @pl.when(i < n_used)
    def _():
        wait_block(i, slot)
        for r in range(rb):
            copy(i + 1, r, 1 - slot).start(priority=r % 2)
        xb = xbuf[slot].astype(BF16)
        gate = _dot(xb, wg_bf[...])
        up = _dot(xb, wu_bf[...])
        hid = gate * _sigmoid(gate) * up
        yb_ref[...] = _dot(hid.astype(BF16), wd_bf[...])

    @pl.when(i == n_used)
    def _():
        wait_block(i, slot)

    @pl.when(i >= n_used)
    def _():
        yb_ref[...] = jnp.zeros_like(yb_ref)


def _experts(block_expert, n_used, tok, h2, w_gate, w_up, w_down, rb):
    M, D = h2.shape
    E, _, F = w_gate.shape
    nb = tok.shape[0] // rb
    grid_spec = pltpu.PrefetchScalarGridSpec(
        num_scalar_prefetch=3,
        grid=(nb,),
        in_specs=[pl.BlockSpec(memory_space=pl.ANY),
                  pl.BlockSpec((1, D, F), lambda i, be, nu, tk: (be[i], 0, 0)),
                  pl.BlockSpec((1, D, F), lambda i, be, nu, tk: (be[i], 0, 0)),
                  pl.BlockSpec((1, F, D), lambda i, be, nu, tk: (be[i], 0, 0))],
        out_specs=pl.BlockSpec((rb, D), lambda i, be, nu, tk: (i, 0)),
        scratch_shapes=[pltpu.VMEM((2, rb, D), F32), pltpu.SemaphoreType.DMA((2,)),
                        pltpu.VMEM((D, F), BF16), pltpu.VMEM((D, F), BF16), pltpu.VMEM((F, D), BF16)],
    )
    return pl.pallas_call(
        _experts_kernel,
        grid_spec=grid_spec,
        out_shape=jax.ShapeDtypeStruct((nb * rb, D), F32),
        compiler_params=_cparams(("arbitrary",)),
        name="experts",
    )(block_expert, n_used, tok, h2, w_gate, w_up, w_down)


def _combine_kernel(dest_ref, gate_ref, x1_ref, yb_hbm, out_ref, buf, sem):
    i = pl.program_id(0)
    tmc = x1_ref.shape[0]
    slot = i & 1

    def copy(tile, t, k, s):
        return pltpu.make_async_copy(yb_hbm.at[pl.ds(dest_ref[2 * (tile * tmc + t) + k], 1)],
                                     buf.at[s, k, pl.ds(t, 1)], sem.at[s])

    def start_tile(tile, s):
        def body(t, carry):
            copy(tile, t, 0, s).start(priority=0)
            copy(tile, t, 1, s).start(priority=1)
            return carry
        lax.fori_loop(0, tmc, body, 0, unroll=8)

    @pl.when(i == 0)
    def _():
        start_tile(0, 0)

    @pl.when(i + 1 < pl.num_programs(0))
    def _():
        start_tile(i + 1, 1 - slot)

    def wait(t, carry):
        copy(i, t, 0, slot).wait()
        copy(i, t, 1, slot).wait()
        return carry

    lax.fori_loop(0, tmc, wait, 0, unroll=8)
    gate = gate_ref[...]
    out_ref[...] = x1_ref[...] + gate[:, 0:1] * buf[slot, 0] + gate[:, 1:2] * buf[slot, 1]


def _combine(dest_flat, gates, x1, yb):
    M, D = x1.shape
    tmc = min(256, M)
    grid_spec = pltpu.PrefetchScalarGridSpec(
        num_scalar_prefetch=1,
        grid=(M // tmc,),
        in_specs=[pl.BlockSpec((tmc, ROUTER_LANES), lambda i, d: (i, 0)),
                  pl.BlockSpec((tmc, D), lambda i, d: (i, 0)),
                  pl.BlockSpec(memory_space=pl.ANY)],
        out_specs=pl.BlockSpec((tmc, D), lambda i, d: (i, 0)),
        scratch_shapes=[pltpu.VMEM((2, 2, tmc, D), F32), pltpu.SemaphoreType.DMA((2,))],
    )
    return pl.pallas_call(
        _combine_kernel,
        grid_spec=grid_spec,
        out_shape=jax.ShapeDtypeStruct((M, D), F32),
        compiler_params=_cparams(("arbitrary",)),
        name="combine",
    )(dest_flat, gates, x1, yb)


def _mixer(x2, B, T, norm_w, w_in, mu, w0, w2, a0, a2, g2, k_k, k_a, r_k, ln_w, ln_b, b_f, qn_w, kn_w):
    rw_heads = w0.shape[0] // HEAD
    fox_heads = b_f.shape[0]
    Wr, Wf = rw_heads * HEAD, fox_heads * HEAD
    lora = w2.shape[0] + a2.shape[0] + g2.shape[0]
    rw_cols = 3 * Wr + lora
    o_w, o_k, o_v, o_a = Wr, Wr + w2.shape[0], 2 * Wr + w2.shape[0], 3 * Wr + w2.shape[0]
    perm = np.concatenate([np.arange(0, Wr), np.arange(o_k, o_k + Wr), np.arange(o_v, o_v + Wr),
                           np.arange(o_w, o_w + w2.shape[0]), np.arange(o_a, rw_cols)])
    w_rw = w_in[:, :rw_cols][:, perm].astype(BF16)
    w_qkv = w_in[:, rw_cols:rw_cols + 3 * Wf].astype(BF16)
    w_f = jnp.pad(w_in[:, rw_cols + 3 * Wf:], ((0, 0), (0, 128 - fox_heads))).astype(BF16)
    p_rw, p_qkv, p_f = _in_proj(x2, norm_w[None, :], w_rw, w_qkv, w_f)

    row = lambda a: a.reshape(1, -1)
    ops = _rw_prep(p_rw.reshape(B, T, rw_cols), row(mu[perm]), row(w0), w2.astype(BF16), row(a0),
                   a2.astype(BF16), g2.astype(BF16), row(k_k), row(k_a), row(r_k), rw_heads)
    y_rw = _rw_scan(*ops, row(ln_w), row(ln_b))

    tile_w = lambda w: row(jnp.tile(w, fox_heads))
    q_aug, k_aug, vt = _fox_prep(p_qkv.reshape(B, T, 3 * Wf), p_f.reshape(B, T, 128),
                                 row(jnp.pad(b_f, (0, 128 - fox_heads))), tile_w(qn_w), tile_w(kn_w), fox_heads)
    y_fox = _fox_attn(q_aug, k_aug, vt)
    return y_rw.reshape(B * T, Wr), y_fox.reshape(B * T, Wf)


def _moe(y_rw, y_fox, x2, w_out, norm_w, rg_w, rg_b, re_w, re_b, w_gate, w_up, w_down):
    M, D = x2.shape
    Wr = y_rw.shape[1]
    pad = ROUTER_LANES - N_GROUPS - N_EXPERTS
    w_r = jnp.pad(jnp.concatenate([rg_w, re_w], axis=1), ((0, 0), (0, pad)))
    b_r = jnp.pad(jnp.concatenate([rg_b, re_b]), (0, pad))[None, :]
    wr_hi = w_r.astype(BF16)
    wr_lo = (w_r - wr_hi.astype(F32)).astype(BF16)
    x1, h2, eid, gates = _out_route(y_rw, y_fox, x2, w_out[:Wr].astype(BF16), w_out[Wr:].astype(BF16),
                                    norm_w[None, :], wr_hi, wr_lo, b_r)
    rank, counts = _rank(eid)

    rb = ROW_BLOCK
    counts = counts[0, :N_EXPERTS]
    padded = (counts + rb - 1) // rb * rb
    pad_end = jnp.cumsum(padded)
    pad_start = pad_end - padded
    n_blocks = (2 * M + N_EXPERTS * (rb - 1) + rb - 1) // rb + 1
    block_start = jnp.arange(n_blocks, dtype=I32) * rb
    block_expert = jnp.minimum(jnp.sum(pad_end[None, :] <= block_start[:, None], axis=1), N_EXPERTS - 1).astype(I32)
    n_used = (pad_end[-1:] // rb).astype(I32)
    start_row = jnp.pad(pad_start.astype(F32), (0, ROUTER_LANES - N_EXPERTS))[None, :]
    dest = _dest(eid, rank, start_row)[:, :2].reshape(-1)

    tok = _invert(dest, counts, pad_start.astype(I32), pad_end.astype(I32), n_blocks * rb)
    yb = _experts(block_expert, n_used, tok, h2, w_gate, w_up, w_down, rb)
    return _combine(dest, gates, x1, yb)


def kernel(x, norm_mix_w, w_in, mu_shift, rw_w0, rw_w2, rw_a0, rw_a2, rw_g2, rw_k_k, rw_k_a, rw_r_k, rw_ln_w, rw_ln_b, fox_b_f, fox_q_norm_w, fox_k_norm_w, w_out, norm_ffn_w, router_group_w, router_group_b, router_expert_w, router_expert_b, exp_w_gate, exp_w_up, exp_w_down):
    B, T, D = x.shape
    x2 = x.reshape(B * T, D)
    for l in range(w_in.shape[0]):
        y_rw, y_fox = _mixer(x2, B, T, norm_mix_w[l], w_in[l], mu_shift[l], rw_w0[l], rw_w2[l], rw_a0[l],
                             rw_a2[l], rw_g2[l], rw_k_k[l], rw_k_a[l], rw_r_k[l].reshape(-1), rw_ln_w[l],
                             rw_ln_b[l], fox_b_f[l], fox_q_norm_w[l], fox_k_norm_w[l])
        x2 = _moe(y_rw, y_fox, x2, w_out[l], norm_ffn_w[l], router_group_w[l], router_group_b[l],
                  router_expert_w[l], router_expert_b[l], exp_w_gate[l], exp_w_up[l], exp_w_down[l])
    return x2.reshape(B, T, D)
```

```python
import functools

import jax
import jax.numpy as jnp
import numpy as np
from jax import lax
from jax.experimental import pallas as pl
from jax.experimental.pallas import tpu as pltpu

F32, BF16, I32 = jnp.float32, jnp.bfloat16, jnp.int32

HEAD = 64
CHUNK = 64
GROUP = 2
GW = GROUP * HEAD
SCAN_BATCH = 2
KV_UNROLL = 2
N_GROUPS = 8
EXPERTS_PER_GROUP = 8
N_EXPERTS = N_GROUPS * EXPERTS_PER_GROUP
ROUTER_LANES = 128
ROW_BLOCK = 256
NORM_EPS = 1e-6
GN_EPS = 64e-5
NEG_BIG = -1e30
LOG2E = 1.4426950408889634
VMEM_LIMIT = 56 * 1024 * 1024


def _cparams(sem):
    return pltpu.CompilerParams(dimension_semantics=sem, vmem_limit_bytes=VMEM_LIMIT)


def _dot(a, b):
    return jnp.dot(a, b, preferred_element_type=F32)


def _dot_nt(a, b):
    return lax.dot_general(a, b, (((1,), (1,)), ((), ())), preferred_element_type=F32)


def _dot_tn(a, b):
    return lax.dot_general(a, b, (((0,), (0,)), ((), ())), preferred_element_type=F32)


def _split3(x):
    hi = x.astype(BF16)
    r1 = x - hi.astype(F32)
    mid = r1.astype(BF16)
    lo = (r1 - mid.astype(F32)).astype(BF16)
    return hi, mid, lo


def _sigmoid(z):
    return 1.0 / (1.0 + jnp.exp(-z))


def _head_sum(x, n_heads):
    lane_head = lax.broadcasted_iota(I32, x.shape, 1) // HEAD
    out = jnp.zeros_like(x)
    for h in range(n_heads):
        m = lane_head == h
        s = jnp.sum(jnp.where(m, x, 0.0), axis=-1, keepdims=True)
        out = jnp.where(m, s, out)
    return out


def _store_token_major(ref, x):
    rows, d = x.shape
    for s in range(d // 128):
        ref[pl.ds(s, rows, stride=d // 128), :] = x[:, s * 128:(s + 1) * 128]


def _load_token_major(ref, rows, d, s):
    return ref[pl.ds(s, rows, stride=d // 128), :]


def _in_proj_kernel(x_ref, nw_ref, wrw_ref, wqkv_ref, wf_ref, prw_ref, pqkv_ref, pf_ref):
    x = x_ref[...]
    h = x * lax.rsqrt(jnp.mean(x * x, axis=-1, keepdims=True) + NORM_EPS) * nw_ref[...]
    hb = h.astype(BF16)
    prw_ref[...] = _dot(hb, wrw_ref[...])
    pqkv_ref[...] = _dot(hb, wqkv_ref[...])
    pf_ref[...] = _dot(hb, wf_ref[...])


def _in_proj(x2, norm_w, w_rw, w_qkv, w_f):
    M, D = x2.shape
    tm = min(512, M)
    n_rw, n_qkv, n_f = w_rw.shape[1], w_qkv.shape[1], w_f.shape[1]
    full = lambda shape: pl.BlockSpec(shape, lambda i: (0, 0))
    return pl.pallas_call(
        _in_proj_kernel,
        grid=(M // tm,),
        in_specs=[pl.BlockSpec((tm, D), lambda i: (i, 0)), full((1, D)),
                  full((D, n_rw)), full((D, n_qkv)), full((D, n_f))],
        out_specs=[pl.BlockSpec((tm, n_rw), lambda i: (i, 0)),
                   pl.BlockSpec((tm, n_qkv), lambda i: (i, 0)),
                   pl.BlockSpec((tm, n_f), lambda i: (i, 0))],
        out_shape=[jax.ShapeDtypeStruct((M, n_rw), F32),
                   jax.ShapeDtypeStruct((M, n_qkv), F32),
                   jax.ShapeDtypeStruct((M, n_f), F32)],
        compiler_params=_cparams(("parallel",)),
        name="in_proj",
    )(x2, norm_w, w_rw, w_qkv, w_f)


def _rw_prep_kernel(n_heads, p_ref, pprev_ref, mu_ref, w0_ref, w2_ref, a0_ref, a2_ref, g2_ref,
                    kk_ref, ka_ref, rk_ref,
                    rt_ref, at_ref, bt_ref, kt_ref, bb_ref, kb_ref, v_ref, dl_ref, bonus_ref, g_ref):
    W = n_heads * HEAD
    j = pl.program_id(1)
    p = p_ref[0]
    tt = p.shape[0]
    last_prev = jnp.where(j > 0, pprev_ref[0, 7:8, :], 0.0)
    row = lax.broadcasted_iota(I32, p.shape, 0)
    prev = jnp.where(row == 0, last_prev, pltpu.roll(p, 1, axis=0))
    ps = p + (prev - p) * mu_ref[...]
    r, k, v = ps[:, 0:W], ps[:, W:2 * W], ps[:, 2 * W:3 * W]
    o = 3 * W
    pw, pa, pg = ps[:, o:o + 64], ps[:, o + 64:o + 128], ps[:, o + 128:o + 256]

    z = w0_ref[...] + _dot(jnp.tanh(pw).astype(BF16), w2_ref[...])
    lw = (-np.exp(-0.5)).astype(np.float32) * _sigmoid(z)
    a_sig = _sigmoid(a0_ref[...] + _dot(pa.astype(BF16), a2_ref[...]))
    g_ref[0] = _dot(_sigmoid(pg).astype(BF16), g2_ref[...])

    kk = k * kk_ref[...]
    kk = kk / jnp.maximum(jnp.sqrt(_head_sum(kk * kk, n_heads)), 1e-12)
    km = k * (1.0 + (a_sig - 1.0) * ka_ref[...])
    a_vec = -kk
    b_vec = kk * a_sig
    bonus_ref[0] = _head_sum(r * km * rk_ref[...], n_heads) * v
    v_ref[0] = v.astype(BF16)

    ti = lax.broadcasted_iota(I32, (tt, tt), 0)
    si = lax.broadcasted_iota(I32, (tt, tt), 1)
    same = (ti // CHUNK) == (si // CHUNK)
    tri = jnp.where(same & (ti >= si), 1.0, 0.0).astype(BF16)
    ones = jnp.where(same, 1.0, 0.0).astype(BF16)
    nck = tt // CHUNK
    ci = lax.broadcasted_iota(I32, (nck, tt), 0)
    cs = lax.broadcasted_iota(I32, (nck, tt), 1)
    sel = jnp.where(ci == cs // CHUNK, 1.0, 0.0).astype(BF16)
    hi, mid, lo = _split3(lw)
    cum = _dot(tri, hi) + _dot(tri, mid) + _dot(tri, lo)
    tot = _dot(ones, hi) + _dot(ones, mid) + _dot(ones, lo)
    dl = jnp.exp(_dot(sel, hi) + _dot(sel, mid) + _dot(sel, lo))
    for ck in range(nck):
        dl_ref[0, ck] = dl[ck:ck + 1, :]

    e_in = jnp.exp(cum)
    e_ex = jnp.exp(cum - lw)
    e_inv = jnp.exp(-cum)
    e_bar = jnp.exp(tot - cum)
    rt_ref[0] = (r * e_in).astype(BF16)
    at_ref[0] = (a_vec * e_ex).astype(BF16)
    bt_ref[0] = (b_vec * e_inv).astype(BF16)
    kt_ref[0] = (km * e_inv).astype(BF16)
    bb_ref[0] = (b_vec * e_bar).astype(BF16)
    kb_ref[0] = (km * e_bar).astype(BF16)


def _rw_prep(p_rw, mu, w0, w2, a0, a2, g2, k_k, k_a, r_k, n_heads):
    B, T, P = p_rw.shape
    W = n_heads * HEAD
    tt = min(512, T)
    row = lambda a: pl.BlockSpec(a.shape, lambda b, j: (0, 0))
    tile = lambda w: pl.BlockSpec((1, tt, w), lambda b, j: (b, j, 0))
    bf = jax.ShapeDtypeStruct((B, T, W), BF16)
    f32 = jax.ShapeDtypeStruct((B, T, W), F32)
    return pl.pallas_call(
        functools.partial(_rw_prep_kernel, n_heads),
        grid=(B, T // tt),
        in_specs=[tile(P),
                  pl.BlockSpec((1, 8, P), lambda b, j: (b, jnp.maximum(j * (tt // 8) - 1, 0), 0)),
                  row(mu), row(w0), row(w2), row(a0), row(a2), row(g2), row(k_k), row(k_a), row(r_k)],
        out_specs=[tile(W)] * 7 + [pl.BlockSpec((1, tt // CHUNK, 1, W), lambda b, j: (b, j, 0, 0)), tile(W), tile(W)],
        out_shape=[bf] * 7 + [jax.ShapeDtypeStruct((B, T // CHUNK, 1, W), F32), f32, f32],
        compiler_params=_cparams(("parallel", "parallel")),
        name="rw_prep",
    )(p_rw, p_rw, mu, w0, w2, a0, a2, g2, k_k, k_a, r_k)


def _rw_scan_kernel(rt_ref, at_ref, bt_ref, kt_ref, bb_ref, kb_ref, v_ref, dl_ref,
                    bonus_ref, g_ref, lnw_ref, lnb_ref, y_ref, s_ref):
    C = CHUNK
    nb, tt, W = rt_ref.shape
    chains = [(b, g) for b in range(nb) for g in range(W // GW)]

    @pl.when(pl.program_id(1) == 0)
    def _():
        s_ref[...] = jnp.zeros_like(s_ref)

    ri = lax.broadcasted_iota(I32, (GW, GW), 0)
    ci = lax.broadcasted_iota(I32, (GW, GW), 1)
    tr, tc = ri & (C - 1), ci & (C - 1)
    strict, incl, eye = tr > tc, tr >= tc, ri == ci
    lane_head = lax.broadcasted_iota(I32, (C, GW), 1) // HEAD

    def bd(x):
        return jnp.concatenate([jnp.where(lane_head == h, x, jnp.zeros_like(x)) for h in range(GROUP)], axis=0)

    bf = lambda xs: [x.astype(BF16) for x in xs]

    def chunk(c, carry):
        rows = pl.ds(pl.multiple_of(c * C, C), C)
        ld = lambda ref: [bd(ref[b, rows, g * GW:(g + 1) * GW]) for b, g in chains]
        Rt, At, Bt, Kt, Bb, Kb, V = (ld(r) for r in (rt_ref, at_ref, bt_ref, kt_ref, bb_ref, kb_ref, v_ref))
        Lab = [jnp.where(strict, _dot_nt(a, b), 0.0) for a, b in zip(At, Bt)]
        Lak = bf([jnp.where(strict, _dot_nt(a, k), 0.0) for a, k in zip(At, Kt)])
        Mrb = bf([jnp.where(incl, _dot_nt(r, b), 0.0) for r, b in zip(Rt, Bt)])
        Mrk = bf([jnp.where(incl, _dot_nt(r, k), 0.0) for r, k in zip(Rt, Kt)])
        P = [jnp.where(eye, 1.0, l) for l in Lab]
        Lp = Lab
        for _ in range(5):
            Lpb = bf(Lp)
            Lp = [_dot(x, x) for x in Lpb]
            P = [p + _dot(pb, lb) for p, pb, lb in zip(P, bf(P), bf(Lp))]
        Pb = bf(P)
        Wk = bf([_dot(l, v) for l, v in zip(Lak, V)])
        Ah = [_dot(p, a) for p, a in zip(Pb, At)]
        U0 = [_dot(p, w) for p, w in zip(Pb, Wk)]
        Ahb, U0b = bf(Ah), bf(U0)
        Rh = bf([r.astype(F32) + _dot(m, a) for r, m, a in zip(Rt, Mrb, Ahb)])
        Y0 = [_dot(mb, u) + _dot(mk, v) for mb, u, mk, v in zip(Mrb, U0b, Mrk, V)]
        GT = bf([_dot_tn(a, b) for a, b in zip(Ahb, Bb)])
        HT = [_dot_tn(u, b) + _dot_tn(v, k) for u, b, v, k in zip(U0b, Bb, V, Kb)]
        S0 = [s_ref[i] for i in range(len(chains))]
        S0b = bf(S0)
        Y = [_dot_nt(r, s) + y0 for r, s, y0 in zip(Rh, S0b, Y0)]
        for i, (b, g) in enumerate(chains):
            ls = slice(g * GW, (g + 1) * GW)
            s_ref[i] = S0[i] * dl_ref[b, c, :, ls] + _dot(S0b[i], GT[i]) + HT[i]
            y = sum(Y[i][h * C:(h + 1) * C] for h in range(GROUP))
            mean = _head_sum(y, GROUP) * (1.0 / HEAD)
            yc = y - mean
            var = _head_sum(yc * yc, GROUP) * (1.0 / HEAD)
            yn = yc * lax.rsqrt(var + GN_EPS) * lnw_ref[:, ls] + lnb_ref[:, ls]
            y_ref[b, rows, ls] = (yn + bonus_ref[b, rows, ls]) * g_ref[b, rows, ls]
        return carry

    lax.fori_loop(0, tt // C, chunk, 0)


def _rw_scan(rt, at, bt, kt, bb, kb, v, dl, bonus, g, ln_w, ln_b):
    B, T, W = rt.shape
    nb = min(SCAN_BATCH, B)
    tt = min(256, T)
    tile = pl.BlockSpec((nb, tt, W), lambda b, j: (b, j, 0))
    row = pl.BlockSpec((1, W), lambda b, j: (0, 0))
    return pl.pallas_call(
        _rw_scan_kernel,
        grid=(B // nb, T // tt),
        in_specs=[tile] * 7 + [pl.BlockSpec((nb, tt // CHUNK, 1, W), lambda b, j: (b, j, 0, 0)), tile, tile, row, row],
        out_specs=tile,
        out_shape=jax.ShapeDtypeStruct((B, T, W), F32),
        scratch_shapes=[pltpu.VMEM((nb * (W // GW), GW, GW), F32)],
        compiler_params=_cparams(("parallel", "arbitrary")),
        name="rw_scan",
    )(rt, at, bt, kt, bb, kb, v, dl, bonus, g, ln_w, ln_b)


def _fox_prep_kernel(n_heads, qkv_ref, f_ref, bf_ref, qnw_ref, knw_ref,
                     q_ref, k_ref, vt_ref, carry_ref):
    W = n_heads * HEAD
    tt = qkv_ref.shape[1]

    @pl.when(pl.program_id(1) == 0)
    def _():
        carry_ref[...] = jnp.zeros_like(carry_ref)

    qkv = qkv_ref[0]
    q, k, v = qkv[:, 0:W], qkv[:, W:2 * W], qkv[:, 2 * W:3 * W]
    inv_n = 1.0 / HEAD
    qn = q * lax.rsqrt(_head_sum(q * q, n_heads) * inv_n + NORM_EPS) * qnw_ref[...]
    kn = k * lax.rsqrt(_head_sum(k * k, n_heads) * inv_n + NORM_EPS) * knw_ref[...]
    qb = (qn * (HEAD ** -0.5 * LOG2E)).astype(BF16)
    kb = kn.astype(BF16)
    for p in range(n_heads // 2):
        vt_ref[0, p, 0] = v[:, p * 128:(p + 1) * 128].T.astype(BF16)

    zf = f_ref[0] + bf_ref[...]
    logf = jnp.minimum(zf, 0.0) - jnp.log(1.0 + jnp.exp(-jnp.abs(zf)))
    ti = lax.broadcasted_iota(I32, (tt, tt), 0)
    si = lax.broadcasted_iota(I32, (tt, tt), 1)
    tri = jnp.where(ti >= si, 1.0, 0.0).astype(BF16)
    hi, mid, lo = _split3(logf)
    c = carry_ref[...] + _dot(tri, hi) + _dot(tri, mid) + _dot(tri, lo)
    carry_ref[...] = c[tt - 1:tt, :]
    c_hi, c_mid, c_lo = _split3(c * (-LOG2E))

    src_w = lax.broadcasted_iota(I32, (W, 128), 0)
    dst_w = lax.broadcasted_iota(I32, (W, 128), 1)
    src_c = lax.broadcasted_iota(I32, (128, 128), 0)
    dst_c = lax.broadcasted_iota(I32, (128, 128), 1)
    lane = lax.broadcasted_iota(I32, (1, 128), 1)
    ones = jnp.where((lane >= HEAD) & (lane < HEAD + 3), 1.0, 0.0)
    for h in range(n_heads):
        sel_w = jnp.where((src_w == dst_w + h * HEAD) & (dst_w < HEAD), 1.0, 0.0).astype(BF16)
        sel_c = [jnp.where((src_c == h) & (dst_c == HEAD + j), 1.0, 0.0).astype(BF16) for j in range(3)]
        q_ref[0, h] = (_dot(qb, sel_w) + ones).astype(BF16)
        k_ref[0, h] = (_dot(kb, sel_w) + _dot(c_hi, sel_c[0]) + _dot(c_mid, sel_c[1])
                       + _dot(c_lo, sel_c[2])).astype(BF16)


def _fox_prep(qkv, f, b_f, qn_w, kn_w, n_heads):
    B, T, _ = qkv.shape
    W = n_heads * HEAD
    tt = min(256, T)
    row = lambda a: pl.BlockSpec(a.shape, lambda b, j: (0, 0))
    aug = jax.ShapeDtypeStruct((B, n_heads, T, 128), BF16)
    aug_spec = pl.BlockSpec((1, n_heads, tt, 128), lambda b, j: (b, 0, j, 0))
    return pl.pallas_call(
        functools.partial(_fox_prep_kernel, n_heads),
        grid=(B, T // tt),
        in_specs=[pl.BlockSpec((1, tt, 3 * W), lambda b, j: (b, j, 0)),
                  pl.BlockSpec((1, tt, 128), lambda b, j: (b, j, 0)),
                  row(b_f), row(qn_w), row(kn_w)],
        out_specs=[aug_spec, aug_spec,
                   pl.BlockSpec((1, n_heads // 2, 1, 128, tt), lambda b, j: (b, 0, j, 0, 0))],
        out_shape=[aug, aug, jax.ShapeDtypeStruct((B, n_heads // 2, T // tt, 128, tt), BF16)],
        scratch_shapes=[pltpu.VMEM((1, 128), F32)],
        compiler_params=_cparams(("parallel", "arbitrary")),
        name="fox_prep",
    )(qkv, f, b_f, qn_w, kn_w)


def _fox_attn_kernel(q_ref, k_ref, vt_ref, o_ref):
    i = pl.program_id(2)
    tq = q_ref.shape[2]
    sub = vt_ref.shape[4]
    q = [q_ref[0, hh] for hh in range(2)]
    ki = lax.broadcasted_iota(I32, (tq, tq), 0)
    qi = lax.broadcasted_iota(I32, (tq, tq), 1)

    def scores(j, diag=False):
        k0 = pl.multiple_of(j * tq, tq)
        s = [_dot_nt(k_ref[0, hh, pl.ds(k0, tq), :], q[hh]) for hh in range(2)]
        return [jnp.where(ki <= qi, x, -jnp.inf) for x in s] if diag else s

    def update(j, s, carry):
        out = []
        for hh in range(2):
            m, l, acc = carry[hh]
            m_new = jnp.maximum(m, jnp.max(s[hh], axis=0, keepdims=True))
            alpha = jnp.exp2(m - m_new)
            p = jnp.exp2(s[hh] - m_new)
            l = alpha * l + jnp.sum(p, axis=0, keepdims=True)
            p = p.astype(BF16)
            pv = sum(_dot(vt_ref[0, 0, j * (tq // sub) + u], p[u * sub:(u + 1) * sub]) for u in range(tq // sub))
            out.append((m_new, l, alpha * acc + pv))
        return tuple(out)

    def several_blocks(jj, carry):
        ss = [scores(KV_UNROLL * jj + u) for u in range(KV_UNROLL)]
        for u in range(KV_UNROLL):
            carry = update(KV_UNROLL * jj + u, ss[u], carry)
        return carry

    init = tuple((jnp.full((1, tq), NEG_BIG, F32), jnp.zeros((1, tq), F32), jnp.zeros((128, tq), F32))
                 for _ in range(2))
    carry = lax.fori_loop(0, i // KV_UNROLL, several_blocks, init)
    carry = lax.fori_loop(i - i % KV_UNROLL, i, lambda j, c: update(j, scores(j), c), carry)
    (_, l0, acc0), (_, l1, acc1) = update(i, scores(i, diag=True), carry)
    row = lax.broadcasted_iota(I32, (128, tq), 0)
    o_ref[0] = jnp.where(row < HEAD, acc0 / l0, acc1 / l1).T


def _fox_attn(q_aug, k_aug, vt):
    B, H, T, _ = q_aug.shape
    tq = min(512, T)
    n_sub, sub = vt.shape[2], vt.shape[4]
    return pl.pallas_call(
        _fox_attn_kernel,
        grid=(B, H // 2, T // tq),
        in_specs=[pl.BlockSpec((1, 2, tq, 128), lambda b, p, i: (b, p, i, 0)),
                  pl.BlockSpec((1, 2, T, 128), lambda b, p, i: (b, p, 0, 0)),
                  pl.BlockSpec((1, 1, n_sub, 128, sub), lambda b, p, i: (b, p, 0, 0, 0))],
        out_specs=pl.BlockSpec((1, tq, 128), lambda b, p, i: (b, i, p)),
        out_shape=jax.ShapeDtypeStruct((B, T, H * HEAD), F32),
        compiler_params=_cparams(("parallel", "parallel", "arbitrary")),
        name="fox_attn",
    )(q_aug, k_aug, vt)


def _out_route_kernel(yrw_ref, yfox_ref, x_ref, wo_rw_ref, wo_fox_ref, nw_ref, wr_hi_ref, wr_lo_ref, br_ref,
                      x1_ref, h2_ref, eid_ref, gate_ref):
    x1 = (x_ref[...] + _dot(yrw_ref[...].astype(BF16), wo_rw_ref[...])
          + _dot(yfox_ref[...].astype(BF16), wo_fox_ref[...]))
    x1_ref[...] = x1
    h2 = x1 * lax.rsqrt(jnp.mean(x1 * x1, axis=-1, keepdims=True) + NORM_EPS) * nw_ref[...]
    _store_token_major(h2_ref, h2)
    h_hi = h2.astype(BF16)
    h_lo = (h2 - h_hi.astype(F32)).astype(BF16)
    logits = (_dot(h_hi, wr_hi_ref[...]) + _dot(h_hi, wr_lo_ref[...]) + _dot(h_lo, wr_hi_ref[...])) + br_ref[...]

    lane_i = lax.broadcasted_iota(I32, logits.shape, 1)
    lane = lane_i.astype(F32)
    first = lambda mask: jnp.min(jnp.where(mask, lane, 1e9), axis=-1, keepdims=True)
    gl = jnp.where(lane_i < N_GROUPS, logits, -jnp.inf)
    gmax = jnp.max(gl, axis=-1, keepdims=True)
    g_sel = first(gl == gmax)
    g_gate = 1.0 / jnp.sum(jnp.exp(gl - gmax), axis=-1, keepdims=True)
    e_lane = lane_i - N_GROUPS
    lane_grp = jnp.right_shift(e_lane, 3).astype(F32)
    in_grp = (e_lane >= 0) & (e_lane < N_EXPERTS) & (lane_grp == g_sel)
    el = jnp.where(in_grp, logits, -jnp.inf)
    m1 = jnp.max(el, axis=-1, keepdims=True)
    i1 = first(el == m1)
    el2 = jnp.where(lane == i1, -jnp.inf, el)
    m2 = jnp.max(el2, axis=-1, keepdims=True)
    i2 = first(el2 == m2)
    e2 = jnp.exp(m2 - m1)
    g1 = g_gate / (1.0 + e2)
    g2 = g_gate * e2 / (1.0 + e2)
    eid_ref[...] = jnp.where(lane_i == 0, i1 - N_GROUPS, jnp.where(lane_i == 1, i2 - N_GROUPS, 0.0)).astype(I32)
    gate_ref[...] = jnp.where(lane_i == 0, g1, jnp.where(lane_i == 1, g2, 0.0))


def _out_route(y_rw, y_fox, x2, wo_rw, wo_fox, norm_w, wr_hi, wr_lo, b_r):
    M, D = x2.shape
    W = y_rw.shape[1]
    tm = min(256, M)
    full = lambda a: pl.BlockSpec(a.shape, lambda i: (0, 0))
    tile = lambda w: pl.BlockSpec((tm, w), lambda i: (i, 0))
    return pl.pallas_call(
        _out_route_kernel,
        grid=(M // tm,),
        in_specs=[tile(W), tile(W), tile(D), full(wo_rw), full(wo_fox), full(norm_w),
                  full(wr_hi), full(wr_lo), full(b_r)],
        out_specs=[tile(D), pl.BlockSpec((tm * (D // 128), 128), lambda i: (i, 0)),
                   tile(ROUTER_LANES), tile(ROUTER_LANES)],
        out_shape=[jax.ShapeDtypeStruct((M, D), F32), jax.ShapeDtypeStruct((M * (D // 128), 128), F32),
                   jax.ShapeDtypeStruct((M, ROUTER_LANES), I32), jax.ShapeDtypeStruct((M, ROUTER_LANES), F32)],
        compiler_params=_cparams(("parallel",)),
        name="out_route",
    )(y_rw, y_fox, x2, wo_rw, wo_fox, norm_w, wr_hi, wr_lo, b_r)


def _rank_kernel(eid_ref, rank_ref, cnt_ref, carry_ref):
    @pl.when(pl.program_id(0) == 0)
    def _():
        carry_ref[...] = jnp.zeros_like(carry_ref)

    eid = eid_ref[...].astype(F32)
    tm = eid.shape[0]
    lane = lax.broadcasted_iota(I32, eid.shape, 1)
    lane_f = lane.astype(F32)
    pick = lambda l: jnp.sum(jnp.where(lane == l, eid, 0.0), axis=-1, keepdims=True)
    e0, e1 = pick(0), pick(1)
    oh0 = (lane_f == e0).astype(F32)
    oh1 = (lane_f == e1).astype(F32)
    both = oh0 + oh1
    ri = lax.broadcasted_iota(I32, (tm, tm), 0)
    ci = lax.broadcasted_iota(I32, (tm, tm), 1)
    before = _dot(jnp.where(ri > ci, 1.0, 0.0).astype(BF16), both.astype(BF16)) + carry_ref[...]
    r0 = jnp.sum(oh0 * before, axis=-1, keepdims=True)
    r1 = jnp.sum(oh1 * (before + oh0), axis=-1, keepdims=True)
    rank_ref[...] = jnp.where(lane == 0, r0, jnp.where(lane == 1, r1, 0.0)).astype(I32)
    total = carry_ref[...] + jnp.sum(both, axis=0, keepdims=True)
    carry_ref[...] = total
    cnt_ref[...] = jnp.broadcast_to(total, cnt_ref.shape).astype(I32)


def _rank(eid):
    M = eid.shape[0]
    tm = min(512, M)
    return pl.pallas_call(
        _rank_kernel,
        grid=(M // tm,),
        in_specs=[pl.BlockSpec((tm, ROUTER_LANES), lambda i: (i, 0))],
        out_specs=[pl.BlockSpec((tm, ROUTER_LANES), lambda i: (i, 0)),
                   pl.BlockSpec((8, ROUTER_LANES), lambda i: (0, 0))],
        out_shape=[jax.ShapeDtypeStruct((M, ROUTER_LANES), I32), jax.ShapeDtypeStruct((8, ROUTER_LANES), I32)],
        scratch_shapes=[pltpu.VMEM((1, ROUTER_LANES), F32)],
        compiler_params=_cparams(("arbitrary",)),
        name="rank",
    )(eid)


def _dest_kernel(eid_ref, rank_ref, start_ref, dest_ref):
    eid = eid_ref[...].astype(F32)
    rank = rank_ref[...].astype(F32)
    lane = lax.broadcasted_iota(I32, eid.shape, 1)
    lane_f = lane.astype(F32)
    pick = lambda x, l: jnp.sum(jnp.where(lane == l, x, 0.0), axis=-1, keepdims=True)
    base = lambda e: jnp.sum(jnp.where(lane_f == e, start_ref[...], 0.0), axis=-1, keepdims=True)
    d0 = base(pick(eid, 0)) + pick(rank, 0)
    d1 = base(pick(eid, 1)) + pick(rank, 1)
    dest_ref[...] = jnp.where(lane == 0, d0, jnp.where(lane == 1, d1, 0.0)).astype(I32)


def _dest(eid, rank, pad_start_row):
    M = eid.shape[0]
    tm = min(1024, M)
    tile = pl.BlockSpec((tm, ROUTER_LANES), lambda i: (i, 0))
    return pl.pallas_call(
        _dest_kernel,
        grid=(M // tm,),
        in_specs=[tile, tile, pl.BlockSpec((1, ROUTER_LANES), lambda i: (0, 0))],
        out_specs=tile,
        out_shape=jax.ShapeDtypeStruct((M, ROUTER_LANES), I32),
        compiler_params=_cparams(("parallel",)),
        name="dest",
    )(eid, rank, pad_start_row)


def _invert_kernel(dest_ref, cnt_ref, start_ref, end_ref, tok_ref):
    def clear(r, carry):
        tok_ref[r] = 0
        return carry

    def clear_expert(e, carry):
        lax.fori_loop(start_ref[e] + cnt_ref[e], end_ref[e], clear, 0)
        return carry

    def put(a, carry):
        tok_ref[dest_ref[a]] = lax.shift_right_logical(a, 1)
        return carry

    lax.fori_loop(0, cnt_ref.shape[0], clear_expert, 0)
    lax.fori_loop(end_ref[end_ref.shape[0] - 1], tok_ref.shape[0], clear, 0)
    lax.fori_loop(0, dest_ref.shape[0], put, 0, unroll=8)


def _invert(dest_flat, counts, pad_start, pad_end, n_rows):
    smem = pl.BlockSpec(memory_space=pltpu.SMEM)
    return pl.pallas_call(
        _invert_kernel,
        in_specs=[smem] * 4,
        out_specs=smem,
        out_shape=jax.ShapeDtypeStruct((n_rows,), I32),
        name="invert",
    )(dest_flat, counts, pad_start, pad_end)


def _experts_kernel(be_ref, nused_ref, tok_ref, h_hbm, wg_ref, wu_ref, wd_ref, yb_ref,
                    xbuf, sem, wg_bf, wu_bf, wd_bf):
    i = pl.program_id(0)
    D, F = wg_bf.shape
    nt = D // 128
    rb = yb_ref.shape[0] // nt
    n_used = nused_ref[0]
    slot = i & 1

    def copy(blk, r, s):
        src = pl.multiple_of(tok_ref[blk * rb + r] * nt, nt)
        return pltpu.make_async_copy(h_hbm.at[pl.ds(src, nt)], xbuf.at[s, pl.ds(r * nt, nt)], sem.at[s])

    def wait_block(blk, s):
        def body(r, carry):
            copy(blk, r, s).wait()
            return carry
        lax.fori_loop(0, rb, body, 0, unroll=8)

    @pl.when(i == 0)
    def _():
        def body(r, carry):
            copy(0, r, 0).start()
            return carry
        lax.fori_loop(0, rb, body, 0, unroll=8)

    prev = be_ref[jnp.maximum(i - 1, 0)]

    @pl.when((i == 0) | (be_ref[i] != prev))
    def _():
        wg_bf[...] = wg_ref[0].astype(BF16)
        wu_bf[...] = wu_ref[0].astype(BF16)
        wd_bf[...] = wd_ref[0].astype(BF16)

    @pl.when(i < n_used)
    def _():
        wait_block(i, slot)

        def start_next(r, carry):
            copy(i + 1, 2 * r, 1 - slot).start(priority=0)
            copy(i + 1, 2 * r + 1, 1 - slot).start(priority=1)
            return carry
        lax.fori_loop(0, rb // 2, start_next, 0, unroll=4)

        xb = jnp.concatenate([_load_token_major(xbuf.at[slot], rb, D, s).astype(BF16) for s in range(nt)], axis=1)
        gate = _dot(xb, wg_bf[...])
        up = _dot(xb, wu_bf[...])
        hid = gate * _sigmoid(gate) * up
        _store_token_major(yb_ref, _dot(hid.astype(BF16), wd_bf[...]))

    @pl.when(i == n_used)
    def _():
        wait_block(i, slot)

    @pl.when(i >= n_used)
    def _():
        yb_ref[...] = jnp.zeros_like(yb_ref)


def _experts(block_expert, n_used, tok, h2, w_gate, w_up, w_down, rb):
    E, D, F = w_gate.shape
    nt = D // 128
    nb = tok.shape[0] // rb
    grid_spec = pltpu.PrefetchScalarGridSpec(
        num_scalar_prefetch=3,
        grid=(nb,),
        in_specs=[pl.BlockSpec(memory_space=pl.ANY),
                  pl.BlockSpec((1, D, F), lambda i, be, nu, tk: (be[i], 0, 0)),
                  pl.BlockSpec((1, D, F), lambda i, be, nu, tk: (be[i], 0, 0)),
                  pl.BlockSpec((1, F, D), lambda i, be, nu, tk: (be[i], 0, 0))],
        out_specs=pl.BlockSpec((rb * nt, 128), lambda i, be, nu, tk: (i, 0)),
        scratch_shapes=[pltpu.VMEM((2, rb * nt, 128), F32), pltpu.SemaphoreType.DMA((2,)),
                        pltpu.VMEM((D, F), BF16), pltpu.VMEM((D, F), BF16), pltpu.VMEM((F, D), BF16)],
    )
    return pl.pallas_call(
        _experts_kernel,
        grid_spec=grid_spec,
        out_shape=jax.ShapeDtypeStruct((nb * rb * nt, 128), F32),
        compiler_params=_cparams(("arbitrary",)),
        name="experts",
    )(block_expert, n_used, tok, h2, w_gate, w_up, w_down)


def _combine_kernel(dest_ref, gate_ref, x1_ref, yb_hbm, out_ref, buf, sem):
    i = pl.program_id(0)
    tmc, D = x1_ref.shape
    nt = D // 128
    slot = i & 1

    def copy(tile, t, k, s):
        src = pl.multiple_of(dest_ref[2 * (tile * tmc + t) + k] * nt, nt)
        return pltpu.make_async_copy(yb_hbm.at[pl.ds(src, nt)], buf.at[s, k, pl.ds(t * nt, nt)], sem.at[s])

    def start_tile(tile, s):
        def body(t, carry):
            copy(tile, t, 0, s).start(priority=0)
            copy(tile, t, 1, s).start(priority=1)
            return carry
        lax.fori_loop(0, tmc, body, 0, unroll=8)

    @pl.when(i == 0)
    def _():
        start_tile(0, 0)

    @pl.when(i + 1 < pl.num_programs(0))
    def _():
        start_tile(i + 1, 1 - slot)

    def wait(t, carry):
        copy(i, t, 0, slot).wait()
        copy(i, t, 1, slot).wait()
        return carry

    lax.fori_loop(0, tmc, wait, 0, unroll=8)
    gate = gate_ref[...]
    g0, g1 = gate[:, 0:1], gate[:, 1:2]
    for s in range(nt):
        cols = slice(s * 128, (s + 1) * 128)
        out_ref[:, cols] = (x1_ref[:, cols] + g0 * _load_token_major(buf.at[slot, 0], tmc, D, s)
                            + g1 * _load_token_major(buf.at[slot, 1], tmc, D, s))


def _combine(dest_flat, gates, x1, yb):
    M, D = x1.shape
    tmc = min(256, M)
    grid_spec = pltpu.PrefetchScalarGridSpec(
        num_scalar_prefetch=1,
        grid=(M // tmc,),
        in_specs=[pl.BlockSpec((tmc, ROUTER_LANES), lambda i, d: (i, 0)),
                  pl.BlockSpec((tmc, D), lambda i, d: (i, 0)),
                  pl.BlockSpec(memory_space=pl.ANY)],
        out_specs=pl.BlockSpec((tmc, D), lambda i, d: (i, 0)),
        scratch_shapes=[pltpu.VMEM((2, 2, tmc * (D // 128), 128), F32), pltpu.SemaphoreType.DMA((2,))],
    )
    return pl.pallas_call(
        _combine_kernel,
        grid_spec=grid_spec,
        out_shape=jax.ShapeDtypeStruct((M, D), F32),
        compiler_params=_cparams(("arbitrary",)),
        name="combine",
    )(dest_flat, gates, x1, yb)


def _mixer(x2, B, T, norm_w, w_in, mu, w0, w2, a0, a2, g2, k_k, k_a, r_k, ln_w, ln_b, b_f, qn_w, kn_w):
    rw_heads = w0.shape[0] // HEAD
    fox_heads = b_f.shape[0]
    Wr, Wf = rw_heads * HEAD, fox_heads * HEAD
    lora = w2.shape[0] + a2.shape[0] + g2.shape[0]
    rw_cols = 3 * Wr + lora
    o_w, o_k, o_v, o_a = Wr, Wr + w2.shape[0], 2 * Wr + w2.shape[0], 3 * Wr + w2.shape[0]
    perm = np.concatenate([np.arange(0, Wr), np.arange(o_k, o_k + Wr), np.arange(o_v, o_v + Wr),
                           np.arange(o_w, o_w + w2.shape[0]), np.arange(o_a, rw_cols)])
    w_rw = w_in[:, :rw_cols][:, perm].astype(BF16)
    w_qkv = w_in[:, rw_cols:rw_cols + 3 * Wf].astype(BF16)
    w_f = jnp.pad(w_in[:, rw_cols + 3 * Wf:], ((0, 0), (0, 128 - fox_heads))).astype(BF16)
    p_rw, p_qkv, p_f = _in_proj(x2, norm_w[None, :], w_rw, w_qkv, w_f)

    row = lambda a: a.reshape(1, -1)
    ops = _rw_prep(p_rw.reshape(B, T, rw_cols), row(mu[perm]), row(w0), w2.astype(BF16), row(a0),
                   a2.astype(BF16), g2.astype(BF16), row(k_k), row(k_a), row(r_k), rw_heads)
    y_rw = _rw_scan(*ops, row(ln_w), row(ln_b))

    tile_w = lambda w: row(jnp.tile(w, fox_heads))
    q_aug, k_aug, vt = _fox_prep(p_qkv.reshape(B, T, 3 * Wf), p_f.reshape(B, T, 128),
                                 row(jnp.pad(b_f, (0, 128 - fox_heads))), tile_w(qn_w), tile_w(kn_w), fox_heads)
    y_fox = _fox_attn(q_aug, k_aug, vt)
    return y_rw.reshape(B * T, Wr), y_fox.reshape(B * T, Wf)


def _moe(y_rw, y_fox, x2, w_out, norm_w, rg_w, rg_b, re_w, re_b, w_gate, w_up, w_down):
    M, D = x2.shape
    Wr = y_rw.shape[1]
    pad = ROUTER_LANES - N_GROUPS - N_EXPERTS
    w_r = jnp.pad(jnp.concatenate([rg_w, re_w], axis=1), ((0, 0), (0, pad)))
    b_r = jnp.pad(jnp.concatenate([rg_b, re_b]), (0, pad))[None, :]
    wr_hi = w_r.astype(BF16)
    wr_lo = (w_r - wr_hi.astype(F32)).astype(BF16)
    x1, h2, eid, gates = _out_route(y_rw, y_fox, x2, w_out[:Wr].astype(BF16), w_out[Wr:].astype(BF16),
                                    norm_w[None, :], wr_hi, wr_lo, b_r)
    rank, counts = _rank(eid)

    rb = ROW_BLOCK
    counts = counts[0, :N_EXPERTS]
    padded = (counts + rb - 1) // rb * rb
    pad_end = jnp.cumsum(padded)
    pad_start = pad_end - padded
    n_blocks = (2 * M + N_EXPERTS * (rb - 1) + rb - 1) // rb + 1
    block_start = jnp.arange(n_blocks, dtype=I32) * rb
    block_expert = jnp.minimum(jnp.sum(pad_end[None, :] <= block_start[:, None], axis=1), N_EXPERTS - 1).astype(I32)
    n_used = (pad_end[-1:] // rb).astype(I32)
    start_row = jnp.pad(pad_start.astype(F32), (0, ROUTER_LANES - N_EXPERTS))[None, :]
    dest = _dest(eid, rank, start_row)[:, :2].reshape(-1)

    tok = _invert(dest, counts, pad_start.astype(I32), pad_end.astype(I32), n_blocks * rb)
    yb = _experts(block_expert, n_used, tok, h2, w_gate, w_up, w_down, rb)
    return _combine(dest, gates, x1, yb)


def kernel(x, norm_mix_w, w_in, mu_shift, rw_w0, rw_w2, rw_a0, rw_a2, rw_g2, rw_k_k, rw_k_a, rw_r_k, rw_ln_w, rw_ln_b, fox_b_f, fox_q_norm_w, fox_k_norm_w, w_out, norm_ffn_w, router_group_w, router_group_b, router_expert_w, router_expert_b, exp_w_gate, exp_w_up, exp_w_down):
    B, T, D = x.shape
    x2 = x.reshape(B * T, D)
    for l in range(w_in.shape[0]):
        y_rw, y_fox = _mixer(x2, B, T, norm_mix_w[l], w_in[l], mu_shift[l], rw_w0[l], rw_w2[l], rw_a0[l],
                             rw_a2[l], rw_g2[l], rw_k_k[l], rw_k_a[l], rw_r_k[l].reshape(-1), rw_ln_w[l],
                             rw_ln_b[l], fox_b_f[l], fox_q_norm_w[l], fox_k_norm_w[l])
        x2 = _moe(y_rw, y_fox, x2, w_out[l], norm_ffn_w[l], router_group_w[l], router_group_b[l],
                  router_expert_w[l], router_expert_b[l], exp_w_gate[l], exp_w_up[l], exp_w_down[l])
    return x2.reshape(B, T, D)
```

```python
import functools

import jax
import jax.numpy as jnp
import numpy as np
from jax import lax
from jax.experimental import pallas as pl
from jax.experimental.pallas import tpu as pltpu

F32, BF16, I32 = jnp.float32, jnp.bfloat16, jnp.int32

HEAD = 64
CHUNK = 64
GROUP = 2
GW = GROUP * HEAD
SCAN_BATCH = 2
KV_UNROLL = 2
N_GROUPS = 8
EXPERTS_PER_GROUP = 8
N_EXPERTS = N_GROUPS * EXPERTS_PER_GROUP
ROUTER_LANES = 128
ROW_BLOCK = 256
NORM_EPS = 1e-6
GN_EPS = 64e-5
NEG_BIG = -1e30
LOG2E = 1.4426950408889634
VMEM_LIMIT = 56 * 1024 * 1024


def _cparams(sem):
    return pltpu.CompilerParams(dimension_semantics=sem, vmem_limit_bytes=VMEM_LIMIT)


def _dot(a, b):
    return jnp.dot(a, b, preferred_element_type=F32)


def _dot_nt(a, b):
    return lax.dot_general(a, b, (((1,), (1,)), ((), ())), preferred_element_type=F32)


def _dot_tn(a, b):
    return lax.dot_general(a, b, (((0,), (0,)), ((), ())), preferred_element_type=F32)


def _split3(x):
    hi = x.astype(BF16)
    r1 = x - hi.astype(F32)
    mid = r1.astype(BF16)
    lo = (r1 - mid.astype(F32)).astype(BF16)
    return hi, mid, lo


def _sigmoid(z):
    return 1.0 / (1.0 + jnp.exp(-z))


def _head_sum(x):
    cols = []
    for c in range(x.shape[1] // 128):
        xc = x[:, c * 128:(c + 1) * 128]
        first = lax.broadcasted_iota(I32, xc.shape, 1) < HEAD
        s0 = jnp.sum(jnp.where(first, xc, 0.0), axis=-1, keepdims=True)
        s1 = jnp.sum(jnp.where(first, 0.0, xc), axis=-1, keepdims=True)
        cols.append(jnp.where(first, s0, s1))
    return cols[0] if len(cols) == 1 else jnp.concatenate(cols, axis=1)


def _store_token_major(ref, x):
    rows, d = x.shape
    for s in range(d // 128):
        ref[pl.ds(s, rows, stride=d // 128), :] = x[:, s * 128:(s + 1) * 128]


def _load_token_major(ref, rows, d, s):
    return ref[pl.ds(s, rows, stride=d // 128), :]


def _in_proj_kernel(x_ref, nw_ref, wrw_ref, wqkv_ref, wf_ref, prw_ref, pqkv_ref, pf_ref):
    x = x_ref[...]
    h = x * lax.rsqrt(jnp.mean(x * x, axis=-1, keepdims=True) + NORM_EPS) * nw_ref[...]
    hb = h.astype(BF16)
    prw_ref[...] = _dot(hb, wrw_ref[...])
    pqkv_ref[...] = _dot(hb, wqkv_ref[...])
    pf_ref[...] = _dot(hb, wf_ref[...])


def _in_proj(x2, norm_w, w_rw, w_qkv, w_f):
    M, D = x2.shape
    tm = min(512, M)
    n_rw, n_qkv, n_f = w_rw.shape[1], w_qkv.shape[1], w_f.shape[1]
    full = lambda shape: pl.BlockSpec(shape, lambda i: (0, 0))
    return pl.pallas_call(
        _in_proj_kernel,
        grid=(M // tm,),
        in_specs=[pl.BlockSpec((tm, D), lambda i: (i, 0)), full((1, D)),
                  full((D, n_rw)), full((D, n_qkv)), full((D, n_f))],
        out_specs=[pl.BlockSpec((tm, n_rw), lambda i: (i, 0)),
                   pl.BlockSpec((tm, n_qkv), lambda i: (i, 0)),
                   pl.BlockSpec((tm, n_f), lambda i: (i, 0))],
        out_shape=[jax.ShapeDtypeStruct((M, n_rw), F32),
                   jax.ShapeDtypeStruct((M, n_qkv), F32),
                   jax.ShapeDtypeStruct((M, n_f), F32)],
        compiler_params=_cparams(("parallel",)),
        name="in_proj",
    )(x2, norm_w, w_rw, w_qkv, w_f)


def _rw_prep_kernel(n_heads, p_ref, pprev_ref, mu_ref, w0_ref, w2_ref, a0_ref, a2_ref, g2_ref,
                    kk_ref, ka_ref, rk_ref,
                    rt_ref, at_ref, bt_ref, kt_ref, bb_ref, kb_ref, v_ref, dl_ref, bonus_ref, g_ref):
    W = n_heads * HEAD
    j = pl.program_id(1)
    p = p_ref[0]
    tt = p.shape[0]
    last_prev = jnp.where(j > 0, pprev_ref[0, 7:8, :], 0.0)
    row = lax.broadcasted_iota(I32, p.shape, 0)
    prev = jnp.where(row == 0, last_prev, pltpu.roll(p, 1, axis=0))
    ps = p + (prev - p) * mu_ref[...]
    r, k, v = ps[:, 0:W], ps[:, W:2 * W], ps[:, 2 * W:3 * W]
    o = 3 * W
    pw, pa, pg = ps[:, o:o + 64], ps[:, o + 64:o + 128], ps[:, o + 128:o + 256]

    z = w0_ref[...] + _dot(jnp.tanh(pw).astype(BF16), w2_ref[...])
    lw = (-np.exp(-0.5)).astype(np.float32) * _sigmoid(z)
    a_sig = _sigmoid(a0_ref[...] + _dot(pa.astype(BF16), a2_ref[...]))
    g_ref[0] = _dot(_sigmoid(pg).astype(BF16), g2_ref[...])

    kk = k * kk_ref[...]
    kk = kk / jnp.maximum(jnp.sqrt(_head_sum(kk * kk)), 1e-12)
    km = k * (1.0 + (a_sig - 1.0) * ka_ref[...])
    a_vec = -kk
    b_vec = kk * a_sig
    bonus_ref[0] = _head_sum(r * km * rk_ref[...]) * v
    v_ref[0] = v.astype(BF16)

    ti = lax.broadcasted_iota(I32, (tt, tt), 0)
    si = lax.broadcasted_iota(I32, (tt, tt), 1)
    same = (ti // CHUNK) == (si // CHUNK)
    tri = jnp.where(same & (ti >= si), 1.0, 0.0).astype(BF16)
    ones = jnp.where(same, 1.0, 0.0).astype(BF16)
    nck = tt // CHUNK
    ci = lax.broadcasted_iota(I32, (nck, tt), 0)
    cs = lax.broadcasted_iota(I32, (nck, tt), 1)
    sel = jnp.where(ci == cs // CHUNK, 1.0, 0.0).astype(BF16)
    hi, mid, lo = _split3(lw)
    cum = _dot(tri, hi) + _dot(tri, mid) + _dot(tri, lo)
    tot = _dot(ones, hi) + _dot(ones, mid) + _dot(ones, lo)
    dl = jnp.exp(_dot(sel, hi) + _dot(sel, mid) + _dot(sel, lo))
    for ck in range(nck):
        dl_ref[0, ck] = dl[ck:ck + 1, :]

    e_in = jnp.exp(cum)
    e_ex = jnp.exp(cum - lw)
    e_inv = jnp.exp(-cum)
    e_bar = jnp.exp(tot - cum)
    rt_ref[0] = (r * e_in).astype(BF16)
    at_ref[0] = (a_vec * e_ex).astype(BF16)
    bt_ref[0] = (b_vec * e_inv).astype(BF16)
    kt_ref[0] = (km * e_inv).astype(BF16)
    bb_ref[0] = (b_vec * e_bar).astype(BF16)
    kb_ref[0] = (km * e_bar).astype(BF16)


def _rw_prep(p_rw, mu, w0, w2, a0, a2, g2, k_k, k_a, r_k, n_heads):
    B, T, P = p_rw.shape
    W = n_heads * HEAD
    tt = min(512, T)
    row = lambda a: pl.BlockSpec(a.shape, lambda b, j: (0, 0))
    tile = lambda w: pl.BlockSpec((1, tt, w), lambda b, j: (b, j, 0))
    bf = jax.ShapeDtypeStruct((B, T, W), BF16)
    f32 = jax.ShapeDtypeStruct((B, T, W), F32)
    return pl.pallas_call(
        functools.partial(_rw_prep_kernel, n_heads),
        grid=(B, T // tt),
        in_specs=[tile(P),
                  pl.BlockSpec((1, 8, P), lambda b, j: (b, jnp.maximum(j * (tt // 8) - 1, 0), 0)),
                  row(mu), row(w0), row(w2), row(a0), row(a2), row(g2), row(k_k), row(k_a), row(r_k)],
        out_specs=[tile(W)] * 7 + [pl.BlockSpec((1, tt // CHUNK, 1, W), lambda b, j: (b, j, 0, 0)), tile(W), tile(W)],
        out_shape=[bf] * 7 + [jax.ShapeDtypeStruct((B, T // CHUNK, 1, W), F32), f32, f32],
        compiler_params=_cparams(("parallel", "parallel")),
        name="rw_prep",
    )(p_rw, p_rw, mu, w0, w2, a0, a2, g2, k_k, k_a, r_k)


def _rw_scan_kernel(rt_ref, at_ref, bt_ref, kt_ref, bb_ref, kb_ref, v_ref, dl_ref,
                    bonus_ref, g_ref, lnw_ref, lnb_ref, y_ref, s_ref):
    C = CHUNK
    nb, tt, W = rt_ref.shape
    chains = [(b, g) for b in range(nb) for g in range(W // GW)]

    @pl.when(pl.program_id(1) == 0)
    def _():
        s_ref[...] = jnp.zeros_like(s_ref)

    ri = lax.broadcasted_iota(I32, (GW, GW), 0)
    ci = lax.broadcasted_iota(I32, (GW, GW), 1)
    tr, tc = ri & (C - 1), ci & (C - 1)
    strict, incl, eye = tr > tc, tr >= tc, ri == ci
    lane_head = lax.broadcasted_iota(I32, (C, GW), 1) // HEAD

    def bd(x):
        return jnp.concatenate([jnp.where(lane_head == h, x, jnp.zeros_like(x)) for h in range(GROUP)], axis=0)

    bf = lambda xs: [x.astype(BF16) for x in xs]

    def chunk(c, carry):
        rows = pl.ds(pl.multiple_of(c * C, C), C)
        ld = lambda ref: [bd(ref[b, rows, g * GW:(g + 1) * GW]) for b, g in chains]
        Rt, At, Bt, Kt, Bb, Kb, V = (ld(r) for r in (rt_ref, at_ref, bt_ref, kt_ref, bb_ref, kb_ref, v_ref))
        Lab = [jnp.where(strict, _dot_nt(a, b), 0.0) for a, b in zip(At, Bt)]
        Lak = bf([jnp.where(strict, _dot_nt(a, k), 0.0) for a, k in zip(At, Kt)])
        Mrb = bf([jnp.where(incl, _dot_nt(r, b), 0.0) for r, b in zip(Rt, Bt)])
        Mrk = bf([jnp.where(incl, _dot_nt(r, k), 0.0) for r, k in zip(Rt, Kt)])
        P = [jnp.where(eye, 1.0, l) for l in Lab]
        Lp = Lab
        for _ in range(5):
            Lpb = bf(Lp)
            Lp = [_dot(x, x) for x in Lpb]
            P = [p + _dot(pb, lb) for p, pb, lb in zip(P, bf(P), bf(Lp))]
        Pb = bf(P)
        Wk = bf([_dot(l, v) for l, v in zip(Lak, V)])
        Ah = [_dot(p, a) for p, a in zip(Pb, At)]
        U0 = [_dot(p, w) for p, w in zip(Pb, Wk)]
        Ahb, U0b = bf(Ah), bf(U0)
        Rh = bf([r.astype(F32) + _dot(m, a) for r, m, a in zip(Rt, Mrb, Ahb)])
        Y0 = [_dot(mb, u) + _dot(mk, v) for mb, u, mk, v in zip(Mrb, U0b, Mrk, V)]
        GT = bf([_dot_tn(a, b) for a, b in zip(Ahb, Bb)])
        HT = [_dot_tn(u, b) + _dot_tn(v, k) for u, b, v, k in zip(U0b, Bb, V, Kb)]
        S0 = [s_ref[i] for i in range(len(chains))]
        S0b = bf(S0)
        Y = [_dot_nt(r, s) + y0 for r, s, y0 in zip(Rh, S0b, Y0)]
        for i, (b, g) in enumerate(chains):
            ls = slice(g * GW, (g + 1) * GW)
            s_ref[i] = S0[i] * dl_ref[b, c, :, ls] + _dot(S0b[i], GT[i]) + HT[i]
            y = sum(Y[i][h * C:(h + 1) * C] for h in range(GROUP))
            mean = _head_sum(y) * (1.0 / HEAD)
            yc = y - mean
            var = _head_sum(yc * yc) * (1.0 / HEAD)
            yn = yc * lax.rsqrt(var + GN_EPS) * lnw_ref[:, ls] + lnb_ref[:, ls]
            y_ref[b, rows, ls] = (yn + bonus_ref[b, rows, ls]) * g_ref[b, rows, ls]
        return carry

    lax.fori_loop(0, tt // C, chunk, 0)


def _rw_scan(rt, at, bt, kt, bb, kb, v, dl, bonus, g, ln_w, ln_b):
    B, T, W = rt.shape
    nb = min(SCAN_BATCH, B)
    tt = min(256, T)
    tile = pl.BlockSpec((nb, tt, W), lambda b, j: (b, j, 0))
    row = pl.BlockSpec((1, W), lambda b, j: (0, 0))
    return pl.pallas_call(
        _rw_scan_kernel,
        grid=(B // nb, T // tt),
        in_specs=[tile] * 7 + [pl.BlockSpec((nb, tt // CHUNK, 1, W), lambda b, j: (b, j, 0, 0)), tile, tile, row, row],
        out_specs=tile,
        out_shape=jax.ShapeDtypeStruct((B, T, W), F32),
        scratch_shapes=[pltpu.VMEM((nb * (W // GW), GW, GW), F32)],
        compiler_params=_cparams(("parallel", "arbitrary")),
        name="rw_scan",
    )(rt, at, bt, kt, bb, kb, v, dl, bonus, g, ln_w, ln_b)


def _fox_prep_kernel(n_heads, qkv_ref, f_ref, bf_ref, qnw_ref, knw_ref,
                     q_ref, k_ref, vt_ref, carry_ref):
    W = n_heads * HEAD
    tt = qkv_ref.shape[1]

    @pl.when(pl.program_id(1) == 0)
    def _():
        carry_ref[...] = jnp.zeros_like(carry_ref)

    qkv = qkv_ref[0]
    q, k, v = qkv[:, 0:W], qkv[:, W:2 * W], qkv[:, 2 * W:3 * W]
    inv_n = 1.0 / HEAD
    qn = q * lax.rsqrt(_head_sum(q * q) * inv_n + NORM_EPS) * (qnw_ref[...] * (HEAD ** -0.5 * LOG2E))
    kn = k * lax.rsqrt(_head_sum(k * k) * inv_n + NORM_EPS) * knw_ref[...]
    for p in range(n_heads // 2):
        vt_ref[0, p, 0] = v[:, p * 128:(p + 1) * 128].T.astype(BF16)

    zf = f_ref[0] + bf_ref[...]
    logf = jnp.minimum(zf, 0.0) - jnp.log(1.0 + jnp.exp(-jnp.abs(zf)))
    ti = lax.broadcasted_iota(I32, (tt, tt), 0)
    si = lax.broadcasted_iota(I32, (tt, tt), 1)
    tri = jnp.where(ti >= si, 1.0, 0.0).astype(BF16)
    hi, mid, lo = _split3(logf)
    c = carry_ref[...] + _dot(tri, hi) + _dot(tri, mid) + _dot(tri, lo)
    carry_ref[...] = c[tt - 1:tt, :]
    c_hi, c_mid, c_lo = _split3(c * (-LOG2E))
    lane = lax.broadcasted_iota(I32, (tt, 128), 1)
    c3 = jnp.where(lane < 8, c_hi.astype(F32),
                   jnp.where(lane < 16, pltpu.roll(c_mid.astype(F32), 8, axis=1),
                             pltpu.roll(c_lo.astype(F32), 16, axis=1))).astype(BF16)
    src = lax.broadcasted_iota(I32, (128, 128), 0)
    dst = lax.broadcasted_iota(I32, (128, 128), 1)
    ones = jnp.where((lane >= HEAD) & (lane < HEAD + 3), 1.0, 0.0)
    for h in range(n_heads):
        col = slice((h // 2) * 128, (h // 2 + 1) * 128)
        qh, kh = qn[:, col], kn[:, col]
        if h % 2:
            qh, kh = pltpu.roll(qh, HEAD, axis=1), pltpu.roll(kh, HEAD, axis=1)
        sel = jnp.where((dst >= HEAD) & (dst < HEAD + 3) & (src == h + 8 * (dst - HEAD)), 1.0, 0.0).astype(BF16)
        q_ref[0, h] = jnp.where(lane < HEAD, qh, ones).astype(BF16)
        k_ref[0, h] = jnp.where(lane < HEAD, kh, _dot(c3, sel)).astype(BF16)


def _fox_prep(qkv, f, b_f, qn_w, kn_w, n_heads):
    B, T, _ = qkv.shape
    W = n_heads * HEAD
    tt = min(256, T)
    row = lambda a: pl.BlockSpec(a.shape, lambda b, j: (0, 0))
    aug = jax.ShapeDtypeStruct((B, n_heads, T, 128), BF16)
    aug_spec = pl.BlockSpec((1, n_heads, tt, 128), lambda b, j: (b, 0, j, 0))
    return pl.pallas_call(
        functools.partial(_fox_prep_kernel, n_heads),
        grid=(B, T // tt),
        in_specs=[pl.BlockSpec((1, tt, 3 * W), lambda b, j: (b, j, 0)),
                  pl.BlockSpec((1, tt, 128), lambda b, j: (b, j, 0)),
                  row(b_f), row(qn_w), row(kn_w)],
        out_specs=[aug_spec, aug_spec,
                   pl.BlockSpec((1, n_heads // 2, 1, 128, tt), lambda b, j: (b, 0, j, 0, 0))],
        out_shape=[aug, aug, jax.ShapeDtypeStruct((B, n_heads // 2, T // tt, 128, tt), BF16)],
        scratch_shapes=[pltpu.VMEM((1, 128), F32)],
        compiler_params=_cparams(("parallel", "arbitrary")),
        name="fox_prep",
    )(qkv, f, b_f, qn_w, kn_w)


def _fox_attn_kernel(q_ref, k_ref, vt_ref, o_ref):
    i = pl.program_id(2)
    tq = q_ref.shape[2]
    sub = vt_ref.shape[4]
    q = [q_ref[0, hh] for hh in range(2)]
    ki = lax.broadcasted_iota(I32, (tq, tq), 0)
    qi = lax.broadcasted_iota(I32, (tq, tq), 1)

    def scores(j, diag=False):
        k0 = pl.multiple_of(j * tq, tq)
        s = [_dot_nt(k_ref[0, hh, pl.ds(k0, tq), :], q[hh]) for hh in range(2)]
        return [jnp.where(ki <= qi, x, -jnp.inf) for x in s] if diag else s

    def update(j, s, carry):
        out = []
        for hh in range(2):
            m, l, acc = carry[hh]
            m_new = jnp.maximum(m, jnp.max(s[hh], axis=0, keepdims=True))
            alpha = jnp.exp2(m - m_new)
            p = jnp.exp2(s[hh] - m_new)
            l = alpha * l + jnp.sum(p, axis=0, keepdims=True)
            p = p.astype(BF16)
            pv = sum(_dot(vt_ref[0, 0, j * (tq // sub) + u], p[u * sub:(u + 1) * sub]) for u in range(tq // sub))
            out.append((m_new, l, alpha * acc + pv))
        return tuple(out)

    def several_blocks(jj, carry):
        ss = [scores(KV_UNROLL * jj + u) for u in range(KV_UNROLL)]
        for u in range(KV_UNROLL):
            carry = update(KV_UNROLL * jj + u, ss[u], carry)
        return carry

    init = tuple((jnp.full((1, tq), NEG_BIG, F32), jnp.zeros((1, tq), F32), jnp.zeros((128, tq), F32))
                 for _ in range(2))
    carry = lax.fori_loop(0, i // KV_UNROLL, several_blocks, init)
    carry = lax.fori_loop(i - i % KV_UNROLL, i, lambda j, c: update(j, scores(j), c), carry)
    (_, l0, acc0), (_, l1, acc1) = update(i, scores(i, diag=True), carry)
    row = lax.broadcasted_iota(I32, (128, tq), 0)
    o_ref[0] = jnp.where(row < HEAD, acc0 / l0, acc1 / l1).T


def _fox_attn(q_aug, k_aug, vt):
    B, H, T, _ = q_aug.shape
    tq = min(512, T)
    n_sub, sub = vt.shape[2], vt.shape[4]
    return pl.pallas_call(
        _fox_attn_kernel,
        grid=(B, H // 2, T // tq),
        in_specs=[pl.BlockSpec((1, 2, tq, 128), lambda b, p, i: (b, p, i, 0)),
                  pl.BlockSpec((1, 2, T, 128), lambda b, p, i: (b, p, 0, 0)),
                  pl.BlockSpec((1, 1, n_sub, 128, sub), lambda b, p, i: (b, p, 0, 0, 0))],
        out_specs=pl.BlockSpec((1, tq, 128), lambda b, p, i: (b, i, p)),
        out_shape=jax.ShapeDtypeStruct((B, T, H * HEAD), F32),
        compiler_params=_cparams(("parallel", "parallel", "arbitrary")),
        name="fox_attn",
    )(q_aug, k_aug, vt)


def _out_route_kernel(yrw_ref, yfox_ref, x_ref, wo_rw_ref, wo_fox_ref, nw_ref, wr_hi_ref, wr_lo_ref, br_ref,
                      x1_ref, h2_ref, eid_ref, gate_ref):
    x1 = (x_ref[...] + _dot(yrw_ref[...].astype(BF16), wo_rw_ref[...])
          + _dot(yfox_ref[...].astype(BF16), wo_fox_ref[...]))
    x1_ref[...] = x1
    h2 = x1 * lax.rsqrt(jnp.mean(x1 * x1, axis=-1, keepdims=True) + NORM_EPS) * nw_ref[...]
    _store_token_major(h2_ref, h2)
    h_hi = h2.astype(BF16)
    h_lo = (h2 - h_hi.astype(F32)).astype(BF16)
    logits = (_dot(h_hi, wr_hi_ref[...]) + _dot(h_hi, wr_lo_ref[...]) + _dot(h_lo, wr_hi_ref[...])) + br_ref[...]

    lane_i = lax.broadcasted_iota(I32, logits.shape, 1)
    lane = lane_i.astype(F32)
    first = lambda mask: jnp.min(jnp.where(mask, lane, 1e9), axis=-1, keepdims=True)
    gl = jnp.where(lane_i < N_GROUPS, logits, -jnp.inf)
    gmax = jnp.max(gl, axis=-1, keepdims=True)
    g_sel = first(gl == gmax)
    g_gate = 1.0 / jnp.sum(jnp.exp(gl - gmax), axis=-1, keepdims=True)
    e_lane = lane_i - N_GROUPS
    lane_grp = jnp.right_shift(e_lane, 3).astype(F32)
    in_grp = (e_lane >= 0) & (e_lane < N_EXPERTS) & (lane_grp == g_sel)
    el = jnp.where(in_grp, logits, -jnp.inf)
    m1 = jnp.max(el, axis=-1, keepdims=True)
    i1 = first(el == m1)
    el2 = jnp.where(lane == i1, -jnp.inf, el)
    m2 = jnp.max(el2, axis=-1, keepdims=True)
    i2 = first(el2 == m2)
    e2 = jnp.exp(m2 - m1)
    g1 = g_gate / (1.0 + e2)
    g2 = g_gate * e2 / (1.0 + e2)
    eid_ref[...] = jnp.where(lane_i == 0, i1 - N_GROUPS, jnp.where(lane_i == 1, i2 - N_GROUPS, 0.0)).astype(I32)
    gate_ref[...] = jnp.where(lane_i == 0, g1, jnp.where(lane_i == 1, g2, 0.0))


def _out_route(y_rw, y_fox, x2, wo_rw, wo_fox, norm_w, wr_hi, wr_lo, b_r):
    M, D = x2.shape
    W = y_rw.shape[1]
    tm = min(256, M)
    full = lambda a: pl.BlockSpec(a.shape, lambda i: (0, 0))
    tile = lambda w: pl.BlockSpec((tm, w), lambda i: (i, 0))
    return pl.pallas_call(
        _out_route_kernel,
        grid=(M // tm,),
        in_specs=[tile(W), tile(W), tile(D), full(wo_rw), full(wo_fox), full(norm_w),
                  full(wr_hi), full(wr_lo), full(b_r)],
        out_specs=[tile(D), pl.BlockSpec((tm * (D // 128), 128), lambda i: (i, 0)),
                   tile(ROUTER_LANES), tile(ROUTER_LANES)],
        out_shape=[jax.ShapeDtypeStruct((M, D), F32), jax.ShapeDtypeStruct((M * (D // 128), 128), F32),
                   jax.ShapeDtypeStruct((M, ROUTER_LANES), I32), jax.ShapeDtypeStruct((M, ROUTER_LANES), F32)],
        compiler_params=_cparams(("parallel",)),
        name="out_route",
    )(y_rw, y_fox, x2, wo_rw, wo_fox, norm_w, wr_hi, wr_lo, b_r)


def _rank_kernel(eid_ref, rank_ref, cnt_ref, carry_ref):
    @pl.when(pl.program_id(0) == 0)
    def _():
        carry_ref[...] = jnp.zeros_like(carry_ref)

    eid = eid_ref[...].astype(F32)
    tm = eid.shape[0]
    lane = lax.broadcasted_iota(I32, eid.shape, 1)
    lane_f = lane.astype(F32)
    pick = lambda l: jnp.sum(jnp.where(lane == l, eid, 0.0), axis=-1, keepdims=True)
    e0, e1 = pick(0), pick(1)
    oh0 = (lane_f == e0).astype(F32)
    oh1 = (lane_f == e1).astype(F32)
    both = oh0 + oh1
    ri = lax.broadcasted_iota(I32, (tm, tm), 0)
    ci = lax.broadcasted_iota(I32, (tm, tm), 1)
    before = _dot(jnp.where(ri > ci, 1.0, 0.0).astype(BF16), both.astype(BF16)) + carry_ref[...]
    r0 = jnp.sum(oh0 * before, axis=-1, keepdims=True)
    r1 = jnp.sum(oh1 * (before + oh0), axis=-1, keepdims=True)
    rank_ref[...] = jnp.where(lane == 0, r0, jnp.where(lane == 1, r1, 0.0)).astype(I32)
    total = carry_ref[...] + jnp.sum(both, axis=0, keepdims=True)
    carry_ref[...] = total
    cnt_ref[...] = jnp.broadcast_to(total, cnt_ref.shape).astype(I32)


def _rank(eid):
    M = eid.shape[0]
    tm = min(512, M)
    return pl.pallas_call(
        _rank_kernel,
        grid=(M // tm,),
        in_specs=[pl.BlockSpec((tm, ROUTER_LANES), lambda i: (i, 0))],
        out_specs=[pl.BlockSpec((tm, ROUTER_LANES), lambda i: (i, 0)),
                   pl.BlockSpec((8, ROUTER_LANES), lambda i: (0, 0))],
        out_shape=[jax.ShapeDtypeStruct((M, ROUTER_LANES), I32), jax.ShapeDtypeStruct((8, ROUTER_LANES), I32)],
        scratch_shapes=[pltpu.VMEM((1, ROUTER_LANES), F32)],
        compiler_params=_cparams(("arbitrary",)),
        name="rank",
    )(eid)


def _dest_kernel(eid_ref, rank_ref, start_ref, dest_ref):
    eid = eid_ref[...].astype(F32)
    rank = rank_ref[...].astype(F32)
    lane = lax.broadcasted_iota(I32, eid.shape, 1)
    lane_f = lane.astype(F32)
    pick = lambda x, l: jnp.sum(jnp.where(lane == l, x, 0.0), axis=-1, keepdims=True)
    base = lambda e: jnp.sum(jnp.where(lane_f == e, start_ref[...], 0.0), axis=-1, keepdims=True)
    d0 = base(pick(eid, 0)) + pick(rank, 0)
    d1 = base(pick(eid, 1)) + pick(rank, 1)
    dest_ref[...] = jnp.where(lane == 0, d0, jnp.where(lane == 1, d1, 0.0)).astype(I32)


def _dest(eid, rank, pad_start_row):
    M = eid.shape[0]
    tm = min(1024, M)
    tile = pl.BlockSpec((tm, ROUTER_LANES), lambda i: (i, 0))
    return pl.pallas_call(
        _dest_kernel,
        grid=(M // tm,),
        in_specs=[tile, tile, pl.BlockSpec((1, ROUTER_LANES), lambda i: (0, 0))],
        out_specs=tile,
        out_shape=jax.ShapeDtypeStruct((M, ROUTER_LANES), I32),
        compiler_params=_cparams(("parallel",)),
        name="dest",
    )(eid, rank, pad_start_row)


def _invert_kernel(dest_ref, cnt_ref, start_ref, end_ref, tok_ref):
    def clear(r, carry):
        tok_ref[r] = 0
        return carry

    def clear_expert(e, carry):
        lax.fori_loop(start_ref[e] + cnt_ref[e], end_ref[e], clear, 0)
        return carry

    def put(a, carry):
        tok_ref[dest_ref[a]] = lax.shift_right_logical(a, 1)
        return carry

    lax.fori_loop(0, cnt_ref.shape[0], clear_expert, 0)
    lax.fori_loop(end_ref[end_ref.shape[0] - 1], tok_ref.shape[0], clear, 0)
    lax.fori_loop(0, dest_ref.shape[0], put, 0, unroll=8)


def _invert(dest_flat, counts, pad_start, pad_end, n_rows):
    smem = pl.BlockSpec(memory_space=pltpu.SMEM)
    return pl.pallas_call(
        _invert_kernel,
        in_specs=[smem] * 4,
        out_specs=smem,
        out_shape=jax.ShapeDtypeStruct((n_rows,), I32),
        name="invert",
    )(dest_flat, counts, pad_start, pad_end)


def _experts_kernel(be_ref, nused_ref, tok_ref, h_hbm, wg_ref, wu_ref, wd_ref, yb_ref,
                    xbuf, sem, wg_bf, wu_bf, wd_bf):
    i = pl.program_id(0)
    D, F = wg_bf.shape
    nt = D // 128
    rb = yb_ref.shape[0] // nt
    n_used = nused_ref[0]
    slot = i & 1

    def copy(blk, r, s):
        src = pl.multiple_of(tok_ref[blk * rb + r] * nt, nt)
        return pltpu.make_async_copy(h_hbm.at[pl.ds(src, nt)], xbuf.at[s, pl.ds(r * nt, nt)], sem.at[s])

    def wait_block(blk, s):
        def body(r, carry):
            copy(blk, r, s).wait()
            return carry
        lax.fori_loop(0, rb, body, 0, unroll=8)

    @pl.when(i == 0)
    def _():
        def body(r, carry):
            copy(0, r, 0).start()
            return carry
        lax.fori_loop(0, rb, body, 0, unroll=8)

    prev = be_ref[jnp.maximum(i - 1, 0)]

    @pl.when((i == 0) | (be_ref[i] != prev))
    def _():
        wg_bf[...] = wg_ref[0].astype(BF16)
        wu_bf[...] = wu_ref[0].astype(BF16)
        wd_bf[...] = wd_ref[0].astype(BF16)

    @pl.when(i < n_used)
    def _():
        wait_block(i, slot)

        def start_next(r, carry):
            copy(i + 1, 2 * r, 1 - slot).start(priority=0)
            copy(i + 1, 2 * r + 1, 1 - slot).start(priority=1)
            return carry
        lax.fori_loop(0, rb // 2, start_next, 0, unroll=4)

        xb = jnp.concatenate([_load_token_major(xbuf.at[slot], rb, D, s).astype(BF16) for s in range(nt)], axis=1)
        gate = _dot(xb, wg_bf[...])
        up = _dot(xb, wu_bf[...])
        hid = gate * _sigmoid(gate) * up
        _store_token_major(yb_ref, _dot(hid.astype(BF16), wd_bf[...]))

    @pl.when(i == n_used)
    def _():
        wait_block(i, slot)

    @pl.when(i >= n_used)
    def _():
        yb_ref[...] = jnp.zeros_like(yb_ref)


def _experts(block_expert, n_used, tok, h2, w_gate, w_up, w_down, rb):
    E, D, F = w_gate.shape
    nt = D // 128
    nb = tok.shape[0] // rb
    grid_spec = pltpu.PrefetchScalarGridSpec(
        num_scalar_prefetch=3,
        grid=(nb,),
        in_specs=[pl.BlockSpec(memory_space=pl.ANY),
                  pl.BlockSpec((1, D, F), lambda i, be, nu, tk: (be[i], 0, 0)),
                  pl.BlockSpec((1, D, F), lambda i, be, nu, tk: (be[i], 0, 0)),
                  pl.BlockSpec((1, F, D), lambda i, be, nu, tk: (be[i], 0, 0))],
        out_specs=pl.BlockSpec((rb * nt, 128), lambda i, be, nu, tk: (i, 0)),
        scratch_shapes=[pltpu.VMEM((2, rb * nt, 128), F32), pltpu.SemaphoreType.DMA((2,)),
                        pltpu.VMEM((D, F), BF16), pltpu.VMEM((D, F), BF16), pltpu.VMEM((F, D), BF16)],
    )
    return pl.pallas_call(
        _experts_kernel,
        grid_spec=grid_spec,
        out_shape=jax.ShapeDtypeStruct((nb * rb * nt, 128), F32),
        compiler_params=_cparams(("arbitrary",)),
        name="experts",
    )(block_expert, n_used, tok, h2, w_gate, w_up, w_down)


def _combine_kernel(dest_ref, gate_ref, x1_ref, yb_hbm, out_ref, buf, sem):
    i = pl.program_id(0)
    tmc, D = x1_ref.shape
    nt = D // 128
    slot = i & 1

    def copy(tile, t, k, s):
        src = pl.multiple_of(dest_ref[2 * (tile * tmc + t) + k] * nt, nt)
        return pltpu.make_async_copy(yb_hbm.at[pl.ds(src, nt)], buf.at[s, k, pl.ds(t * nt, nt)], sem.at[s])

    def start_tile(tile, s):
        def body(t, carry):
            copy(tile, t, 0, s).start(priority=0)
            copy(tile, t, 1, s).start(priority=1)
            return carry
        lax.fori_loop(0, tmc, body, 0, unroll=8)

    @pl.when(i == 0)
    def _():
        start_tile(0, 0)

    @pl.when(i + 1 < pl.num_programs(0))
    def _():
        start_tile(i + 1, 1 - slot)

    def wait(t, carry):
        copy(i, t, 0, slot).wait()
        copy(i, t, 1, slot).wait()
        return carry

    lax.fori_loop(0, tmc, wait, 0, unroll=8)
    gate = gate_ref[...]
    g0, g1 = gate[:, 0:1], gate[:, 1:2]
    for s in range(nt):
        cols = slice(s * 128, (s + 1) * 128)
        out_ref[:, cols] = (x1_ref[:, cols] + g0 * _load_token_major(buf.at[slot, 0], tmc, D, s)
                            + g1 * _load_token_major(buf.at[slot, 1], tmc, D, s))


def _combine(dest_flat, gates, x1, yb):
    M, D = x1.shape
    tmc = min(256, M)
    grid_spec = pltpu.PrefetchScalarGridSpec(
        num_scalar_prefetch=1,
        grid=(M // tmc,),
        in_specs=[pl.BlockSpec((tmc, ROUTER_LANES), lambda i, d: (i, 0)),
                  pl.BlockSpec((tmc, D), lambda i, d: (i, 0)),
                  pl.BlockSpec(memory_space=pl.ANY)],
        out_specs=pl.BlockSpec((tmc, D), lambda i, d: (i, 0)),
        scratch_shapes=[pltpu.VMEM((2, 2, tmc * (D // 128), 128), F32), pltpu.SemaphoreType.DMA((2,))],
    )
    return pl.pallas_call(
        _combine_kernel,
        grid_spec=grid_spec,
        out_shape=jax.ShapeDtypeStruct((M, D), F32),
        compiler_params=_cparams(("arbitrary",)),
        name="combine",
    )(dest_flat, gates, x1, yb)


def _mixer(x2, B, T, norm_w, w_in, mu, w0, w2, a0, a2, g2, k_k, k_a, r_k, ln_w, ln_b, b_f, qn_w, kn_w):
    rw_heads = w0.shape[0] // HEAD
    fox_heads = b_f.shape[0]
    Wr, Wf = rw_heads * HEAD, fox_heads * HEAD
    lora = w2.shape[0] + a2.shape[0] + g2.shape[0]
    rw_cols = 3 * Wr + lora
    o_w, o_k, o_v, o_a = Wr, Wr + w2.shape[0], 2 * Wr + w2.shape[0], 3 * Wr + w2.shape[0]
    perm = np.concatenate([np.arange(0, Wr), np.arange(o_k, o_k + Wr), np.arange(o_v, o_v + Wr),
                           np.arange(o_w, o_w + w2.shape[0]), np.arange(o_a, rw_cols)])
    w_rw = w_in[:, :rw_cols][:, perm].astype(BF16)
    w_qkv = w_in[:, rw_cols:rw_cols + 3 * Wf].astype(BF16)
    w_f = jnp.pad(w_in[:, rw_cols + 3 * Wf:], ((0, 0), (0, 128 - fox_heads))).astype(BF16)
    p_rw, p_qkv, p_f = _in_proj(x2, norm_w[None, :], w_rw, w_qkv, w_f)

    row = lambda a: a.reshape(1, -1)
    ops = _rw_prep(p_rw.reshape(B, T, rw_cols), row(mu[perm]), row(w0), w2.astype(BF16), row(a0),
                   a2.astype(BF16), g2.astype(BF16), row(k_k), row(k_a), row(r_k), rw_heads)
    y_rw = _rw_scan(*ops, row(ln_w), row(ln_b))

    tile_w = lambda w: row(jnp.tile(w, fox_heads))
    q_aug, k_aug, vt = _fox_prep(p_qkv.reshape(B, T, 3 * Wf), p_f.reshape(B, T, 128),
                                 row(jnp.pad(b_f, (0, 128 - fox_heads))), tile_w(qn_w), tile_w(kn_w), fox_heads)
    y_fox = _fox_attn(q_aug, k_aug, vt)
    return y_rw.reshape(B * T, Wr), y_fox.reshape(B * T, Wf)


def _moe(y_rw, y_fox, x2, w_out, norm_w, rg_w, rg_b, re_w, re_b, w_gate, w_up, w_down):
    M, D = x2.shape
    Wr = y_rw.shape[1]
    pad = ROUTER_LANES - N_GROUPS - N_EXPERTS
    w_r = jnp.pad(jnp.concatenate([rg_w, re_w], axis=1), ((0, 0), (0, pad)))
    b_r = jnp.pad(jnp.concatenate([rg_b, re_b]), (0, pad))[None, :]
    wr_hi = w_r.astype(BF16)
    wr_lo = (w_r - wr_hi.astype(F32)).astype(BF16)
    x1, h2, eid, gates = _out_route(y_rw, y_fox, x2, w_out[:Wr].astype(BF16), w_out[Wr:].astype(BF16),
                                    norm_w[None, :], wr_hi, wr_lo, b_r)
    rank, counts = _rank(eid)

    rb = ROW_BLOCK
    counts = counts[0, :N_EXPERTS]
    padded = (counts + rb - 1) // rb * rb
    pad_end = jnp.cumsum(padded)
    pad_start = pad_end - padded
    n_blocks = (2 * M + N_EXPERTS * (rb - 1) + rb - 1) // rb + 1
    block_start = jnp.arange(n_blocks, dtype=I32) * rb
    block_expert = jnp.minimum(jnp.sum(pad_end[None, :] <= block_start[:, None], axis=1), N_EXPERTS - 1).astype(I32)
    n_used = (pad_end[-1:] // rb).astype(I32)
    start_row = jnp.pad(pad_start.astype(F32), (0, ROUTER_LANES - N_EXPERTS))[None, :]
    dest = _dest(eid, rank, start_row)[:, :2].reshape(-1)

    tok = _invert(dest, counts, pad_start.astype(I32), pad_end.astype(I32), n_blocks * rb)
    yb = _experts(block_expert, n_used, tok, h2, w_gate, w_up, w_down, rb)
    return _combine(dest, gates, x1, yb)


def kernel(x, norm_mix_w, w_in, mu_shift, rw_w0, rw_w2, rw_a0, rw_a2, rw_g2, rw_k_k, rw_k_a, rw_r_k, rw_ln_w, rw_ln_b, fox_b_f, fox_q_norm_w, fox_k_norm_w, w_out, norm_ffn_w, router_group_w, router_group_b, router_expert_w, router_expert_b, exp_w_gate, exp_w_up, exp_w_down):
    B, T, D = x.shape
    x2 = x.reshape(B * T, D)
    for l in range(w_in.shape[0]):
        y_rw, y_fox = _mixer(x2, B, T, norm_mix_w[l], w_in[l], mu_shift[l], rw_w0[l], rw_w2[l], rw_a0[l],
                             rw_a2[l], rw_g2[l], rw_k_k[l], rw_k_a[l], rw_r_k[l].reshape(-1), rw_ln_w[l],
                             rw_ln_b[l], fox_b_f[l], fox_q_norm_w[l], fox_k_norm_w[l])
        x2 = _moe(y_rw, y_fox, x2, w_out[l], norm_ffn_w[l], router_group_w[l], router_group_b[l],
                  router_expert_w[l], router_expert_b[l], exp_w_gate[l], exp_w_up[l], exp_w_down[l])
    return x2.reshape(B, T, D)
```

```python
import functools

import jax
import jax.numpy as jnp
import numpy as np
from jax import lax
from jax.experimental import pallas as pl
from jax.experimental.pallas import tpu as pltpu

F32, BF16, I32 = jnp.float32, jnp.bfloat16, jnp.int32

HEAD = 64
CHUNK = 64
GROUP = 2
GW = GROUP * HEAD
SCAN_BATCH = 2
N_GROUPS = 8
EXPERTS_PER_GROUP = 8
N_EXPERTS = N_GROUPS * EXPERTS_PER_GROUP
ROUTER_LANES = 128
ROW_BLOCK = 256
NORM_EPS = 1e-6
GN_EPS = 64e-5
NEG_BIG = -1e30
LOG2E = 1.4426950408889634
VMEM_LIMIT = 56 * 1024 * 1024


def _cparams(sem):
    return pltpu.CompilerParams(dimension_semantics=sem, vmem_limit_bytes=VMEM_LIMIT)


def _dot(a, b):
    return jnp.dot(a, b, preferred_element_type=F32)


def _dot_nt(a, b):
    return lax.dot_general(a, b, (((1,), (1,)), ((), ())), preferred_element_type=F32)


def _dot_tn(a, b):
    return lax.dot_general(a, b, (((0,), (0,)), ((), ())), preferred_element_type=F32)


def _split3(x):
    hi = x.astype(BF16)
    r1 = x - hi.astype(F32)
    mid = r1.astype(BF16)
    lo = (r1 - mid.astype(F32)).astype(BF16)
    return hi, mid, lo


def _sigmoid(z):
    return 1.0 / (1.0 + jnp.exp(-z))


def _head_sum(x):
    cols = []
    for c in range(x.shape[1] // 128):
        xc = x[:, c * 128:(c + 1) * 128]
        first = lax.broadcasted_iota(I32, xc.shape, 1) < HEAD
        s0 = jnp.sum(jnp.where(first, xc, 0.0), axis=-1, keepdims=True)
        s1 = jnp.sum(jnp.where(first, 0.0, xc), axis=-1, keepdims=True)
        cols.append(jnp.where(first, s0, s1))
    return cols[0] if len(cols) == 1 else jnp.concatenate(cols, axis=1)


def _store_token_major(ref, x):
    rows, d = x.shape
    for s in range(d // 128):
        ref[pl.ds(s, rows, stride=d // 128), :] = x[:, s * 128:(s + 1) * 128]


def _load_token_major(ref, rows, d, s):
    return ref[pl.ds(s, rows, stride=d // 128), :]


def _in_proj_kernel(x_ref, nw_ref, wrw_ref, wqkv_ref, wf_ref, prw_ref, pqkv_ref, pf_ref):
    x = x_ref[...]
    h = x * lax.rsqrt(jnp.mean(x * x, axis=-1, keepdims=True) + NORM_EPS) * nw_ref[...]
    hb = h.astype(BF16)
    prw_ref[...] = _dot(hb, wrw_ref[...])
    pqkv_ref[...] = _dot(hb, wqkv_ref[...])
    pf_ref[...] = _dot(hb, wf_ref[...])


def _in_proj(x2, norm_w, w_rw, w_qkv, w_f):
    M, D = x2.shape
    tm = min(512, M)
    n_rw, n_qkv, n_f = w_rw.shape[1], w_qkv.shape[1], w_f.shape[1]
    full = lambda shape: pl.BlockSpec(shape, lambda i: (0, 0))
    return pl.pallas_call(
        _in_proj_kernel,
        grid=(M // tm,),
        in_specs=[pl.BlockSpec((tm, D), lambda i: (i, 0)), full((1, D)),
                  full((D, n_rw)), full((D, n_qkv)), full((D, n_f))],
        out_specs=[pl.BlockSpec((tm, n_rw), lambda i: (i, 0)),
                   pl.BlockSpec((tm, n_qkv), lambda i: (i, 0)),
                   pl.BlockSpec((tm, n_f), lambda i: (i, 0))],
        out_shape=[jax.ShapeDtypeStruct((M, n_rw), F32),
                   jax.ShapeDtypeStruct((M, n_qkv), F32),
                   jax.ShapeDtypeStruct((M, n_f), F32)],
        compiler_params=_cparams(("parallel",)),
        name="in_proj",
    )(x2, norm_w, w_rw, w_qkv, w_f)


def _rw_prep_kernel(n_heads, p_ref, pprev_ref, mu_ref, w0_ref, w2_ref, a0_ref, a2_ref, g2_ref,
                    kk_ref, ka_ref, rk_ref,
                    rt_ref, at_ref, bt_ref, kt_ref, bb_ref, kb_ref, v_ref, dl_ref, bonus_ref, g_ref):
    W = n_heads * HEAD
    j = pl.program_id(1)
    p = p_ref[0]
    tt = p.shape[0]
    last_prev = jnp.where(j > 0, pprev_ref[0, 7:8, :], 0.0)
    row = lax.broadcasted_iota(I32, p.shape, 0)
    prev = jnp.where(row == 0, last_prev, pltpu.roll(p, 1, axis=0))
    ps = p + (prev - p) * mu_ref[...]
    r, k, v = ps[:, 0:W], ps[:, W:2 * W], ps[:, 2 * W:3 * W]
    o = 3 * W
    pw, pa, pg = ps[:, o:o + 64], ps[:, o + 64:o + 128], ps[:, o + 128:o + 256]

    z = w0_ref[...] + _dot(jnp.tanh(pw).astype(BF16), w2_ref[...])
    lw = (-np.exp(-0.5)).astype(np.float32) * _sigmoid(z)
    a_sig = _sigmoid(a0_ref[...] + _dot(pa.astype(BF16), a2_ref[...]))
    g_ref[0] = _dot(_sigmoid(pg).astype(BF16), g2_ref[...])

    kk = k * kk_ref[...]
    kk = kk / jnp.maximum(jnp.sqrt(_head_sum(kk * kk)), 1e-12)
    km = k * (1.0 + (a_sig - 1.0) * ka_ref[...])
    a_vec = -kk
    b_vec = kk * a_sig
    bonus_ref[0] = _head_sum(r * km * rk_ref[...]) * v
    v_ref[0] = v.astype(BF16)

    ti = lax.broadcasted_iota(I32, (tt, tt), 0)
    si = lax.broadcasted_iota(I32, (tt, tt), 1)
    same = (ti // CHUNK) == (si // CHUNK)
    tri = jnp.where(same & (ti >= si), 1.0, 0.0).astype(BF16)
    ones = jnp.where(same, 1.0, 0.0).astype(BF16)
    nck = tt // CHUNK
    ci = lax.broadcasted_iota(I32, (nck, tt), 0)
    cs = lax.broadcasted_iota(I32, (nck, tt), 1)
    sel = jnp.where(ci == cs // CHUNK, 1.0, 0.0).astype(BF16)
    hi, mid, lo = _split3(lw)
    cum = _dot(tri, hi) + _dot(tri, mid) + _dot(tri, lo)
    tot = _dot(ones, hi) + _dot(ones, mid) + _dot(ones, lo)
    dl = jnp.exp(_dot(sel, hi) + _dot(sel, mid) + _dot(sel, lo))
    for ck in range(nck):
        dl_ref[0, ck] = dl[ck:ck + 1, :]

    e_in = jnp.exp(cum)
    e_ex = jnp.exp(cum - lw)
    e_inv = jnp.exp(-cum)
    e_bar = jnp.exp(tot - cum)
    rt_ref[0] = (r * e_in).astype(BF16)
    at_ref[0] = (a_vec * e_ex).astype(BF16)
    bt_ref[0] = (b_vec * e_inv).astype(BF16)
    kt_ref[0] = (km * e_inv).astype(BF16)
    bb_ref[0] = (b_vec * e_bar).astype(BF16)
    kb_ref[0] = (km * e_bar).astype(BF16)


def _rw_prep(p_rw, mu, w0, w2, a0, a2, g2, k_k, k_a, r_k, n_heads):
    B, T, P = p_rw.shape
    W = n_heads * HEAD
    tt = min(512, T)
    row = lambda a: pl.BlockSpec(a.shape, lambda b, j: (0, 0))
    tile = lambda w: pl.BlockSpec((1, tt, w), lambda b, j: (b, j, 0))
    bf = jax.ShapeDtypeStruct((B, T, W), BF16)
    f32 = jax.ShapeDtypeStruct((B, T, W), F32)
    return pl.pallas_call(
        functools.partial(_rw_prep_kernel, n_heads),
        grid=(B, T // tt),
        in_specs=[tile(P),
                  pl.BlockSpec((1, 8, P), lambda b, j: (b, jnp.maximum(j * (tt // 8) - 1, 0), 0)),
                  row(mu), row(w0), row(w2), row(a0), row(a2), row(g2), row(k_k), row(k_a), row(r_k)],
        out_specs=[tile(W)] * 7 + [pl.BlockSpec((1, tt // CHUNK, 1, W), lambda b, j: (b, j, 0, 0)), tile(W), tile(W)],
        out_shape=[bf] * 7 + [jax.ShapeDtypeStruct((B, T // CHUNK, 1, W), F32), f32, f32],
        compiler_params=_cparams(("parallel", "parallel")),
        name="rw_prep",
    )(p_rw, p_rw, mu, w0, w2, a0, a2, g2, k_k, k_a, r_k)


def _rw_scan_kernel(rt_ref, at_ref, bt_ref, kt_ref, bb_ref, kb_ref, v_ref, dl_ref,
                    bonus_ref, g_ref, lnw_ref, lnb_ref, y_ref, s_ref):
    C = CHUNK
    nb, tt, W = rt_ref.shape
    chains = [(b, g) for b in range(nb) for g in range(W // GW)]

    @pl.when(pl.program_id(1) == 0)
    def _():
        s_ref[...] = jnp.zeros_like(s_ref)

    ri = lax.broadcasted_iota(I32, (GW, GW), 0)
    ci = lax.broadcasted_iota(I32, (GW, GW), 1)
    tr, tc = ri & (C - 1), ci & (C - 1)
    strict, incl, eye = tr > tc, tr >= tc, ri == ci
    lane_head = lax.broadcasted_iota(I32, (C, GW), 1) // HEAD

    def bd(x):
        return jnp.concatenate([jnp.where(lane_head == h, x, jnp.zeros_like(x)) for h in range(GROUP)], axis=0)

    bf = lambda xs: [x.astype(BF16) for x in xs]

    def chunk(c, carry):
        rows = pl.ds(pl.multiple_of(c * C, C), C)
        ld = lambda ref: [bd(ref[b, rows, g * GW:(g + 1) * GW]) for b, g in chains]
        Rt, At, Bt, Kt, Bb, Kb, V = (ld(r) for r in (rt_ref, at_ref, bt_ref, kt_ref, bb_ref, kb_ref, v_ref))
        Lab = [jnp.where(strict, _dot_nt(a, b), 0.0) for a, b in zip(At, Bt)]
        Lak = bf([jnp.where(strict, _dot_nt(a, k), 0.0) for a, k in zip(At, Kt)])
        Mrb = bf([jnp.where(incl, _dot_nt(r, b), 0.0) for r, b in zip(Rt, Bt)])
        Mrk = bf([jnp.where(incl, _dot_nt(r, k), 0.0) for r, k in zip(Rt, Kt)])
        P = [jnp.where(eye, 1.0, l) for l in Lab]
        Lp = Lab
        for _ in range(5):
            Lpb = bf(Lp)
            Lp = [_dot(x, x) for x in Lpb]
            P = [p + _dot(pb, lb) for p, pb, lb in zip(P, bf(P), bf(Lp))]
        Pb = bf(P)
        Wk = bf([_dot(l, v) for l, v in zip(Lak, V)])
        Ah = [_dot(p, a) for p, a in zip(Pb, At)]
        U0 = [_dot(p, w) for p, w in zip(Pb, Wk)]
        Ahb, U0b = bf(Ah), bf(U0)
        Rh = bf([r.astype(F32) + _dot(m, a) for r, m, a in zip(Rt, Mrb, Ahb)])
        Y0 = [_dot(mb, u) + _dot(mk, v) for mb, u, mk, v in zip(Mrb, U0b, Mrk, V)]
        GT = bf([_dot_tn(a, b) for a, b in zip(Ahb, Bb)])
        HT = [_dot_tn(u, b) + _dot_tn(v, k) for u, b, v, k in zip(U0b, Bb, V, Kb)]
        S0 = [s_ref[i] for i in range(len(chains))]
        S0b = bf(S0)
        Y = [_dot_nt(r, s) + y0 for r, s, y0 in zip(Rh, S0b, Y0)]
        for i, (b, g) in enumerate(chains):
            ls = slice(g * GW, (g + 1) * GW)
            s_ref[i] = S0[i] * dl_ref[b, c, :, ls] + _dot(S0b[i], GT[i]) + HT[i]
            y = sum(Y[i][h * C:(h + 1) * C] for h in range(GROUP))
            mean = _head_sum(y) * (1.0 / HEAD)
            yc = y - mean
            var = _head_sum(yc * yc) * (1.0 / HEAD)
            yn = yc * lax.rsqrt(var + GN_EPS) * lnw_ref[:, ls] + lnb_ref[:, ls]
            y_ref[b, rows, ls] = (yn + bonus_ref[b, rows, ls]) * g_ref[b, rows, ls]
        return carry

    lax.fori_loop(0, tt // C, chunk, 0)


def _rw_scan(rt, at, bt, kt, bb, kb, v, dl, bonus, g, ln_w, ln_b):
    B, T, W = rt.shape
    nb = min(SCAN_BATCH, B)
    tt = min(256, T)
    tile = pl.BlockSpec((nb, tt, W), lambda b, j: (b, j, 0))
    row = pl.BlockSpec((1, W), lambda b, j: (0, 0))
    return pl.pallas_call(
        _rw_scan_kernel,
        grid=(B // nb, T // tt),
        in_specs=[tile] * 7 + [pl.BlockSpec((nb, tt // CHUNK, 1, W), lambda b, j: (b, j, 0, 0)), tile, tile, row, row],
        out_specs=tile,
        out_shape=jax.ShapeDtypeStruct((B, T, W), F32),
        scratch_shapes=[pltpu.VMEM((nb * (W // GW), GW, GW), F32)],
        compiler_params=_cparams(("parallel", "arbitrary")),
        name="rw_scan",
    )(rt, at, bt, kt, bb, kb, v, dl, bonus, g, ln_w, ln_b)


def _fox_prep_kernel(n_heads, qkv_ref, f_ref, bf_ref, qnw_ref, knw_ref,
                     q_ref, k_ref, vt_ref, carry_ref):
    W = n_heads * HEAD
    tt = qkv_ref.shape[1]

    @pl.when(pl.program_id(1) == 0)
    def _():
        carry_ref[...] = jnp.zeros_like(carry_ref)

    qkv = qkv_ref[0]
    q, k, v = qkv[:, 0:W], qkv[:, W:2 * W], qkv[:, 2 * W:3 * W]
    inv_n = 1.0 / HEAD
    qn = q * lax.rsqrt(_head_sum(q * q) * inv_n + NORM_EPS) * (qnw_ref[...] * (HEAD ** -0.5 * LOG2E))
    kn = k * lax.rsqrt(_head_sum(k * k) * inv_n + NORM_EPS) * knw_ref[...]
    for p in range(n_heads // 2):
        vt_ref[0, p, 0] = v[:, p * 128:(p + 1) * 128].T.astype(BF16)

    zf = f_ref[0] + bf_ref[...]
    logf = jnp.minimum(zf, 0.0) - jnp.log(1.0 + jnp.exp(-jnp.abs(zf)))
    ti = lax.broadcasted_iota(I32, (tt, tt), 0)
    si = lax.broadcasted_iota(I32, (tt, tt), 1)
    tri = jnp.where(ti >= si, 1.0, 0.0).astype(BF16)
    hi, mid, lo = _split3(logf)
    c = carry_ref[...] + _dot(tri, hi) + _dot(tri, mid) + _dot(tri, lo)
    carry_ref[...] = c[tt - 1:tt, :]
    c_hi, c_mid, c_lo = _split3(c * (-LOG2E))
    lane = lax.broadcasted_iota(I32, (tt, 128), 1)
    c3 = jnp.where(lane < 8, c_hi.astype(F32),
                   jnp.where(lane < 16, pltpu.roll(c_mid.astype(F32), 8, axis=1),
                             pltpu.roll(c_lo.astype(F32), 16, axis=1))).astype(BF16)
    src = lax.broadcasted_iota(I32, (128, 128), 0)
    dst = lax.broadcasted_iota(I32, (128, 128), 1)
    ones = jnp.where((lane >= HEAD) & (lane < HEAD + 3), 1.0, 0.0)
    for h in range(n_heads):
        col = slice((h // 2) * 128, (h // 2 + 1) * 128)
        qh, kh = qn[:, col], kn[:, col]
        if h % 2:
            qh, kh = pltpu.roll(qh, HEAD, axis=1), pltpu.roll(kh, HEAD, axis=1)
        sel = jnp.where((dst >= HEAD) & (dst < HEAD + 3) & (src == h + 8 * (dst - HEAD)), 1.0, 0.0).astype(BF16)
        q_ref[0, h] = jnp.where(lane < HEAD, qh, ones).astype(BF16)
        k_ref[0, h] = jnp.where(lane < HEAD, kh, _dot(c3, sel)).astype(BF16)


def _fox_prep(qkv, f, b_f, qn_w, kn_w, n_heads):
    B, T, _ = qkv.shape
    W = n_heads * HEAD
    tt = min(256, T)
    row = lambda a: pl.BlockSpec(a.shape, lambda b, j: (0, 0))
    aug = jax.ShapeDtypeStruct((B, n_heads, T, 128), BF16)
    aug_spec = pl.BlockSpec((1, n_heads, tt, 128), lambda b, j: (b, 0, j, 0))
    return pl.pallas_call(
        functools.partial(_fox_prep_kernel, n_heads),
        grid=(B, T // tt),
        in_specs=[pl.BlockSpec((1, tt, 3 * W), lambda b, j: (b, j, 0)),
                  pl.BlockSpec((1, tt, 128), lambda b, j: (b, j, 0)),
                  row(b_f), row(qn_w), row(kn_w)],
        out_specs=[aug_spec, aug_spec,
                   pl.BlockSpec((1, n_heads // 2, 1, 128, tt), lambda b, j: (b, 0, j, 0, 0))],
        out_shape=[aug, aug, jax.ShapeDtypeStruct((B, n_heads // 2, T // tt, 128, tt), BF16)],
        scratch_shapes=[pltpu.VMEM((1, 128), F32)],
        compiler_params=_cparams(("parallel", "arbitrary")),
        name="fox_prep",
    )(qkv, f, b_f, qn_w, kn_w)


def _fox_attn_kernel(q_ref, k_ref, vt_ref, o_ref, sa_ref, sb_ref):
    i = pl.program_id(2)
    tq = q_ref.shape[2]
    sub = vt_ref.shape[4]
    q = [q_ref[0, hh] for hh in range(2)]
    ki = lax.broadcasted_iota(I32, (tq, tq), 0)
    qi = lax.broadcasted_iota(I32, (tq, tq), 1)

    def scores(j, s_ref, diag=False):
        k0 = pl.multiple_of(j * tq, tq)
        for hh in range(2):
            s = _dot_nt(k_ref[0, hh, pl.ds(k0, tq), :], q[hh])
            s_ref[hh] = jnp.where(ki <= qi, s, -jnp.inf) if diag else s

    def update(j, s_ref, carry):
        out = []
        for hh in range(2):
            m, l, acc = carry[hh]
            m_new = jnp.maximum(m, jnp.max(s_ref[hh], axis=0, keepdims=True))
            alpha = jnp.exp2(m - m_new)
            p = jnp.exp2(s_ref[hh] - m_new)
            l = alpha * l + jnp.sum(p, axis=0, keepdims=True)
            p = p.astype(BF16)
            pv = sum(_dot(vt_ref[0, 0, j * (tq // sub) + u], p[u * sub:(u + 1) * sub]) for u in range(tq // sub))
            out.append((m_new, l, alpha * acc + pv))
        return tuple(out)

    def two_blocks(jj, carry):
        scores(2 * jj + 1, sb_ref)
        carry = update(2 * jj, sa_ref, carry)
        scores(jnp.minimum(2 * jj + 2, i), sa_ref)
        return update(2 * jj + 1, sb_ref, carry)

    init = tuple((jnp.full((1, tq), NEG_BIG, F32), jnp.zeros((1, tq), F32), jnp.zeros((128, tq), F32))
                 for _ in range(2))
    scores(0, sa_ref)
    carry = lax.fori_loop(0, i // 2, two_blocks, init)
    carry = lax.fori_loop(0, i & 1, lambda _, c: update(i - 1, sa_ref, c), carry)
    scores(i, sb_ref, diag=True)
    (_, l0, acc0), (_, l1, acc1) = update(i, sb_ref, carry)
    row = lax.broadcasted_iota(I32, (128, tq), 0)
    o_ref[0] = jnp.where(row < HEAD, acc0 / l0, acc1 / l1).T


def _fox_attn(q_aug, k_aug, vt):
    B, H, T, _ = q_aug.shape
    tq = min(512, T)
    n_sub, sub = vt.shape[2], vt.shape[4]
    return pl.pallas_call(
        _fox_attn_kernel,
        grid=(B, H // 2, T // tq),
        in_specs=[pl.BlockSpec((1, 2, tq, 128), lambda b, p, i: (b, p, i, 0)),
                  pl.BlockSpec((1, 2, T, 128), lambda b, p, i: (b, p, 0, 0)),
                  pl.BlockSpec((1, 1, n_sub, 128, sub), lambda b, p, i: (b, p, 0, 0, 0))],
        out_specs=pl.BlockSpec((1, tq, 128), lambda b, p, i: (b, i, p)),
        out_shape=jax.ShapeDtypeStruct((B, T, H * HEAD), F32),
        scratch_shapes=[pltpu.VMEM((2, tq, tq), F32), pltpu.VMEM((2, tq, tq), F32)],
        compiler_params=_cparams(("parallel", "parallel", "arbitrary")),
        name="fox_attn",
    )(q_aug, k_aug, vt)


def _out_route_kernel(yrw_ref, yfox_ref, x_ref, wo_rw_ref, wo_fox_ref, nw_ref, wr_hi_ref, wr_lo_ref, br_ref,
                      x1_ref, h2_ref, eid_ref, gate_ref):
    x1 = (x_ref[...] + _dot(yrw_ref[...].astype(BF16), wo_rw_ref[...])
          + _dot(yfox_ref[...].astype(BF16), wo_fox_ref[...]))
    x1_ref[...] = x1
    h2 = x1 * lax.rsqrt(jnp.mean(x1 * x1, axis=-1, keepdims=True) + NORM_EPS) * nw_ref[...]
    _store_token_major(h2_ref, h2)
    h_hi = h2.astype(BF16)
    h_lo = (h2 - h_hi.astype(F32)).astype(BF16)
    logits = (_dot(h_hi, wr_hi_ref[...]) + _dot(h_hi, wr_lo_ref[...]) + _dot(h_lo, wr_hi_ref[...])) + br_ref[...]

    lane_i = lax.broadcasted_iota(I32, logits.shape, 1)
    lane = lane_i.astype(F32)
    first = lambda mask: jnp.min(jnp.where(mask, lane, 1e9), axis=-1, keepdims=True)
    gl = jnp.where(lane_i < N_GROUPS, logits, -jnp.inf)
    gmax = jnp.max(gl, axis=-1, keepdims=True)
    g_sel = first(gl == gmax)
    g_gate = 1.0 / jnp.sum(jnp.exp(gl - gmax), axis=-1, keepdims=True)
    e_lane = lane_i - N_GROUPS
    lane_grp = jnp.right_shift(e_lane, 3).astype(F32)
    in_grp = (e_lane >= 0) & (e_lane < N_EXPERTS) & (lane_grp == g_sel)
    el = jnp.where(in_grp, logits, -jnp.inf)
    m1 = jnp.max(el, axis=-1, keepdims=True)
    i1 = first(el == m1)
    el2 = jnp.where(lane == i1, -jnp.inf, el)
    m2 = jnp.max(el2, axis=-1, keepdims=True)
    i2 = first(el2 == m2)
    e2 = jnp.exp(m2 - m1)
    g1 = g_gate / (1.0 + e2)
    g2 = g_gate * e2 / (1.0 + e2)
    eid_ref[...] = jnp.where(lane_i == 0, i1 - N_GROUPS, jnp.where(lane_i == 1, i2 - N_GROUPS, 0.0)).astype(I32)
    gate_ref[...] = jnp.where(lane_i == 0, g1, jnp.where(lane_i == 1, g2, 0.0))


def _out_route(y_rw, y_fox, x2, wo_rw, wo_fox, norm_w, wr_hi, wr_lo, b_r):
    M, D = x2.shape
    W = y_rw.shape[1]
    tm = min(256, M)
    full = lambda a: pl.BlockSpec(a.shape, lambda i: (0, 0))
    tile = lambda w: pl.BlockSpec((tm, w), lambda i: (i, 0))
    return pl.pallas_call(
        _out_route_kernel,
        grid=(M // tm,),
        in_specs=[tile(W), tile(W), tile(D), full(wo_rw), full(wo_fox), full(norm_w),
                  full(wr_hi), full(wr_lo), full(b_r)],
        out_specs=[tile(D), pl.BlockSpec((tm * (D // 128), 128), lambda i: (i, 0)),
                   tile(ROUTER_LANES), tile(ROUTER_LANES)],
        out_shape=[jax.ShapeDtypeStruct((M, D), F32), jax.ShapeDtypeStruct((M * (D // 128), 128), F32),
                   jax.ShapeDtypeStruct((M, ROUTER_LANES), I32), jax.ShapeDtypeStruct((M, ROUTER_LANES), F32)],
        compiler_params=_cparams(("parallel",)),
        name="out_route",
    )(y_rw, y_fox, x2, wo_rw, wo_fox, norm_w, wr_hi, wr_lo, b_r)


def _rank_kernel(eid_ref, rank_ref, cnt_ref, carry_ref):
    @pl.when(pl.program_id(0) == 0)
    def _():
        carry_ref[...] = jnp.zeros_like(carry_ref)

    eid = eid_ref[...].astype(F32)
    tm = eid.shape[0]
    lane = lax.broadcasted_iota(I32, eid.shape, 1)
    lane_f = lane.astype(F32)
    pick = lambda l: jnp.sum(jnp.where(lane == l, eid, 0.0), axis=-1, keepdims=True)
    e0, e1 = pick(0), pick(1)
    oh0 = (lane_f == e0).astype(F32)
    oh1 = (lane_f == e1).astype(F32)
    both = oh0 + oh1
    ri = lax.broadcasted_iota(I32, (tm, tm), 0)
    ci = lax.broadcasted_iota(I32, (tm, tm), 1)
    before = _dot(jnp.where(ri > ci, 1.0, 0.0).astype(BF16), both.astype(BF16)) + carry_ref[...]
    r0 = jnp.sum(oh0 * before, axis=-1, keepdims=True)
    r1 = jnp.sum(oh1 * (before + oh0), axis=-1, keepdims=True)
    rank_ref[...] = jnp.where(lane == 0, r0, jnp.where(lane == 1, r1, 0.0)).astype(I32)
    total = carry_ref[...] + jnp.sum(both, axis=0, keepdims=True)
    carry_ref[...] = total
    cnt_ref[...] = jnp.broadcast_to(total, cnt_ref.shape).astype(I32)


def _rank(eid):
    M = eid.shape[0]
    tm = min(512, M)
    return pl.pallas_call(
        _rank_kernel,
        grid=(M // tm,),
        in_specs=[pl.BlockSpec((tm, ROUTER_LANES), lambda i: (i, 0))],
        out_specs=[pl.BlockSpec((tm, ROUTER_LANES), lambda i: (i, 0)),
                   pl.BlockSpec((8, ROUTER_LANES), lambda i: (0, 0))],
        out_shape=[jax.ShapeDtypeStruct((M, ROUTER_LANES), I32), jax.ShapeDtypeStruct((8, ROUTER_LANES), I32)],
        scratch_shapes=[pltpu.VMEM((1, ROUTER_LANES), F32)],
        compiler_params=_cparams(("arbitrary",)),
        name="rank",
    )(eid)


def _dest_kernel(eid_ref, rank_ref, start_ref, dest_ref):
    eid = eid_ref[...].astype(F32)
    rank = rank_ref[...].astype(F32)
    lane = lax.broadcasted_iota(I32, eid.shape, 1)
    lane_f = lane.astype(F32)
    pick = lambda x, l: jnp.sum(jnp.where(lane == l, x, 0.0), axis=-1, keepdims=True)
    base = lambda e: jnp.sum(jnp.where(lane_f == e, start_ref[...], 0.0), axis=-1, keepdims=True)
    d0 = base(pick(eid, 0)) + pick(rank, 0)
    d1 = base(pick(eid, 1)) + pick(rank, 1)
    dest_ref[...] = jnp.where(lane == 0, d0, jnp.where(lane == 1, d1, 0.0)).astype(I32)


def _dest(eid, rank, pad_start_row):
    M = eid.shape[0]
    tm = min(1024, M)
    tile = pl.BlockSpec((tm, ROUTER_LANES), lambda i: (i, 0))
    return pl.pallas_call(
        _dest_kernel,
        grid=(M // tm,),
        in_specs=[tile, tile, pl.BlockSpec((1, ROUTER_LANES), lambda i: (0, 0))],
        out_specs=tile,
        out_shape=jax.ShapeDtypeStruct((M, ROUTER_LANES), I32),
        compiler_params=_cparams(("parallel",)),
        name="dest",
    )(eid, rank, pad_start_row)


def _invert_kernel(dest_ref, cnt_ref, start_ref, end_ref, tok_ref):
    def clear(r, carry):
        tok_ref[r] = 0
        return carry

    def clear_expert(e, carry):
        lax.fori_loop(start_ref[e] + cnt_ref[e], end_ref[e], clear, 0)
        return carry

    def put(a, carry):
        tok_ref[dest_ref[a]] = lax.shift_right_logical(a, 1)
        return carry

    lax.fori_loop(0, cnt_ref.shape[0], clear_expert, 0)
    lax.fori_loop(end_ref[end_ref.shape[0] - 1], tok_ref.shape[0], clear, 0)
    lax.fori_loop(0, dest_ref.shape[0], put, 0, unroll=8)


def _invert(dest_flat, counts, pad_start, pad_end, n_rows):
    smem = pl.BlockSpec(memory_space=pltpu.SMEM)
    return pl.pallas_call(
        _invert_kernel,
        in_specs=[smem] * 4,
        out_specs=smem,
        out_shape=jax.ShapeDtypeStruct((n_rows,), I32),
        name="invert",
    )(dest_flat, counts, pad_start, pad_end)


def _experts_kernel(be_ref, nused_ref, tok_ref, h_hbm, wg_ref, wu_ref, wd_ref, yb_ref,
                    xbuf, sem, wg_bf, wu_bf, wd_bf):
    i = pl.program_id(0)
    D, F = wg_bf.shape
    nt = D // 128
    rb = yb_ref.shape[0] // nt
    n_used = nused_ref[0]
    slot = i & 1

    def copy(blk, r, s):
        src = pl.multiple_of(tok_ref[blk * rb + r] * nt, nt)
        return pltpu.make_async_copy(h_hbm.at[pl.ds(src, nt)], xbuf.at[s, pl.ds(r * nt, nt)], sem.at[s])

    def wait_block(blk, s):
        def body(r, carry):
            copy(blk, r, s).wait()
            return carry
        lax.fori_loop(0, rb, body, 0, unroll=8)

    @pl.when(i == 0)
    def _():
        def body(r, carry):
            copy(0, r, 0).start()
            return carry
        lax.fori_loop(0, rb, body, 0, unroll=8)

    prev = be_ref[jnp.maximum(i - 1, 0)]

    @pl.when((i == 0) | (be_ref[i] != prev))
    def _():
        wg_bf[...] = wg_ref[0].astype(BF16)
        wu_bf[...] = wu_ref[0].astype(BF16)
        wd_bf[...] = wd_ref[0].astype(BF16)

    @pl.when(i < n_used)
    def _():
        wait_block(i, slot)

        def start_next(r, carry):
            copy(i + 1, 2 * r, 1 - slot).start(priority=0)
            copy(i + 1, 2 * r + 1, 1 - slot).start(priority=1)
            return carry
        lax.fori_loop(0, rb // 2, start_next, 0, unroll=4)

        xb = jnp.concatenate([_load_token_major(xbuf.at[slot], rb, D, s).astype(BF16) for s in range(nt)], axis=1)
        gate = _dot(xb, wg_bf[...])
        up = _dot(xb, wu_bf[...])
        hid = gate * _sigmoid(gate) * up
        _store_token_major(yb_ref, _dot(hid.astype(BF16), wd_bf[...]))

    @pl.when(i == n_used)
    def _():
        wait_block(i, slot)

    @pl.when(i >= n_used)
    def _():
        yb_ref[...] = jnp.zeros_like(yb_ref)


def _experts(block_expert, n_used, tok, h2, w_gate, w_up, w_down, rb):
    E, D, F = w_gate.shape
    nt = D // 128
    nb = tok.shape[0] // rb
    grid_spec = pltpu.PrefetchScalarGridSpec(
        num_scalar_prefetch=3,
        grid=(nb,),
        in_specs=[pl.BlockSpec(memory_space=pl.ANY),
                  pl.BlockSpec((1, D, F), lambda i, be, nu, tk: (be[i], 0, 0)),
                  pl.BlockSpec((1, D, F), lambda i, be, nu, tk: (be[i], 0, 0)),
                  pl.BlockSpec((1, F, D), lambda i, be, nu, tk: (be[i], 0, 0))],
        out_specs=pl.BlockSpec((rb * nt, 128), lambda i, be, nu, tk: (i, 0)),
        scratch_shapes=[pltpu.VMEM((2, rb * nt, 128), F32), pltpu.SemaphoreType.DMA((2,)),
                        pltpu.VMEM((D, F), BF16), pltpu.VMEM((D, F), BF16), pltpu.VMEM((F, D), BF16)],
    )
    return pl.pallas_call(
        _experts_kernel,
        grid_spec=grid_spec,
        out_shape=jax.ShapeDtypeStruct((nb * rb * nt, 128), F32),
        compiler_params=_cparams(("arbitrary",)),
        name="experts",
    )(block_expert, n_used, tok, h2, w_gate, w_up, w_down)


def _combine_kernel(dest_ref, gate_ref, x1_ref, yb_hbm, out_ref, buf, sem):
    i = pl.program_id(0)
    tmc, D = x1_ref.shape
    nt = D // 128
    slot = i & 1

    def copy(tile, t, k, s):
        src = pl.multiple_of(dest_ref[2 * (tile * tmc + t) + k] * nt, nt)
        return pltpu.make_async_copy(yb_hbm.at[pl.ds(src, nt)], buf.at[s, k, pl.ds(t * nt, nt)], sem.at[s])

    def start_tile(tile, s):
        def body(t, carry):
            copy(tile, t, 0, s).start(priority=0)
            copy(tile, t, 1, s).start(priority=1)
            return carry
        lax.fori_loop(0, tmc, body, 0, unroll=8)

    @pl.when(i == 0)
    def _():
        start_tile(0, 0)

    @pl.when(i + 1 < pl.num_programs(0))
    def _():
        start_tile(i + 1, 1 - slot)

    def wait(t, carry):
        copy(i, t, 0, slot).wait()
        copy(i, t, 1, slot).wait()
        return carry

    lax.fori_loop(0, tmc, wait, 0, unroll=8)
    gate = gate_ref[...]
    g0, g1 = gate[:, 0:1], gate[:, 1:2]
    for s in range(nt):
        cols = slice(s * 128, (s + 1) * 128)
        out_ref[:, cols] = (x1_ref[:, cols] + g0 * _load_token_major(buf.at[slot, 0], tmc, D, s)
                            + g1 * _load_token_major(buf.at[slot, 1], tmc, D, s))


def _combine(dest_flat, gates, x1, yb):
    M, D = x1.shape
    tmc = min(256, M)
    grid_spec = pltpu.PrefetchScalarGridSpec(
        num_scalar_prefetch=1,
        grid=(M // tmc,),
        in_specs=[pl.BlockSpec((tmc, ROUTER_LANES), lambda i, d: (i, 0)),
                  pl.BlockSpec((tmc, D), lambda i, d: (i, 0)),
                  pl.BlockSpec(memory_space=pl.ANY)],
        out_specs=pl.BlockSpec((tmc, D), lambda i, d: (i, 0)),
        scratch_shapes=[pltpu.VMEM((2, 2, tmc * (D // 128), 128), F32), pltpu.SemaphoreType.DMA((2,))],
    )
    return pl.pallas_call(
        _combine_kernel,
        grid_spec=grid_spec,
        out_shape=jax.ShapeDtypeStruct((M, D), F32),
        compiler_params=_cparams(("arbitrary",)),
        name="combine",
    )(dest_flat, gates, x1, yb)


def _mixer(x2, B, T, norm_w, w_in, mu, w0, w2, a0, a2, g2, k_k, k_a, r_k, ln_w, ln_b, b_f, qn_w, kn_w):
    rw_heads = w0.shape[0] // HEAD
    fox_heads = b_f.shape[0]
    Wr, Wf = rw_heads * HEAD, fox_heads * HEAD
    lora = w2.shape[0] + a2.shape[0] + g2.shape[0]
    rw_cols = 3 * Wr + lora
    o_w, o_k, o_v, o_a = Wr, Wr + w2.shape[0], 2 * Wr + w2.shape[0], 3 * Wr + w2.shape[0]
    perm = np.concatenate([np.arange(0, Wr), np.arange(o_k, o_k + Wr), np.arange(o_v, o_v + Wr),
                           np.arange(o_w, o_w + w2.shape[0]), np.arange(o_a, rw_cols)])
    w_rw = w_in[:, :rw_cols][:, perm].astype(BF16)
    w_qkv = w_in[:, rw_cols:rw_cols + 3 * Wf].astype(BF16)
    w_f = jnp.pad(w_in[:, rw_cols + 3 * Wf:], ((0, 0), (0, 128 - fox_heads))).astype(BF16)
    p_rw, p_qkv, p_f = _in_proj(x2, norm_w[None, :], w_rw, w_qkv, w_f)

    row = lambda a: a.reshape(1, -1)
    ops = _rw_prep(p_rw.reshape(B, T, rw_cols), row(mu[perm]), row(w0), w2.astype(BF16), row(a0),
                   a2.astype(BF16), g2.astype(BF16), row(k_k), row(k_a), row(r_k), rw_heads)
    y_rw = _rw_scan(*ops, row(ln_w), row(ln_b))

    tile_w = lambda w: row(jnp.tile(w, fox_heads))
    q_aug, k_aug, vt = _fox_prep(p_qkv.reshape(B, T, 3 * Wf), p_f.reshape(B, T, 128),
                                 row(jnp.pad(b_f, (0, 128 - fox_heads))), tile_w(qn_w), tile_w(kn_w), fox_heads)
    y_fox = _fox_attn(q_aug, k_aug, vt)
    return y_rw.reshape(B * T, Wr), y_fox.reshape(B * T, Wf)


def _moe(y_rw, y_fox, x2, w_out, norm_w, rg_w, rg_b, re_w, re_b, w_gate, w_up, w_down):
    M, D = x2.shape
    Wr = y_rw.shape[1]
    pad = ROUTER_LANES - N_GROUPS - N_EXPERTS
    w_r = jnp.pad(jnp.concatenate([rg_w, re_w], axis=1), ((0, 0), (0, pad)))
    b_r = jnp.pad(jnp.concatenate([rg_b, re_b]), (0, pad))[None, :]
    wr_hi = w_r.astype(BF16)
    wr_lo = (w_r - wr_hi.astype(F32)).astype(BF16)
    x1, h2, eid, gates = _out_route(y_rw, y_fox, x2, w_out[:Wr].astype(BF16), w_out[Wr:].astype(BF16),
                                    norm_w[None, :], wr_hi, wr_lo, b_r)
    rank, counts = _rank(eid)

    rb = ROW_BLOCK
    counts = counts[0, :N_EXPERTS]
    padded = (counts + rb - 1) // rb * rb
    pad_end = jnp.cumsum(padded)
    pad_start = pad_end - padded
    n_blocks = (2 * M + N_EXPERTS * (rb - 1) + rb - 1) // rb + 1
    block_start = jnp.arange(n_blocks, dtype=I32) * rb
    block_expert = jnp.minimum(jnp.sum(pad_end[None, :] <= block_start[:, None], axis=1), N_EXPERTS - 1).astype(I32)
    n_used = (pad_end[-1:] // rb).astype(I32)
    start_row = jnp.pad(pad_start.astype(F32), (0, ROUTER_LANES - N_EXPERTS))[None, :]
    dest = _dest(eid, rank, start_row)[:, :2].reshape(-1)

    tok = _invert(dest, counts, pad_start.astype(I32), pad_end.astype(I32), n_blocks * rb)
    yb = _experts(block_expert, n_used, tok, h2, w_gate, w_up, w_down, rb)
    return _combine(dest, gates, x1, yb)


def kernel(x, norm_mix_w, w_in, mu_shift, rw_w0, rw_w2, rw_a0, rw_a2, rw_g2, rw_k_k, rw_k_a, rw_r_k, rw_ln_w, rw_ln_b, fox_b_f, fox_q_norm_w, fox_k_norm_w, w_out, norm_ffn_w, router_group_w, router_group_b, router_expert_w, router_expert_b, exp_w_gate, exp_w_up, exp_w_down):
    B, T, D = x.shape
    x2 = x.reshape(B * T, D)
    for l in range(w_in.shape[0]):
        y_rw, y_fox = _mixer(x2, B, T, norm_mix_w[l], w_in[l], mu_shift[l], rw_w0[l], rw_w2[l], rw_a0[l],
                             rw_a2[l], rw_g2[l], rw_k_k[l], rw_k_a[l], rw_r_k[l].reshape(-1), rw_ln_w[l],
                             rw_ln_b[l], fox_b_f[l], fox_q_norm_w[l], fox_k_norm_w[l])
        x2 = _moe(y_rw, y_fox, x2, w_out[l], norm_ffn_w[l], router_group_w[l], router_group_b[l],
                  router_expert_w[l], router_expert_b[l], exp_w_gate[l], exp_w_up[l], exp_w_down[l])
    return x2.reshape(B, T, D)
```

```python
import functools

import jax
import jax.numpy as jnp
import numpy as np
from jax import lax
from jax.experimental import pallas as pl
from jax.experimental.pallas import tpu as pltpu

F32, BF16, I32 = jnp.float32, jnp.bfloat16, jnp.int32

HEAD = 64
CHUNK = 64
GROUP = 2
GW = GROUP * HEAD
SCAN_BATCH = 2
N_GROUPS = 8
EXPERTS_PER_GROUP = 8
N_EXPERTS = N_GROUPS * EXPERTS_PER_GROUP
ROUTER_LANES = 128
ROW_BLOCK = 256
NORM_EPS = 1e-6
GN_EPS = 64e-5
NEG_BIG = -1e30
LOG2E = 1.4426950408889634
VMEM_LIMIT = 56 * 1024 * 1024


def _cparams(sem):
    return pltpu.CompilerParams(dimension_semantics=sem, vmem_limit_bytes=VMEM_LIMIT)


def _dot(a, b):
    return jnp.dot(a, b, preferred_element_type=F32)


def _dot_nt(a, b):
    return lax.dot_general(a, b, (((1,), (1,)), ((), ())), preferred_element_type=F32)


def _dot_tn(a, b):
    return lax.dot_general(a, b, (((0,), (0,)), ((), ())), preferred_element_type=F32)


def _split3(x):
    hi = x.astype(BF16)
    r1 = x - hi.astype(F32)
    mid = r1.astype(BF16)
    lo = (r1 - mid.astype(F32)).astype(BF16)
    return hi, mid, lo


def _sigmoid(z):
    return 1.0 / (1.0 + jnp.exp(-z))


def _head_sum(x):
    cols = []
    for c in range(x.shape[1] // 128):
        xc = x[:, c * 128:(c + 1) * 128]
        first = lax.broadcasted_iota(I32, xc.shape, 1) < HEAD
        s0 = jnp.sum(jnp.where(first, xc, 0.0), axis=-1, keepdims=True)
        s1 = jnp.sum(jnp.where(first, 0.0, xc), axis=-1, keepdims=True)
        cols.append(jnp.where(first, s0, s1))
    return cols[0] if len(cols) == 1 else jnp.concatenate(cols, axis=1)


def _store_token_major(ref, x):
    rows, d = x.shape
    for s in range(d // 128):
        ref[pl.ds(s, rows, stride=d // 128), :] = x[:, s * 128:(s + 1) * 128]


def _load_token_major(ref, rows, d, s):
    return ref[pl.ds(s, rows, stride=d // 128), :]


def _in_proj_kernel(x_ref, nw_ref, wrw_ref, wqkv_ref, wf_ref, prw_ref, pqkv_ref, pf_ref):
    x = x_ref[...]
    h = x * lax.rsqrt(jnp.mean(x * x, axis=-1, keepdims=True) + NORM_EPS) * nw_ref[...]
    hb = h.astype(BF16)
    prw_ref[...] = _dot(hb, wrw_ref[...])
    pqkv_ref[...] = _dot(hb, wqkv_ref[...])
    pf_ref[...] = _dot(hb, wf_ref[...])


def _in_proj(x2, norm_w, w_rw, w_qkv, w_f):
    M, D = x2.shape
    tm = min(512, M)
    n_rw, n_qkv, n_f = w_rw.shape[1], w_qkv.shape[1], w_f.shape[1]
    full = lambda shape: pl.BlockSpec(shape, lambda i: (0, 0))
    return pl.pallas_call(
        _in_proj_kernel,
        grid=(M // tm,),
        in_specs=[pl.BlockSpec((tm, D), lambda i: (i, 0)), full((1, D)),
                  full((D, n_rw)), full((D, n_qkv)), full((D, n_f))],
        out_specs=[pl.BlockSpec((tm, n_rw), lambda i: (i, 0)),
                   pl.BlockSpec((tm, n_qkv), lambda i: (i, 0)),
                   pl.BlockSpec((tm, n_f), lambda i: (i, 0))],
        out_shape=[jax.ShapeDtypeStruct((M, n_rw), F32),
                   jax.ShapeDtypeStruct((M, n_qkv), F32),
                   jax.ShapeDtypeStruct((M, n_f), F32)],
        compiler_params=_cparams(("parallel",)),
        name="in_proj",
    )(x2, norm_w, w_rw, w_qkv, w_f)


def _rw_prep_kernel(n_heads, p_ref, pprev_ref, mu_ref, w0_ref, w2_ref, a0_ref, a2_ref, g2_ref,
                    kk_ref, ka_ref, rk_ref,
                    rt_ref, at_ref, bt_ref, kt_ref, bb_ref, kb_ref, v_ref, dl_ref, bonus_ref, g_ref):
    W = n_heads * HEAD
    j = pl.program_id(1)
    p = p_ref[0]
    tt = p.shape[0]
    last_prev = jnp.where(j > 0, pprev_ref[0, 7:8, :], 0.0)
    row = lax.broadcasted_iota(I32, p.shape, 0)
    prev = jnp.where(row == 0, last_prev, pltpu.roll(p, 1, axis=0))
    ps = p + (prev - p) * mu_ref[...]
    r, k, v = ps[:, 0:W], ps[:, W:2 * W], ps[:, 2 * W:3 * W]
    o = 3 * W
    pw, pa, pg = ps[:, o:o + 64], ps[:, o + 64:o + 128], ps[:, o + 128:o + 256]

    z = w0_ref[...] + _dot(jnp.tanh(pw).astype(BF16), w2_ref[...])
    lw = (-np.exp(-0.5)).astype(np.float32) * _sigmoid(z)
    a_sig = _sigmoid(a0_ref[...] + _dot(pa.astype(BF16), a2_ref[...]))
    g_ref[0] = _dot(_sigmoid(pg).astype(BF16), g2_ref[...])

    kk = k * kk_ref[...]
    kk = kk / jnp.maximum(jnp.sqrt(_head_sum(kk * kk)), 1e-12)
    km = k * (1.0 + (a_sig - 1.0) * ka_ref[...])
    a_vec = -kk
    b_vec = kk * a_sig
    bonus_ref[0] = _head_sum(r * km * rk_ref[...]) * v
    v_ref[0] = v.astype(BF16)

    ti = lax.broadcasted_iota(I32, (tt, tt), 0)
    si = lax.broadcasted_iota(I32, (tt, tt), 1)
    same = (ti // CHUNK) == (si // CHUNK)
    tri = jnp.where(same & (ti >= si), 1.0, 0.0).astype(BF16)
    ones = jnp.where(same, 1.0, 0.0).astype(BF16)
    nck = tt // CHUNK
    ci = lax.broadcasted_iota(I32, (nck, tt), 0)
    cs = lax.broadcasted_iota(I32, (nck, tt), 1)
    sel = jnp.where(ci == cs // CHUNK, 1.0, 0.0).astype(BF16)
    hi, mid, lo = _split3(lw)
    cum = _dot(tri, hi) + _dot(tri, mid) + _dot(tri, lo)
    tot = _dot(ones, hi) + _dot(ones, mid) + _dot(ones, lo)
    dl = jnp.exp(_dot(sel, hi) + _dot(sel, mid) + _dot(sel, lo))
    for ck in range(nck):
        dl_ref[0, ck] = dl[ck:ck + 1, :]

    e_in = jnp.exp(cum)
    e_ex = jnp.exp(cum - lw)
    e_inv = jnp.exp(-cum)
    e_bar = jnp.exp(tot - cum)
    rt_ref[0] = (r * e_in).astype(BF16)
    at_ref[0] = (a_vec * e_ex).astype(BF16)
    bt_ref[0] = (b_vec * e_inv).astype(BF16)
    kt_ref[0] = (km * e_inv).astype(BF16)
    bb_ref[0] = (b_vec * e_bar).astype(BF16)
    kb_ref[0] = (km * e_bar).astype(BF16)


def _rw_prep(p_rw, mu, w0, w2, a0, a2, g2, k_k, k_a, r_k, n_heads):
    B, T, P = p_rw.shape
    W = n_heads * HEAD
    tt = min(512, T)
    row = lambda a: pl.BlockSpec(a.shape, lambda b, j: (0, 0))
    tile = lambda w: pl.BlockSpec((1, tt, w), lambda b, j: (b, j, 0))
    bf = jax.ShapeDtypeStruct((B, T, W), BF16)
    f32 = jax.ShapeDtypeStruct((B, T, W), F32)
    return pl.pallas_call(
        functools.partial(_rw_prep_kernel, n_heads),
        grid=(B, T // tt),
        in_specs=[tile(P),
                  pl.BlockSpec((1, 8, P), lambda b, j: (b, jnp.maximum(j * (tt // 8) - 1, 0), 0)),
                  row(mu), row(w0), row(w2), row(a0), row(a2), row(g2), row(k_k), row(k_a), row(r_k)],
        out_specs=[tile(W)] * 7 + [pl.BlockSpec((1, tt // CHUNK, 1, W), lambda b, j: (b, j, 0, 0)), tile(W), tile(W)],
        out_shape=[bf] * 7 + [jax.ShapeDtypeStruct((B, T // CHUNK, 1, W), F32), f32, f32],
        compiler_params=_cparams(("parallel", "parallel")),
        name="rw_prep",
    )(p_rw, p_rw, mu, w0, w2, a0, a2, g2, k_k, k_a, r_k)


def _rw_scan_kernel(rt_ref, at_ref, bt_ref, kt_ref, bb_ref, kb_ref, v_ref, dl_ref,
                    bonus_ref, g_ref, lnw_ref, lnb_ref, y_ref, s_ref):
    C = CHUNK
    nb, tt, W = rt_ref.shape
    chains = [(b, g) for b in range(nb) for g in range(W // GW)]

    @pl.when(pl.program_id(1) == 0)
    def _():
        s_ref[...] = jnp.zeros_like(s_ref)

    ri = lax.broadcasted_iota(I32, (GW, GW), 0)
    ci = lax.broadcasted_iota(I32, (GW, GW), 1)
    tr, tc = ri & (C - 1), ci & (C - 1)
    strict, incl, eye = tr > tc, tr >= tc, ri == ci
    lane_head = lax.broadcasted_iota(I32, (C, GW), 1) // HEAD

    def bd(x):
        return jnp.concatenate([jnp.where(lane_head == h, x, jnp.zeros_like(x)) for h in range(GROUP)], axis=0)

    bf = lambda xs: [x.astype(BF16) for x in xs]

    def chunk(c, carry):
        rows = pl.ds(pl.multiple_of(c * C, C), C)
        ld = lambda ref: [bd(ref[b, rows, g * GW:(g + 1) * GW]) for b, g in chains]
        Rt, At, Bt, Kt, Bb, Kb, V = (ld(r) for r in (rt_ref, at_ref, bt_ref, kt_ref, bb_ref, kb_ref, v_ref))
        BK = [jnp.concatenate([b, k], axis=0) for b, k in zip(Bt, Kt)]
        XA = [_dot_nt(a, bk) for a, bk in zip(At, BK)]
        XR = [_dot_nt(r, bk) for r, bk in zip(Rt, BK)]
        Lab = [jnp.where(strict, x[:, :GW], 0.0) for x in XA]
        Lak = bf([jnp.where(strict, x[:, GW:], 0.0) for x in XA])
        Mrb = bf([jnp.where(incl, x[:, :GW], 0.0) for x in XR])
        Mrk = bf([jnp.where(incl, x[:, GW:], 0.0) for x in XR])
        P = [jnp.where(eye, 1.0, l) for l in Lab]
        Lp = Lab
        for _ in range(5):
            Lpb = bf(Lp)
            Lp = [_dot(x, x) for x in Lpb]
            P = [p + _dot(pb, lb) for p, pb, lb in zip(P, bf(P), bf(Lp))]
        Pb = bf(P)
        Wk = bf([_dot(l, v) for l, v in zip(Lak, V)])
        AU = bf([_dot(p, jnp.concatenate([a, w], axis=1)) for p, a, w in zip(Pb, At, Wk)])
        Ahb, U0b = [x[:, :GW] for x in AU], [x[:, GW:] for x in AU]
        MAU = [_dot(m, x) for m, x in zip(Mrb, AU)]
        Rh = bf([r.astype(F32) + x[:, :GW] for r, x in zip(Rt, MAU)])
        Y0 = [x[:, GW:] + _dot(mk, v) for x, mk, v in zip(MAU, Mrk, V)]
        GT = bf([_dot_tn(a, b) for a, b in zip(Ahb, Bb)])
        HT = [_dot_tn(u, b) + _dot_tn(v, k) for u, b, v, k in zip(U0b, Bb, V, Kb)]
        S0 = [s_ref[i] for i in range(len(chains))]
        S0b = bf(S0)
        Y = [_dot_nt(r, s) + y0 for r, s, y0 in zip(Rh, S0b, Y0)]
        for i, (b, g) in enumerate(chains):
            ls = slice(g * GW, (g + 1) * GW)
            s_ref[i] = S0[i] * dl_ref[b, c, :, ls] + _dot(S0b[i], GT[i]) + HT[i]
            y = sum(Y[i][h * C:(h + 1) * C] for h in range(GROUP))
            mean = _head_sum(y) * (1.0 / HEAD)
            yc = y - mean
            var = _head_sum(yc * yc) * (1.0 / HEAD)
            yn = yc * lax.rsqrt(var + GN_EPS) * lnw_ref[:, ls] + lnb_ref[:, ls]
            y_ref[b, rows, ls] = (yn + bonus_ref[b, rows, ls]) * g_ref[b, rows, ls]
        return carry

    lax.fori_loop(0, tt // C, chunk, 0)


def _rw_scan(rt, at, bt, kt, bb, kb, v, dl, bonus, g, ln_w, ln_b):
    B, T, W = rt.shape
    nb = min(SCAN_BATCH, B)
    tt = min(256, T)
    tile = pl.BlockSpec((nb, tt, W), lambda b, j: (b, j, 0))
    row = pl.BlockSpec((1, W), lambda b, j: (0, 0))
    return pl.pallas_call(
        _rw_scan_kernel,
        grid=(B // nb, T // tt),
        in_specs=[tile] * 7 + [pl.BlockSpec((nb, tt // CHUNK, 1, W), lambda b, j: (b, j, 0, 0)), tile, tile, row, row],
        out_specs=tile,
        out_shape=jax.ShapeDtypeStruct((B, T, W), F32),
        scratch_shapes=[pltpu.VMEM((nb * (W // GW), GW, GW), F32)],
        compiler_params=_cparams(("parallel", "arbitrary")),
        name="rw_scan",
    )(rt, at, bt, kt, bb, kb, v, dl, bonus, g, ln_w, ln_b)


def _fox_prep_kernel(n_heads, qkv_ref, f_ref, bf_ref, qnw_ref, knw_ref,
                     q_ref, k_ref, vt_ref, carry_ref):
    W = n_heads * HEAD
    tt = qkv_ref.shape[1]

    @pl.when(pl.program_id(1) == 0)
    def _():
        carry_ref[...] = jnp.zeros_like(carry_ref)

    qkv = qkv_ref[0]
    q, k, v = qkv[:, 0:W], qkv[:, W:2 * W], qkv[:, 2 * W:3 * W]
    inv_n = 1.0 / HEAD
    qn = q * lax.rsqrt(_head_sum(q * q) * inv_n + NORM_EPS) * (qnw_ref[...] * (HEAD ** -0.5 * LOG2E))
    kn = k * lax.rsqrt(_head_sum(k * k) * inv_n + NORM_EPS) * knw_ref[...]
    for p in range(n_heads // 2):
        vt_ref[0, p, 0] = v[:, p * 128:(p + 1) * 128].T.astype(BF16)

    zf = f_ref[0] + bf_ref[...]
    logf = jnp.minimum(zf, 0.0) - jnp.log(1.0 + jnp.exp(-jnp.abs(zf)))
    ti = lax.broadcasted_iota(I32, (tt, tt), 0)
    si = lax.broadcasted_iota(I32, (tt, tt), 1)
    tri = jnp.where(ti >= si, 1.0, 0.0).astype(BF16)
    hi, mid, lo = _split3(logf)
    c = carry_ref[...] + _dot(tri, hi) + _dot(tri, mid) + _dot(tri, lo)
    carry_ref[...] = c[tt - 1:tt, :]
    c_hi, c_mid, c_lo = _split3(c * (-LOG2E))
    lane = lax.broadcasted_iota(I32, (tt, 128), 1)
    c3 = jnp.where(lane < 8, c_hi.astype(F32),
                   jnp.where(lane < 16, pltpu.roll(c_mid.astype(F32), 8, axis=1),
                             pltpu.roll(c_lo.astype(F32), 16, axis=1))).astype(BF16)
    src = lax.broadcasted_iota(I32, (128, 128), 0)
    dst = lax.broadcasted_iota(I32, (128, 128), 1)
    ones = jnp.where((lane >= HEAD) & (lane < HEAD + 3), 1.0, 0.0)
    for h in range(n_heads):
        col = slice((h // 2) * 128, (h // 2 + 1) * 128)
        qh, kh = qn[:, col], kn[:, col]
        if h % 2:
            qh, kh = pltpu.roll(qh, HEAD, axis=1), pltpu.roll(kh, HEAD, axis=1)
        sel = jnp.where((dst >= HEAD) & (dst < HEAD + 3) & (src == h + 8 * (dst - HEAD)), 1.0, 0.0).astype(BF16)
        q_ref[0, h] = jnp.where(lane < HEAD, qh, ones).astype(BF16)
        k_ref[0, h] = jnp.where(lane < HEAD, kh, _dot(c3, sel)).astype(BF16)


def _fox_prep(qkv, f, b_f, qn_w, kn_w, n_heads):
    B, T, _ = qkv.shape
    W = n_heads * HEAD
    tt = min(256, T)
    row = lambda a: pl.BlockSpec(a.shape, lambda b, j: (0, 0))
    aug = jax.ShapeDtypeStruct((B, n_heads, T, 128), BF16)
    aug_spec = pl.BlockSpec((1, n_heads, tt, 128), lambda b, j: (b, 0, j, 0))
    return pl.pallas_call(
        functools.partial(_fox_prep_kernel, n_heads),
        grid=(B, T // tt),
        in_specs=[pl.BlockSpec((1, tt, 3 * W), lambda b, j: (b, j, 0)),
                  pl.BlockSpec((1, tt, 128), lambda b, j: (b, j, 0)),
                  row(b_f), row(qn_w), row(kn_w)],
        out_specs=[aug_spec, aug_spec,
                   pl.BlockSpec((1, n_heads // 2, 1, 128, tt), lambda b, j: (b, 0, j, 0, 0))],
        out_shape=[aug, aug, jax.ShapeDtypeStruct((B, n_heads // 2, T // tt, 128, tt), BF16)],
        scratch_shapes=[pltpu.VMEM((1, 128), F32)],
        compiler_params=_cparams(("parallel", "arbitrary")),
        name="fox_prep",
    )(qkv, f, b_f, qn_w, kn_w)


def _fox_attn_kernel(q_ref, k_ref, vt_ref, o_ref, sa_ref, sb_ref):
    i = pl.program_id(2)
    tq = q_ref.shape[2]
    sub = vt_ref.shape[4]
    q = [q_ref[0, hh] for hh in range(2)]
    ki = lax.broadcasted_iota(I32, (tq, tq), 0)
    qi = lax.broadcasted_iota(I32, (tq, tq), 1)

    def scores(j, s_ref, diag=False):
        k0 = pl.multiple_of(j * tq, tq)
        for hh in range(2):
            s = _dot_nt(k_ref[0, hh, pl.ds(k0, tq), :], q[hh])
            s_ref[hh] = jnp.where(ki <= qi, s, -jnp.inf) if diag else s

    def update(j, s_ref, carry):
        out = []
        for hh in range(2):
            m, l, acc = carry[hh]
            m_new = jnp.maximum(m, jnp.max(s_ref[hh], axis=0, keepdims=True))
            alpha = jnp.exp2(m - m_new)
            p = jnp.exp2(s_ref[hh] - m_new)
            l = alpha * l + jnp.sum(p, axis=0, keepdims=True)
            p = p.astype(BF16)
            pv = sum(_dot(vt_ref[0, 0, j * (tq // sub) + u], p[u * sub:(u + 1) * sub]) for u in range(tq // sub))
            out.append((m_new, l, alpha * acc + pv))
        return tuple(out)

    def two_blocks(jj, carry):
        scores(2 * jj + 1, sb_ref)
        carry = update(2 * jj, sa_ref, carry)
        scores(jnp.minimum(2 * jj + 2, i), sa_ref)
        return update(2 * jj + 1, sb_ref, carry)

    init = tuple((jnp.full((1, tq), NEG_BIG, F32), jnp.zeros((1, tq), F32), jnp.zeros((128, tq), F32))
                 for _ in range(2))
    scores(0, sa_ref)
    carry = lax.fori_loop(0, i // 2, two_blocks, init)
    carry = lax.fori_loop(0, i & 1, lambda _, c: update(i - 1, sa_ref, c), carry)
    scores(i, sb_ref, diag=True)
    (_, l0, acc0), (_, l1, acc1) = update(i, sb_ref, carry)
    row = lax.broadcasted_iota(I32, (128, tq), 0)
    o_ref[0] = jnp.where(row < HEAD, acc0 / l0, acc1 / l1).T


def _fox_attn(q_aug, k_aug, vt):
    B, H, T, _ = q_aug.shape
    tq = min(512, T)
    n_sub, sub = vt.shape[2], vt.shape[4]
    return pl.pallas_call(
        _fox_attn_kernel,
        grid=(B, H // 2, T // tq),
        in_specs=[pl.BlockSpec((1, 2, tq, 128), lambda b, p, i: (b, p, i, 0)),
                  pl.BlockSpec((1, 2, T, 128), lambda b, p, i: (b, p, 0, 0)),
                  pl.BlockSpec((1, 1, n_sub, 128, sub), lambda b, p, i: (b, p, 0, 0, 0))],
        out_specs=pl.BlockSpec((1, tq, 128), lambda b, p, i: (b, i, p)),
        out_shape=jax.ShapeDtypeStruct((B, T, H * HEAD), F32),
        scratch_shapes=[pltpu.VMEM((2, tq, tq), F32), pltpu.VMEM((2, tq, tq), F32)],
        compiler_params=_cparams(("parallel", "parallel", "arbitrary")),
        name="fox_attn",
    )(q_aug, k_aug, vt)


def _out_route_kernel(yrw_ref, yfox_ref, x_ref, wo_rw_ref, wo_fox_ref, nw_ref, wr_hi_ref, wr_lo_ref, br_ref,
                      x1_ref, h2_ref, eid_ref, gate_ref):
    x1 = (x_ref[...] + _dot(yrw_ref[...].astype(BF16), wo_rw_ref[...])
          + _dot(yfox_ref[...].astype(BF16), wo_fox_ref[...]))
    x1_ref[...] = x1
    h2 = x1 * lax.rsqrt(jnp.mean(x1 * x1, axis=-1, keepdims=True) + NORM_EPS) * nw_ref[...]
    _store_token_major(h2_ref, h2)
    h_hi = h2.astype(BF16)
    h_lo = (h2 - h_hi.astype(F32)).astype(BF16)
    logits = (_dot(h_hi, wr_hi_ref[...]) + _dot(h_hi, wr_lo_ref[...]) + _dot(h_lo, wr_hi_ref[...])) + br_ref[...]

    lane_i = lax.broadcasted_iota(I32, logits.shape, 1)
    lane = lane_i.astype(F32)
    first = lambda mask: jnp.min(jnp.where(mask, lane, 1e9), axis=-1, keepdims=True)
    gl = jnp.where(lane_i < N_GROUPS, logits, -jnp.inf)
    gmax = jnp.max(gl, axis=-1, keepdims=True)
    g_sel = first(gl == gmax)
    g_gate = 1.0 / jnp.sum(jnp.exp(gl - gmax), axis=-1, keepdims=True)
    e_lane = lane_i - N_GROUPS
    lane_grp = jnp.right_shift(e_lane, 3).astype(F32)
    in_grp = (e_lane >= 0) & (e_lane < N_EXPERTS) & (lane_grp == g_sel)
    el = jnp.where(in_grp, logits, -jnp.inf)
    m1 = jnp.max(el, axis=-1, keepdims=True)
    i1 = first(el == m1)
    el2 = jnp.where(lane == i1, -jnp.inf, el)
    m2 = jnp.max(el2, axis=-1, keepdims=True)
    i2 = first(el2 == m2)
    e2 = jnp.exp(m2 - m1)
    g1 = g_gate / (1.0 + e2)
    g2 = g_gate * e2 / (1.0 + e2)
    eid_ref[...] = jnp.where(lane_i == 0, i1 - N_GROUPS, jnp.where(lane_i == 1, i2 - N_GROUPS, 0.0)).astype(I32)
    gate_ref[...] = jnp.where(lane_i == 0, g1, jnp.where(lane_i == 1, g2, 0.0))


def _out_route(y_rw, y_fox, x2, wo_rw, wo_fox, norm_w, wr_hi, wr_lo, b_r):
    M, D = x2.shape
    W = y_rw.shape[1]
    tm = min(256, M)
    full = lambda a: pl.BlockSpec(a.shape, lambda i: (0, 0))
    tile = lambda w: pl.BlockSpec((tm, w), lambda i: (i, 0))
    return pl.pallas_call(
        _out_route_kernel,
        grid=(M // tm,),
        in_specs=[tile(W), tile(W), tile(D), full(wo_rw), full(wo_fox), full(norm_w),
                  full(wr_hi), full(wr_lo), full(b_r)],
        out_specs=[tile(D), pl.BlockSpec((tm * (D // 128), 128), lambda i: (i, 0)),
                   tile(ROUTER_LANES), tile(ROUTER_LANES)],
        out_shape=[jax.ShapeDtypeStruct((M, D), F32), jax.ShapeDtypeStruct((M * (D // 128), 128), F32),
                   jax.ShapeDtypeStruct((M, ROUTER_LANES), I32), jax.ShapeDtypeStruct((M, ROUTER_LANES), F32)],
        compiler_params=_cparams(("parallel",)),
        name="out_route",
    )(y_rw, y_fox, x2, wo_rw, wo_fox, norm_w, wr_hi, wr_lo, b_r)


def _rank_kernel(eid_ref, rank_ref, cnt_ref, carry_ref):
    @pl.when(pl.program_id(0) == 0)
    def _():
        carry_ref[...] = jnp.zeros_like(carry_ref)

    eid = eid_ref[...].astype(F32)
    tm = eid.shape[0]
    lane = lax.broadcasted_iota(I32, eid.shape, 1)
    lane_f = lane.astype(F32)
    pick = lambda l: jnp.sum(jnp.where(lane == l, eid, 0.0), axis=-1, keepdims=True)
    e0, e1 = pick(0), pick(1)
    oh0 = (lane_f == e0).astype(F32)
    oh1 = (lane_f == e1).astype(F32)
    both = oh0 + oh1
    ri = lax.broadcasted_iota(I32, (tm, tm), 0)
    ci = lax.broadcasted_iota(I32, (tm, tm), 1)
    before = _dot(jnp.where(ri > ci, 1.0, 0.0).astype(BF16), both.astype(BF16)) + carry_ref[...]
    r0 = jnp.sum(oh0 * before, axis=-1, keepdims=True)
    r1 = jnp.sum(oh1 * (before + oh0), axis=-1, keepdims=True)
    rank_ref[...] = jnp.where(lane == 0, r0, jnp.where(lane == 1, r1, 0.0)).astype(I32)
    total = carry_ref[...] + jnp.sum(both, axis=0, keepdims=True)
    carry_ref[...] = total
    cnt_ref[...] = jnp.broadcast_to(total, cnt_ref.shape).astype(I32)


def _rank(eid):
    M = eid.shape[0]
    tm = min(512, M)
    return pl.pallas_call(
        _rank_kernel,
        grid=(M // tm,),
        in_specs=[pl.BlockSpec((tm, ROUTER_LANES), lambda i: (i, 0))],
        out_specs=[pl.BlockSpec((tm, ROUTER_LANES), lambda i: (i, 0)),
                   pl.BlockSpec((8, ROUTER_LANES), lambda i: (0, 0))],
        out_shape=[jax.ShapeDtypeStruct((M, ROUTER_LANES), I32), jax.ShapeDtypeStruct((8, ROUTER_LANES), I32)],
        scratch_shapes=[pltpu.VMEM((1, ROUTER_LANES), F32)],
        compiler_params=_cparams(("arbitrary",)),
        name="rank",
    )(eid)


def _dest_kernel(eid_ref, rank_ref, start_ref, dest_ref):
    eid = eid_ref[...].astype(F32)
    rank = rank_ref[...].astype(F32)
    lane = lax.broadcasted_iota(I32, eid.shape, 1)
    lane_f = lane.astype(F32)
    pick = lambda x, l: jnp.sum(jnp.where(lane == l, x, 0.0), axis=-1, keepdims=True)
    base = lambda e: jnp.sum(jnp.where(lane_f == e, start_ref[...], 0.0), axis=-1, keepdims=True)
    d0 = base(pick(eid, 0)) + pick(rank, 0)
    d1 = base(pick(eid, 1)) + pick(rank, 1)
    dest_ref[...] = jnp.where(lane == 0, d0, jnp.where(lane == 1, d1, 0.0)).astype(I32)


def _dest(eid, rank, pad_start_row):
    M = eid.shape[0]
    tm = min(1024, M)
    tile = pl.BlockSpec((tm, ROUTER_LANES), lambda i: (i, 0))
    return pl.pallas_call(
        _dest_kernel,
        grid=(M // tm,),
        in_specs=[tile, tile, pl.BlockSpec((1, ROUTER_LANES), lambda i: (0, 0))],
        out_specs=tile,
        out_shape=jax.ShapeDtypeStruct((M, ROUTER_LANES), I32),
        compiler_params=_cparams(("parallel",)),
        name="dest",
    )(eid, rank, pad_start_row)


def _invert_kernel(dest_ref, cnt_ref, start_ref, end_ref, tok_ref):
    def clear(r, carry):
        tok_ref[r] = 0
        return carry

    def clear_expert(e, carry):
        lax.fori_loop(start_ref[e] + cnt_ref[e], end_ref[e], clear, 0)
        return carry

    def put(a, carry):
        tok_ref[dest_ref[a]] = lax.shift_right_logical(a, 1)
        return carry

    lax.fori_loop(0, cnt_ref.shape[0], clear_expert, 0)
    lax.fori_loop(end_ref[end_ref.shape[0] - 1], tok_ref.shape[0], clear, 0)
    lax.fori_loop(0, dest_ref.shape[0], put, 0, unroll=16)


def _invert(dest_flat, counts, pad_start, pad_end, n_rows):
    smem = pl.BlockSpec(memory_space=pltpu.SMEM)
    return pl.pallas_call(
        _invert_kernel,
        in_specs=[smem] * 4,
        out_specs=smem,
        out_shape=jax.ShapeDtypeStruct((n_rows,), I32),
        name="invert",
    )(dest_flat, counts, pad_start, pad_end)


def _experts_kernel(be_ref, nused_ref, tok_ref, h_hbm, wg_ref, wu_ref, wd_ref, yb_ref,
                    xbuf, sem, wg_bf, wu_bf, wd_bf):
    i = pl.program_id(0)
    D, F = wg_bf.shape
    nt = D // 128
    rb = yb_ref.shape[0] // nt
    n_used = nused_ref[0]
    slot = i & 1

    def copy(blk, r, s):
        src = pl.multiple_of(tok_ref[blk * rb + r] * nt, nt)
        return pltpu.make_async_copy(h_hbm.at[pl.ds(src, nt)], xbuf.at[s, pl.ds(r * nt, nt)], sem.at[s])

    def wait_block(blk, s):
        def body(r, carry):
            copy(blk, r, s).wait()
            return carry
        lax.fori_loop(0, rb, body, 0, unroll=8)

    @pl.when(i == 0)
    def _():
        def body(r, carry):
            copy(0, r, 0).start()
            return carry
        lax.fori_loop(0, rb, body, 0, unroll=8)

    prev = be_ref[jnp.maximum(i - 1, 0)]

    @pl.when((i == 0) | (be_ref[i] != prev))
    def _():
        wg_bf[...] = wg_ref[0].astype(BF16)
        wu_bf[...] = wu_ref[0].astype(BF16)
        wd_bf[...] = wd_ref[0].astype(BF16)

    @pl.when(i < n_used)
    def _():
        wait_block(i, slot)

        def start_next(r, carry):
            copy(i + 1, 2 * r, 1 - slot).start(priority=0)
            copy(i + 1, 2 * r + 1, 1 - slot).start(priority=1)
            return carry
        lax.fori_loop(0, rb // 2, start_next, 0, unroll=4)

        xb = jnp.concatenate([_load_token_major(xbuf.at[slot], rb, D, s).astype(BF16) for s in range(nt)], axis=1)
        gate = _dot(xb, wg_bf[...])
        up = _dot(xb, wu_bf[...])
        hid = gate * _sigmoid(gate) * up
        _store_token_major(yb_ref, _dot(hid.astype(BF16), wd_bf[...]))

    @pl.when(i == n_used)
    def _():
        wait_block(i, slot)

    @pl.when(i >= n_used)
    def _():
        yb_ref[...] = jnp.zeros_like(yb_ref)


def _experts(block_expert, n_used, tok, h2, w_gate, w_up, w_down, rb):
    E, D, F = w_gate.shape
    nt = D // 128
    nb = tok.shape[0] // rb
    grid_spec = pltpu.PrefetchScalarGridSpec(
        num_scalar_prefetch=3,
        grid=(nb,),
        in_specs=[pl.BlockSpec(memory_space=pl.ANY),
                  pl.BlockSpec((1, D, F), lambda i, be, nu, tk: (be[i], 0, 0)),
                  pl.BlockSpec((1, D, F), lambda i, be, nu, tk: (be[i], 0, 0)),
                  pl.BlockSpec((1, F, D), lambda i, be, nu, tk: (be[i], 0, 0))],
        out_specs=pl.BlockSpec((rb * nt, 128), lambda i, be, nu, tk: (i, 0)),
        scratch_shapes=[pltpu.VMEM((2, rb * nt, 128), F32), pltpu.SemaphoreType.DMA((2,)),
                        pltpu.VMEM((D, F), BF16), pltpu.VMEM((D, F), BF16), pltpu.VMEM((F, D), BF16)],
    )
    return pl.pallas_call(
        _experts_kernel,
        grid_spec=grid_spec,
        out_shape=jax.ShapeDtypeStruct((nb * rb * nt, 128), F32),
        compiler_params=_cparams(("arbitrary",)),
        name="experts",
    )(block_expert, n_used, tok, h2, w_gate, w_up, w_down)


def _combine_kernel(dest_ref, gate_ref, x1_ref, yb_hbm, out_ref, buf, sem):
    i = pl.program_id(0)
    tmc, D = x1_ref.shape
    nt = D // 128
    slot = i & 1

    def copy(tile, t, k, s):
        src = pl.multiple_of(dest_ref[2 * (tile * tmc + t) + k] * nt, nt)
        return pltpu.make_async_copy(yb_hbm.at[pl.ds(src, nt)], buf.at[s, k, pl.ds(t * nt, nt)], sem.at[s])

    def start_tile(tile, s):
        def body(t, carry):
            copy(tile, t, 0, s).start(priority=0)
            copy(tile, t, 1, s).start(priority=1)
            return carry
        lax.fori_loop(0, tmc, body, 0, unroll=8)

    @pl.when(i == 0)
    def _():
        start_tile(0, 0)

    @pl.when(i + 1 < pl.num_programs(0))
    def _():
        start_tile(i + 1, 1 - slot)

    def wait(t, carry):
        copy(i, t, 0, slot).wait()
        copy(i, t, 1, slot).wait()
        return carry

    lax.fori_loop(0, tmc, wait, 0, unroll=8)
    gate = gate_ref[...]
    g0, g1 = gate[:, 0:1], gate[:, 1:2]
    for s in range(nt):
        cols = slice(s * 128, (s + 1) * 128)
        out_ref[:, cols] = (x1_ref[:, cols] + g0 * _load_token_major(buf.at[slot, 0], tmc, D, s)
                            + g1 * _load_token_major(buf.at[slot, 1], tmc, D, s))


def _combine(dest_flat, gates, x1, yb):
    M, D = x1.shape
    tmc = min(256, M)
    grid_spec = pltpu.PrefetchScalarGridSpec(
        num_scalar_prefetch=1,
        grid=(M // tmc,),
        in_specs=[pl.BlockSpec((tmc, ROUTER_LANES), lambda i, d: (i, 0)),
                  pl.BlockSpec((tmc, D), lambda i, d: (i, 0)),
                  pl.BlockSpec(memory_space=pl.ANY)],
        out_specs=pl.BlockSpec((tmc, D), lambda i, d: (i, 0)),
        scratch_shapes=[pltpu.VMEM((2, 2, tmc * (D // 128), 128), F32), pltpu.SemaphoreType.DMA((2,))],
    )
    return pl.pallas_call(
        _combine_kernel,
        grid_spec=grid_spec,
        out_shape=jax.ShapeDtypeStruct((M, D), F32),
        compiler_params=_cparams(("arbitrary",)),
        name="combine",
    )(dest_flat, gates, x1, yb)


def _mixer(x2, B, T, norm_w, w_in, mu, w0, w2, a0, a2, g2, k_k, k_a, r_k, ln_w, ln_b, b_f, qn_w, kn_w):
    rw_heads = w0.shape[0] // HEAD
    fox_heads = b_f.shape[0]
    Wr, Wf = rw_heads * HEAD, fox_heads * HEAD
    lora = w2.shape[0] + a2.shape[0] + g2.shape[0]
    rw_cols = 3 * Wr + lora
    o_w, o_k, o_v, o_a = Wr, Wr + w2.shape[0], 2 * Wr + w2.shape[0], 3 * Wr + w2.shape[0]
    perm = np.concatenate([np.arange(0, Wr), np.arange(o_k, o_k + Wr), np.arange(o_v, o_v + Wr),
                           np.arange(o_w, o_w + w2.shape[0]), np.arange(o_a, rw_cols)])
    w_rw = w_in[:, :rw_cols][:, perm].astype(BF16)
    w_qkv = w_in[:, rw_cols:rw_cols + 3 * Wf].astype(BF16)
    w_f = jnp.pad(w_in[:, rw_cols + 3 * Wf:], ((0, 0), (0, 128 - fox_heads))).astype(BF16)
    p_rw, p_qkv, p_f = _in_proj(x2, norm_w[None, :], w_rw, w_qkv, w_f)

    row = lambda a: a.reshape(1, -1)
    ops = _rw_prep(p_rw.reshape(B, T, rw_cols), row(mu[perm]), row(w0), w2.astype(BF16), row(a0),
                   a2.astype(BF16), g2.astype(BF16), row(k_k), row(k_a), row(r_k), rw_heads)
    y_rw = _rw_scan(*ops, row(ln_w), row(ln_b))

    tile_w = lambda w: row(jnp.tile(w, fox_heads))
    q_aug, k_aug, vt = _fox_prep(p_qkv.reshape(B, T, 3 * Wf), p_f.reshape(B, T, 128),
                                 row(jnp.pad(b_f, (0, 128 - fox_heads))), tile_w(qn_w), tile_w(kn_w), fox_heads)
    y_fox = _fox_attn(q_aug, k_aug, vt)
    return y_rw.reshape(B * T, Wr), y_fox.reshape(B * T, Wf)


def _moe(y_rw, y_fox, x2, w_out, norm_w, rg_w, rg_b, re_w, re_b, w_gate, w_up, w_down):
    M, D = x2.shape
    Wr = y_rw.shape[1]
    pad = ROUTER_LANES - N_GROUPS - N_EXPERTS
    w_r = jnp.pad(jnp.concatenate([rg_w, re_w], axis=1), ((0, 0), (0, pad)))
    b_r = jnp.pad(jnp.concatenate([rg_b, re_b]), (0, pad))[None, :]
    wr_hi = w_r.astype(BF16)
    wr_lo = (w_r - wr_hi.astype(F32)).astype(BF16)
    x1, h2, eid, gates = _out_route(y_rw, y_fox, x2, w_out[:Wr].astype(BF16), w_out[Wr:].astype(BF16),
                                    norm_w[None, :], wr_hi, wr_lo, b_r)
    rank, counts = _rank(eid)

    rb = ROW_BLOCK
    counts = counts[0, :N_EXPERTS]
    padded = (counts + rb - 1) // rb * rb
    pad_end = jnp.cumsum(padded)
    pad_start = pad_end - padded
    n_blocks = (2 * M + N_EXPERTS * (rb - 1) + rb - 1) // rb + 1
    block_start = jnp.arange(n_blocks, dtype=I32) * rb
    block_expert = jnp.minimum(jnp.sum(pad_end[None, :] <= block_start[:, None], axis=1), N_EXPERTS - 1).astype(I32)
    n_used = (pad_end[-1:] // rb).astype(I32)
    start_row = jnp.pad(pad_start.astype(F32), (0, ROUTER_LANES - N_EXPERTS))[None, :]
    dest = _dest(eid, rank, start_row)[:, :2].reshape(-1)

    tok = _invert(dest, counts, pad_start.astype(I32), pad_end.astype(I32), n_blocks * rb)
    yb = _experts(block_expert, n_used, tok, h2, w_gate, w_up, w_down, rb)
    return _combine(dest, gates, x1, yb)


def kernel(x, norm_mix_w, w_in, mu_shift, rw_w0, rw_w2, rw_a0, rw_a2, rw_g2, rw_k_k, rw_k_a, rw_r_k, rw_ln_w, rw_ln_b, fox_b_f, fox_q_norm_w, fox_k_norm_w, w_out, norm_ffn_w, router_group_w, router_group_b, router_expert_w, router_expert_b, exp_w_gate, exp_w_up, exp_w_down):
    B, T, D = x.shape
    x2 = x.reshape(B * T, D)
    for l in range(w_in.shape[0]):
        y_rw, y_fox = _mixer(x2, B, T, norm_mix_w[l], w_in[l], mu_shift[l], rw_w0[l], rw_w2[l], rw_a0[l],
                             rw_a2[l], rw_g2[l], rw_k_k[l], rw_k_a[l], rw_r_k[l].reshape(-1), rw_ln_w[l],
                             rw_ln_b[l], fox_b_f[l], fox_q_norm_w[l], fox_k_norm_w[l])
        x2 = _moe(y_rw, y_fox, x2, w_out[l], norm_ffn_w[l], router_group_w[l], router_group_b[l],
                  router_expert_w[l], router_expert_b[l], exp_w_gate[l], exp_w_up[l], exp_w_down[l])
    return x2.reshape(B, T, D)
```

```python
import functools

import jax
import jax.numpy as jnp
import numpy as np
from jax import lax
from jax.experimental import pallas as pl
from jax.experimental.pallas import tpu as pltpu

F32, BF16, I32 = jnp.float32, jnp.bfloat16, jnp.int32

HEAD = 64
CHUNK = 64
GROUP = 2
GW = GROUP * HEAD
SCAN_BATCH = 4
N_GROUPS = 8
EXPERTS_PER_GROUP = 8
N_EXPERTS = N_GROUPS * EXPERTS_PER_GROUP
ROUTER_LANES = 128
ROW_BLOCK = 256
NORM_EPS = 1e-6
GN_EPS = 64e-5
NEG_BIG = -1e30
LOG2E = 1.4426950408889634
VMEM_LIMIT = 56 * 1024 * 1024


def _cparams(sem):
    return pltpu.CompilerParams(dimension_semantics=sem, vmem_limit_bytes=VMEM_LIMIT)


def _dot(a, b):
    return jnp.dot(a, b, preferred_element_type=F32)


def _dot_nt(a, b):
    return lax.dot_general(a, b, (((1,), (1,)), ((), ())), preferred_element_type=F32)


def _dot_tn(a, b):
    return lax.dot_general(a, b, (((0,), (0,)), ((), ())), preferred_element_type=F32)


def _split3(x):
    hi = x.astype(BF16)
    r1 = x - hi.astype(F32)
    mid = r1.astype(BF16)
    lo = (r1 - mid.astype(F32)).astype(BF16)
    return hi, mid, lo


def _sigmoid(z):
    return 1.0 / (1.0 + jnp.exp(-z))


def _head_sum(x):
    cols = []
    for c in range(x.shape[1] // 128):
        xc = x[:, c * 128:(c + 1) * 128]
        first = lax.broadcasted_iota(I32, xc.shape, 1) < HEAD
        s0 = jnp.sum(jnp.where(first, xc, 0.0), axis=-1, keepdims=True)
        s1 = jnp.sum(jnp.where(first, 0.0, xc), axis=-1, keepdims=True)
        cols.append(jnp.where(first, s0, s1))
    return cols[0] if len(cols) == 1 else jnp.concatenate(cols, axis=1)


def _store_token_major(ref, x):
    rows, d = x.shape
    for s in range(d // 128):
        ref[pl.ds(s, rows, stride=d // 128), :] = x[:, s * 128:(s + 1) * 128]


def _load_token_major(ref, rows, d, s):
    return ref[pl.ds(s, rows, stride=d // 128), :]


def _in_proj_kernel(x_ref, nw_ref, wrw_ref, wqkv_ref, wf_ref, prw_ref, pqkv_ref, pf_ref):
    x = x_ref[...]
    h = x * lax.rsqrt(jnp.mean(x * x, axis=-1, keepdims=True) + NORM_EPS) * nw_ref[...]
    hb = h.astype(BF16)
    prw_ref[...] = _dot(hb, wrw_ref[...])
    pqkv_ref[...] = _dot(hb, wqkv_ref[...])
    pf_ref[...] = _dot(hb, wf_ref[...])


def _in_proj(x2, norm_w, w_rw, w_qkv, w_f):
    M, D = x2.shape
    tm = min(512, M)
    n_rw, n_qkv, n_f = w_rw.shape[1], w_qkv.shape[1], w_f.shape[1]
    full = lambda shape: pl.BlockSpec(shape, lambda i: (0, 0))
    return pl.pallas_call(
        _in_proj_kernel,
        grid=(M // tm,),
        in_specs=[pl.BlockSpec((tm, D), lambda i: (i, 0)), full((1, D)),
                  full((D, n_rw)), full((D, n_qkv)), full((D, n_f))],
        out_specs=[pl.BlockSpec((tm, n_rw), lambda i: (i, 0)),
                   pl.BlockSpec((tm, n_qkv), lambda i: (i, 0)),
                   pl.BlockSpec((tm, n_f), lambda i: (i, 0))],
        out_shape=[jax.ShapeDtypeStruct((M, n_rw), F32),
                   jax.ShapeDtypeStruct((M, n_qkv), F32),
                   jax.ShapeDtypeStruct((M, n_f), F32)],
        compiler_params=_cparams(("parallel",)),
        name="in_proj",
    )(x2, norm_w, w_rw, w_qkv, w_f)


def _rw_prep_kernel(n_heads, p_ref, pprev_ref, mu_ref, w0_ref, w2_ref, a0_ref, a2_ref, g2_ref,
                    kk_ref, ka_ref, rk_ref,
                    rt_ref, at_ref, bt_ref, kt_ref, bb_ref, kb_ref, v_ref, dl_ref, bonus_ref, g_ref):
    W = n_heads * HEAD
    j = pl.program_id(1)
    p = p_ref[0]
    tt = p.shape[0]
    last_prev = jnp.where(j > 0, pprev_ref[0, 7:8, :], 0.0)
    row = lax.broadcasted_iota(I32, p.shape, 0)
    prev = jnp.where(row == 0, last_prev, pltpu.roll(p, 1, axis=0))
    ps = p + (prev - p) * mu_ref[...]
    r, k, v = ps[:, 0:W], ps[:, W:2 * W], ps[:, 2 * W:3 * W]
    o = 3 * W
    pw, pa, pg = ps[:, o:o + 64], ps[:, o + 64:o + 128], ps[:, o + 128:o + 256]

    z = w0_ref[...] + _dot(jnp.tanh(pw).astype(BF16), w2_ref[...])
    lw = (-np.exp(-0.5)).astype(np.float32) * _sigmoid(z)
    a_sig = _sigmoid(a0_ref[...] + _dot(pa.astype(BF16), a2_ref[...]))
    g_ref[0] = _dot(_sigmoid(pg).astype(BF16), g2_ref[...])

    kk = k * kk_ref[...]
    kk = kk / jnp.maximum(jnp.sqrt(_head_sum(kk * kk)), 1e-12)
    km = k * (1.0 + (a_sig - 1.0) * ka_ref[...])
    a_vec = -kk
    b_vec = kk * a_sig
    bonus_ref[0] = _head_sum(r * km * rk_ref[...]) * v
    v_ref[0] = v.astype(BF16)

    ti = lax.broadcasted_iota(I32, (tt, tt), 0)
    si = lax.broadcasted_iota(I32, (tt, tt), 1)
    same = (ti // CHUNK) == (si // CHUNK)
    tri = jnp.where(same & (ti >= si), 1.0, 0.0).astype(BF16)
    ones = jnp.where(same, 1.0, 0.0).astype(BF16)
    nck = tt // CHUNK
    ci = lax.broadcasted_iota(I32, (nck, tt), 0)
    cs = lax.broadcasted_iota(I32, (nck, tt), 1)
    sel = jnp.where(ci == cs // CHUNK, 1.0, 0.0).astype(BF16)
    hi, mid, lo = _split3(lw)
    cum = _dot(tri, hi) + _dot(tri, mid) + _dot(tri, lo)
    tot = _dot(ones, hi) + _dot(ones, mid) + _dot(ones, lo)
    dl = jnp.exp(_dot(sel, hi) + _dot(sel, mid) + _dot(sel, lo))
    for ck in range(nck):
        dl_ref[0, ck] = dl[ck:ck + 1, :]

    e_in = jnp.exp(cum)
    e_ex = jnp.exp(cum - lw)
    e_inv = jnp.exp(-cum)
    e_bar = jnp.exp(tot - cum)
    rt_ref[0] = (r * e_in).astype(BF16)
    at_ref[0] = (a_vec * e_ex).astype(BF16)
    bt_ref[0] = (b_vec * e_inv).astype(BF16)
    kt_ref[0] = (km * e_inv).astype(BF16)
    bb_ref[0] = (b_vec * e_bar).astype(BF16)
    kb_ref[0] = (km * e_bar).astype(BF16)


def _rw_prep(p_rw, mu, w0, w2, a0, a2, g2, k_k, k_a, r_k, n_heads):
    B, T, P = p_rw.shape
    W = n_heads * HEAD
    tt = min(512, T)
    row = lambda a: pl.BlockSpec(a.shape, lambda b, j: (0, 0))
    tile = lambda w: pl.BlockSpec((1, tt, w), lambda b, j: (b, j, 0))
    bf = jax.ShapeDtypeStruct((B, T, W), BF16)
    f32 = jax.ShapeDtypeStruct((B, T, W), F32)
    return pl.pallas_call(
        functools.partial(_rw_prep_kernel, n_heads),
        grid=(B, T // tt),
        in_specs=[tile(P),
                  pl.BlockSpec((1, 8, P), lambda b, j: (b, jnp.maximum(j * (tt // 8) - 1, 0), 0)),
                  row(mu), row(w0), row(w2), row(a0), row(a2), row(g2), row(k_k), row(k_a), row(r_k)],
        out_specs=[tile(W)] * 7 + [pl.BlockSpec((1, tt // CHUNK, 1, W), lambda b, j: (b, j, 0, 0)), tile(W), tile(W)],
        out_shape=[bf] * 7 + [jax.ShapeDtypeStruct((B, T // CHUNK, 1, W), F32), f32, f32],
        compiler_params=_cparams(("parallel", "parallel")),
        name="rw_prep",
    )(p_rw, p_rw, mu, w0, w2, a0, a2, g2, k_k, k_a, r_k)


def _rw_scan_kernel(rt_ref, at_ref, bt_ref, kt_ref, bb_ref, kb_ref, v_ref, dl_ref,
                    bonus_ref, g_ref, lnw_ref, lnb_ref, y_ref, s_ref):
    C = CHUNK
    nb, tt, W = rt_ref.shape
    chains = [(b, g) for b in range(nb) for g in range(W // GW)]

    @pl.when(pl.program_id(1) == 0)
    def _():
        s_ref[...] = jnp.zeros_like(s_ref)

    ri = lax.broadcasted_iota(I32, (GW, GW), 0)
    ci = lax.broadcasted_iota(I32, (GW, GW), 1)
    tr, tc = ri & (C - 1), ci & (C - 1)
    strict, incl, eye = tr > tc, tr >= tc, ri == ci
    lane_head = lax.broadcasted_iota(I32, (C, GW), 1) // HEAD

    def bd(x):
        return jnp.concatenate([jnp.where(lane_head == h, x, jnp.zeros_like(x)) for h in range(GROUP)], axis=0)

    bf = lambda xs: [x.astype(BF16) for x in xs]

    def chunk(c, carry):
        rows = pl.ds(pl.multiple_of(c * C, C), C)
        ld = lambda ref: [bd(ref[b, rows, g * GW:(g + 1) * GW]) for b, g in chains]
        Rt, At, Bt, Kt, Bb, Kb, V = (ld(r) for r in (rt_ref, at_ref, bt_ref, kt_ref, bb_ref, kb_ref, v_ref))
        BK = [jnp.concatenate([b, k], axis=0) for b, k in zip(Bt, Kt)]
        XA = [_dot_nt(a, bk) for a, bk in zip(At, BK)]
        XR = [_dot_nt(r, bk) for r, bk in zip(Rt, BK)]
        Lab = [jnp.where(strict, x[:, :GW], 0.0) for x in XA]
        Lak = bf([jnp.where(strict, x[:, GW:], 0.0) for x in XA])
        Mrb = bf([jnp.where(incl, x[:, :GW], 0.0) for x in XR])
        Mrk = bf([jnp.where(incl, x[:, GW:], 0.0) for x in XR])
        P = [jnp.where(eye, 1.0, l) for l in Lab]
        Lp = Lab
        for _ in range(5):
            Lpb = bf(Lp)
            Lp = [_dot(x, x) for x in Lpb]
            P = [p + _dot(pb, lb) for p, pb, lb in zip(P, bf(P), bf(Lp))]
        Pb = bf(P)
        Wk = bf([_dot(l, v) for l, v in zip(Lak, V)])
        AU = bf([_dot(p, jnp.concatenate([a, w], axis=1)) for p, a, w in zip(Pb, At, Wk)])
        Ahb, U0b = [x[:, :GW] for x in AU], [x[:, GW:] for x in AU]
        MAU = [_dot(m, x) for m, x in zip(Mrb, AU)]
        Rh = bf([r.astype(F32) + x[:, :GW] for r, x in zip(Rt, MAU)])
        Y0 = [x[:, GW:] + _dot(mk, v) for x, mk, v in zip(MAU, Mrk, V)]
        GT = bf([_dot_tn(a, b) for a, b in zip(Ahb, Bb)])
        HT = [_dot_tn(u, b) + _dot_tn(v, k) for u, b, v, k in zip(U0b, Bb, V, Kb)]
        S0 = [s_ref[i] for i in range(len(chains))]
        S0b = bf(S0)
        Y = [_dot_nt(r, s) + y0 for r, s, y0 in zip(Rh, S0b, Y0)]
        for i, (b, g) in enumerate(chains):
            ls = slice(g * GW, (g + 1) * GW)
            s_ref[i] = S0[i] * dl_ref[b, c, :, ls] + _dot(S0b[i], GT[i]) + HT[i]
            y = sum(Y[i][h * C:(h + 1) * C] for h in range(GROUP))
            mean = _head_sum(y) * (1.0 / HEAD)
            yc = y - mean
            var = _head_sum(yc * yc) * (1.0 / HEAD)
            yn = yc * lax.rsqrt(var + GN_EPS) * lnw_ref[:, ls] + lnb_ref[:, ls]
            y_ref[b, rows, ls] = (yn + bonus_ref[b, rows, ls]) * g_ref[b, rows, ls]
        return carry

    lax.fori_loop(0, tt // C, chunk, 0)


def _rw_scan(rt, at, bt, kt, bb, kb, v, dl, bonus, g, ln_w, ln_b):
    B, T, W = rt.shape
    nb = min(SCAN_BATCH, B)
    tt = min(256, T)
    tile = pl.BlockSpec((nb, tt, W), lambda b, j: (b, j, 0))
    row = pl.BlockSpec((1, W), lambda b, j: (0, 0))
    return pl.pallas_call(
        _rw_scan_kernel,
        grid=(B // nb, T // tt),
        in_specs=[tile] * 7 + [pl.BlockSpec((nb, tt // CHUNK, 1, W), lambda b, j: (b, j, 0, 0)), tile, tile, row, row],
        out_specs=tile,
        out_shape=jax.ShapeDtypeStruct((B, T, W), F32),
        scratch_shapes=[pltpu.VMEM((nb * (W // GW), GW, GW), F32)],
        compiler_params=_cparams(("parallel", "arbitrary")),
        name="rw_scan",
    )(rt, at, bt, kt, bb, kb, v, dl, bonus, g, ln_w, ln_b)


def _fox_prep_kernel(n_heads, qkv_ref, f_ref, bf_ref, qnw_ref, knw_ref,
                     q_ref, k_ref, vt_ref, carry_ref):
    W = n_heads * HEAD
    tt = qkv_ref.shape[1]

    @pl.when(pl.program_id(1) == 0)
    def _():
        carry_ref[...] = jnp.zeros_like(carry_ref)

    qkv = qkv_ref[0]
    q, k, v = qkv[:, 0:W], qkv[:, W:2 * W], qkv[:, 2 * W:3 * W]
    inv_n = 1.0 / HEAD
    qn = q * lax.rsqrt(_head_sum(q * q) * inv_n + NORM_EPS) * (qnw_ref[...] * (HEAD ** -0.5 * LOG2E))
    kn = k * lax.rsqrt(_head_sum(k * k) * inv_n + NORM_EPS) * knw_ref[...]
    for p in range(n_heads // 2):
        vt_ref[0, p, 0] = v[:, p * 128:(p + 1) * 128].T.astype(BF16)

    zf = f_ref[0] + bf_ref[...]
    logf = jnp.minimum(zf, 0.0) - jnp.log(1.0 + jnp.exp(-jnp.abs(zf)))
    ti = lax.broadcasted_iota(I32, (tt, tt), 0)
    si = lax.broadcasted_iota(I32, (tt, tt), 1)
    tri = jnp.where(ti >= si, 1.0, 0.0).astype(BF16)
    hi, mid, lo = _split3(logf)
    c = carry_ref[...] + _dot(tri, hi) + _dot(tri, mid) + _dot(tri, lo)
    carry_ref[...] = c[tt - 1:tt, :]
    c_hi, c_mid, c_lo = _split3(c * (-LOG2E))
    lane = lax.broadcasted_iota(I32, (tt, 128), 1)
    c3 = jnp.where(lane < 8, c_hi.astype(F32),
                   jnp.where(lane < 16, pltpu.roll(c_mid.astype(F32), 8, axis=1),
                             pltpu.roll(c_lo.astype(F32), 16, axis=1))).astype(BF16)
    src = lax.broadcasted_iota(I32, (128, 128), 0)
    dst = lax.broadcasted_iota(I32, (128, 128), 1)
    ones = jnp.where((lane >= HEAD) & (lane < HEAD + 3), 1.0, 0.0)
    for h in range(n_heads):
        col = slice((h // 2) * 128, (h // 2 + 1) * 128)
        qh, kh = qn[:, col], kn[:, col]
        if h % 2:
            qh, kh = pltpu.roll(qh, HEAD, axis=1), pltpu.roll(kh, HEAD, axis=1)
        sel = jnp.where((dst >= HEAD) & (dst < HEAD + 3) & (src == h + 8 * (dst - HEAD)), 1.0, 0.0).astype(BF16)
        q_ref[0, h] = jnp.where(lane < HEAD, qh, ones).astype(BF16)
        k_ref[0, h] = jnp.where(lane < HEAD, kh, _dot(c3, sel)).astype(BF16)


def _fox_prep(qkv, f, b_f, qn_w, kn_w, n_heads):
    B, T, _ = qkv.shape
    W = n_heads * HEAD
    tt = min(256, T)
    row = lambda a: pl.BlockSpec(a.shape, lambda b, j: (0, 0))
    aug = jax.ShapeDtypeStruct((B, n_heads, T, 128), BF16)
    aug_spec = pl.BlockSpec((1, n_heads, tt, 128), lambda b, j: (b, 0, j, 0))
    return pl.pallas_call(
        functools.partial(_fox_prep_kernel, n_heads),
        grid=(B, T // tt),
        in_specs=[pl.BlockSpec((1, tt, 3 * W), lambda b, j: (b, j, 0)),
                  pl.BlockSpec((1, tt, 128), lambda b, j: (b, j, 0)),
                  row(b_f), row(qn_w), row(kn_w)],
        out_specs=[aug_spec, aug_spec,
                   pl.BlockSpec((1, n_heads // 2, 1, 128, tt), lambda b, j: (b, 0, j, 0, 0))],
        out_shape=[aug, aug, jax.ShapeDtypeStruct((B, n_heads // 2, T // tt, 128, tt), BF16)],
        scratch_shapes=[pltpu.VMEM((1, 128), F32)],
        compiler_params=_cparams(("parallel", "arbitrary")),
        name="fox_prep",
    )(qkv, f, b_f, qn_w, kn_w)


def _fox_attn_kernel(q_ref, k_ref, vt_ref, o_ref, sa_ref, sb_ref):
    i = pl.program_id(2)
    tq = q_ref.shape[2]
    sub = vt_ref.shape[4]
    q = [q_ref[0, hh] for hh in range(2)]
    ki = lax.broadcasted_iota(I32, (tq, tq), 0)
    qi = lax.broadcasted_iota(I32, (tq, tq), 1)

    def scores(j, s_ref, diag=False):
        k0 = pl.multiple_of(j * tq, tq)
        for hh in range(2):
            s = _dot_nt(k_ref[0, hh, pl.ds(k0, tq), :], q[hh])
            s_ref[hh] = jnp.where(ki <= qi, s, -jnp.inf) if diag else s

    def update(j, s_ref, carry):
        out = []
        for hh in range(2):
            m, l, acc = carry[hh]
            m_new = jnp.maximum(m, jnp.max(s_ref[hh], axis=0, keepdims=True))
            alpha = jnp.exp2(m - m_new)
            p = jnp.exp2(s_ref[hh] - m_new)
            l = alpha * l + jnp.sum(p, axis=0, keepdims=True)
            p = p.astype(BF16)
            pv = sum(_dot(vt_ref[0, 0, j * (tq // sub) + u], p[u * sub:(u + 1) * sub]) for u in range(tq // sub))
            out.append((m_new, l, alpha * acc + pv))
        return tuple(out)

    def two_blocks(jj, carry):
        scores(2 * jj + 1, sb_ref)
        carry = update(2 * jj, sa_ref, carry)
        scores(jnp.minimum(2 * jj + 2, i), sa_ref)
        return update(2 * jj + 1, sb_ref, carry)

    init = tuple((jnp.full((1, tq), NEG_BIG, F32), jnp.zeros((1, tq), F32), jnp.zeros((128, tq), F32))
                 for _ in range(2))
    scores(0, sa_ref)
    carry = lax.fori_loop(0, i // 2, two_blocks, init)
    carry = lax.fori_loop(0, i & 1, lambda _, c: update(i - 1, sa_ref, c), carry)
    scores(i, sb_ref, diag=True)
    (_, l0, acc0), (_, l1, acc1) = update(i, sb_ref, carry)
    row = lax.broadcasted_iota(I32, (128, tq), 0)
    o_ref[0] = jnp.where(row < HEAD, acc0 / l0, acc1 / l1).T


def _fox_attn(q_aug, k_aug, vt):
    B, H, T, _ = q_aug.shape
    tq = min(512, T)
    n_sub, sub = vt.shape[2], vt.shape[4]
    return pl.pallas_call(
        _fox_attn_kernel,
        grid=(B, H // 2, T // tq),
        in_specs=[pl.BlockSpec((1, 2, tq, 128), lambda b, p, i: (b, p, i, 0)),
                  pl.BlockSpec((1, 2, T, 128), lambda b, p, i: (b, p, 0, 0)),
                  pl.BlockSpec((1, 1, n_sub, 128, sub), lambda b, p, i: (b, p, 0, 0, 0))],
        out_specs=pl.BlockSpec((1, tq, 128), lambda b, p, i: (b, i, p)),
        out_shape=jax.ShapeDtypeStruct((B, T, H * HEAD), F32),
        scratch_shapes=[pltpu.VMEM((2, tq, tq), F32), pltpu.VMEM((2, tq, tq), F32)],
        compiler_params=_cparams(("parallel", "parallel", "arbitrary")),
        name="fox_attn",
    )(q_aug, k_aug, vt)


def _out_route_kernel(yrw_ref, yfox_ref, x_ref, wo_rw_ref, wo_fox_ref, nw_ref, wr_hi_ref, wr_lo_ref, br_ref,
                      x1_ref, h2_ref, eid_ref, gate_ref):
    x1 = (x_ref[...] + _dot(yrw_ref[...].astype(BF16), wo_rw_ref[...])
          + _dot(yfox_ref[...].astype(BF16), wo_fox_ref[...]))
    x1_ref[...] = x1
    h2 = x1 * lax.rsqrt(jnp.mean(x1 * x1, axis=-1, keepdims=True) + NORM_EPS) * nw_ref[...]
    _store_token_major(h2_ref, h2)
    h_hi = h2.astype(BF16)
    h_lo = (h2 - h_hi.astype(F32)).astype(BF16)
    logits = (_dot(h_hi, wr_hi_ref[...]) + _dot(h_hi, wr_lo_ref[...]) + _dot(h_lo, wr_hi_ref[...])) + br_ref[...]

    lane_i = lax.broadcasted_iota(I32, logits.shape, 1)
    lane = lane_i.astype(F32)
    first = lambda mask: jnp.min(jnp.where(mask, lane, 1e9), axis=-1, keepdims=True)
    gl = jnp.where(lane_i < N_GROUPS, logits, -jnp.inf)
    gmax = jnp.max(gl, axis=-1, keepdims=True)
    g_sel = first(gl == gmax)
    g_gate = 1.0 / jnp.sum(jnp.exp(gl - gmax), axis=-1, keepdims=True)
    e_lane = lane_i - N_GROUPS
    lane_grp = jnp.right_shift(e_lane, 3).astype(F32)
    in_grp = (e_lane >= 0) & (e_lane < N_EXPERTS) & (lane_grp == g_sel)
    el = jnp.where(in_grp, logits, -jnp.inf)
    m1 = jnp.max(el, axis=-1, keepdims=True)
    i1 = first(el == m1)
    el2 = jnp.where(lane == i1, -jnp.inf, el)
    m2 = jnp.max(el2, axis=-1, keepdims=True)
    i2 = first(el2 == m2)
    e2 = jnp.exp(m2 - m1)
    g1 = g_gate / (1.0 + e2)
    g2 = g_gate * e2 / (1.0 + e2)
    eid_ref[...] = jnp.where(lane_i == 0, i1 - N_GROUPS, jnp.where(lane_i == 1, i2 - N_GROUPS, 0.0)).astype(I32)
    gate_ref[...] = jnp.where(lane_i == 0, g1, jnp.where(lane_i == 1, g2, 0.0))


def _out_route(y_rw, y_fox, x2, wo_rw, wo_fox, norm_w, wr_hi, wr_lo, b_r):
    M, D = x2.shape
    W = y_rw.shape[1]
    tm = min(512, M)
    full = lambda a: pl.BlockSpec(a.shape, lambda i: (0, 0))
    tile = lambda w: pl.BlockSpec((tm, w), lambda i: (i, 0))
    return pl.pallas_call(
        _out_route_kernel,
        grid=(M // tm,),
        in_specs=[tile(W), tile(W), tile(D), full(wo_rw), full(wo_fox), full(norm_w),
                  full(wr_hi), full(wr_lo), full(b_r)],
        out_specs=[tile(D), pl.BlockSpec((tm * (D // 128), 128), lambda i: (i, 0)),
                   tile(ROUTER_LANES), tile(ROUTER_LANES)],
        out_shape=[jax.ShapeDtypeStruct((M, D), F32), jax.ShapeDtypeStruct((M * (D // 128), 128), F32),
                   jax.ShapeDtypeStruct((M, ROUTER_LANES), I32), jax.ShapeDtypeStruct((M, ROUTER_LANES), F32)],
        compiler_params=_cparams(("parallel",)),
        name="out_route",
    )(y_rw, y_fox, x2, wo_rw, wo_fox, norm_w, wr_hi, wr_lo, b_r)


def _rank_kernel(eid_ref, rank_ref, cnt_ref, carry_ref):
    @pl.when(pl.program_id(0) == 0)
    def _():
        carry_ref[...] = jnp.zeros_like(carry_ref)

    eid = eid_ref[...].astype(F32)
    tm = eid.shape[0]
    lane = lax.broadcasted_iota(I32, eid.shape, 1)
    lane_f = lane.astype(F32)
    pick = lambda l: jnp.sum(jnp.where(lane == l, eid, 0.0), axis=-1, keepdims=True)
    e0, e1 = pick(0), pick(1)
    oh0 = (lane_f == e0).astype(F32)
    oh1 = (lane_f == e1).astype(F32)
    both = oh0 + oh1
    ri = lax.broadcasted_iota(I32, (tm, tm), 0)
    ci = lax.broadcasted_iota(I32, (tm, tm), 1)
    before = _dot(jnp.where(ri > ci, 1.0, 0.0).astype(BF16), both.astype(BF16)) + carry_ref[...]
    r0 = jnp.sum(oh0 * before, axis=-1, keepdims=True)
    r1 = jnp.sum(oh1 * (before + oh0), axis=-1, keepdims=True)
    rank_ref[...] = jnp.where(lane == 0, r0, jnp.where(lane == 1, r1, 0.0)).astype(I32)
    total = carry_ref[...] + jnp.sum(both, axis=0, keepdims=True)
    carry_ref[...] = total
    cnt_ref[...] = jnp.broadcast_to(total, cnt_ref.shape).astype(I32)


def _rank(eid):
    M = eid.shape[0]
    tm = min(512, M)
    return pl.pallas_call(
        _rank_kernel,
        grid=(M // tm,),
        in_specs=[pl.BlockSpec((tm, ROUTER_LANES), lambda i: (i, 0))],
        out_specs=[pl.BlockSpec((tm, ROUTER_LANES), lambda i: (i, 0)),
                   pl.BlockSpec((8, ROUTER_LANES), lambda i: (0, 0))],
        out_shape=[jax.ShapeDtypeStruct((M, ROUTER_LANES), I32), jax.ShapeDtypeStruct((8, ROUTER_LANES), I32)],
        scratch_shapes=[pltpu.VMEM((1, ROUTER_LANES), F32)],
        compiler_params=_cparams(("arbitrary",)),
        name="rank",
    )(eid)


def _dest_kernel(eid_ref, rank_ref, start_ref, dest_ref):
    eid = eid_ref[...].astype(F32)
    rank = rank_ref[...].astype(F32)
    lane = lax.broadcasted_iota(I32, eid.shape, 1)
    lane_f = lane.astype(F32)
    pick = lambda x, l: jnp.sum(jnp.where(lane == l, x, 0.0), axis=-1, keepdims=True)
    base = lambda e: jnp.sum(jnp.where(lane_f == e, start_ref[...], 0.0), axis=-1, keepdims=True)
    d0 = base(pick(eid, 0)) + pick(rank, 0)
    d1 = base(pick(eid, 1)) + pick(rank, 1)
    dest_ref[...] = jnp.where(lane == 0, d0, jnp.where(lane == 1, d1, 0.0)).astype(I32)


def _dest(eid, rank, pad_start_row):
    M = eid.shape[0]
    tm = min(1024, M)
    tile = pl.BlockSpec((tm, ROUTER_LANES), lambda i: (i, 0))
    return pl.pallas_call(
        _dest_kernel,
        grid=(M // tm,),
        in_specs=[tile, tile, pl.BlockSpec((1, ROUTER_LANES), lambda i: (0, 0))],
        out_specs=tile,
        out_shape=jax.ShapeDtypeStruct((M, ROUTER_LANES), I32),
        compiler_params=_cparams(("parallel",)),
        name="dest",
    )(eid, rank, pad_start_row)


def _invert_kernel(dest_ref, cnt_ref, start_ref, end_ref, tok_ref):
    def clear(r, carry):
        tok_ref[r] = 0
        return carry

    def clear_expert(e, carry):
        lax.fori_loop(start_ref[e] + cnt_ref[e], end_ref[e], clear, 0)
        return carry

    def put(a, carry):
        tok_ref[dest_ref[a]] = lax.shift_right_logical(a, 1)
        return carry

    lax.fori_loop(0, cnt_ref.shape[0], clear_expert, 0)
    lax.fori_loop(end_ref[end_ref.shape[0] - 1], tok_ref.shape[0], clear, 0)
    lax.fori_loop(0, dest_ref.shape[0], put, 0, unroll=16)


def _invert(dest_flat, counts, pad_start, pad_end, n_rows):
    smem = pl.BlockSpec(memory_space=pltpu.SMEM)
    return pl.pallas_call(
        _invert_kernel,
        in_specs=[smem] * 4,
        out_specs=smem,
        out_shape=jax.ShapeDtypeStruct((n_rows,), I32),
        name="invert",
    )(dest_flat, counts, pad_start, pad_end)


def _experts_kernel(be_ref, nused_ref, tok_ref, h_hbm, wg_ref, wu_ref, wd_ref, yb_ref,
                    xbuf, sem, wg_bf, wu_bf, wd_bf):
    i = pl.program_id(0)
    D, F = wg_bf.shape
    nt = D // 128
    rb = yb_ref.shape[0] // nt
    n_used = nused_ref[0]
    slot = i & 1

    def copy(blk, r, s):
        src = pl.multiple_of(tok_ref[blk * rb + r] * nt, nt)
        return pltpu.make_async_copy(h_hbm.at[pl.ds(src, nt)], xbuf.at[s, pl.ds(r * nt, nt)], sem.at[s])

    def wait_block(blk, s):
        def body(r, carry):
            copy(blk, r, s).wait()
            return carry
        lax.fori_loop(0, rb, body, 0, unroll=8)

    @pl.when(i == 0)
    def _():
        def body(r, carry):
            copy(0, r, 0).start()
            return carry
        lax.fori_loop(0, rb, body, 0, unroll=8)

    prev = be_ref[jnp.maximum(i - 1, 0)]

    @pl.when((i == 0) | (be_ref[i] != prev))
    def _():
        wg_bf[...] = wg_ref[0].astype(BF16)
        wu_bf[...] = wu_ref[0].astype(BF16)
        wd_bf[...] = wd_ref[0].astype(BF16)

    @pl.when(i < n_used)
    def _():
        wait_block(i, slot)

        def start_next(r, carry):
            copy(i + 1, 2 * r, 1 - slot).start(priority=0)
            copy(i + 1, 2 * r + 1, 1 - slot).start(priority=1)
            return carry
        lax.fori_loop(0, rb // 2, start_next, 0, unroll=4)

        xb = jnp.concatenate([_load_token_major(xbuf.at[slot], rb, D, s).astype(BF16) for s in range(nt)], axis=1)
        gate = _dot(xb, wg_bf[...])
        up = _dot(xb, wu_bf[...])
        hid = gate * _sigmoid(gate) * up
        _store_token_major(yb_ref, _dot(hid.astype(BF16), wd_bf[...]))

    @pl.when(i == n_used)
    def _():
        wait_block(i, slot)

    @pl.when(i >= n_used)
    def _():
        yb_ref[...] = jnp.zeros_like(yb_ref)


def _experts(block_expert, n_used, tok, h2, w_gate, w_up, w_down, rb):
    E, D, F = w_gate.shape
    nt = D // 128
    nb = tok.shape[0] // rb
    grid_spec = pltpu.PrefetchScalarGridSpec(
        num_scalar_prefetch=3,
        grid=(nb,),
        in_specs=[pl.BlockSpec(memory_space=pl.ANY),
                  pl.BlockSpec((1, D, F), lambda i, be, nu, tk: (be[i], 0, 0)),
                  pl.BlockSpec((1, D, F), lambda i, be, nu, tk: (be[i], 0, 0)),
                  pl.BlockSpec((1, F, D), lambda i, be, nu, tk: (be[i], 0, 0))],
        out_specs=pl.BlockSpec((rb * nt, 128), lambda i, be, nu, tk: (i, 0)),
        scratch_shapes=[pltpu.VMEM((2, rb * nt, 128), F32), pltpu.SemaphoreType.DMA((2,)),
                        pltpu.VMEM((D, F), BF16), pltpu.VMEM((D, F), BF16), pltpu.VMEM((F, D), BF16)],
    )
    return pl.pallas_call(
        _experts_kernel,
        grid_spec=grid_spec,
        out_shape=jax.ShapeDtypeStruct((nb * rb * nt, 128), F32),
        compiler_params=_cparams(("arbitrary",)),
        name="experts",
    )(block_expert, n_used, tok, h2, w_gate, w_up, w_down)


def _combine_kernel(dest_ref, gate_ref, x1_ref, yb_hbm, out_ref, buf, sem):
    i = pl.program_id(0)
    tmc, D = x1_ref.shape
    nt = D // 128
    slot = i & 1

    def copy(tile, t, k, s):
        src = pl.multiple_of(dest_ref[2 * (tile * tmc + t) + k] * nt, nt)
        return pltpu.make_async_copy(yb_hbm.at[pl.ds(src, nt)], buf.at[s, k, pl.ds(t * nt, nt)], sem.at[s])

    def start_tile(tile, s):
        def body(t, carry):
            copy(tile, t, 0, s).start(priority=0)
            copy(tile, t, 1, s).start(priority=1)
            return carry
        lax.fori_loop(0, tmc, body, 0, unroll=8)

    @pl.when(i == 0)
    def _():
        start_tile(0, 0)

    @pl.when(i + 1 < pl.num_programs(0))
    def _():
        start_tile(i + 1, 1 - slot)

    def wait(t, carry):
        copy(i, t, 0, slot).wait()
        copy(i, t, 1, slot).wait()
        return carry

    lax.fori_loop(0, tmc, wait, 0, unroll=8)
    gate = gate_ref[...]
    g0, g1 = gate[:, 0:1], gate[:, 1:2]
    for s in range(nt):
        cols = slice(s * 128, (s + 1) * 128)
        out_ref[:, cols] = (x1_ref[:, cols] + g0 * _load_token_major(buf.at[slot, 0], tmc, D, s)
                            + g1 * _load_token_major(buf.at[slot, 1], tmc, D, s))


def _combine(dest_flat, gates, x1, yb):
    M, D = x1.shape
    tmc = min(256, M)
    grid_spec = pltpu.PrefetchScalarGridSpec(
        num_scalar_prefetch=1,
        grid=(M // tmc,),
        in_specs=[pl.BlockSpec((tmc, ROUTER_LANES), lambda i, d: (i, 0)),
                  pl.BlockSpec((tmc, D), lambda i, d: (i, 0)),
                  pl.BlockSpec(memory_space=pl.ANY)],
        out_specs=pl.BlockSpec((tmc, D), lambda i, d: (i, 0)),
        scratch_shapes=[pltpu.VMEM((2, 2, tmc * (D // 128), 128), F32), pltpu.SemaphoreType.DMA((2,))],
    )
    return pl.pallas_call(
        _combine_kernel,
        grid_spec=grid_spec,
        out_shape=jax.ShapeDtypeStruct((M, D), F32),
        compiler_params=_cparams(("arbitrary",)),
        name="combine",
    )(dest_flat, gates, x1, yb)


def _mixer(x2, B, T, norm_w, w_in, mu, w0, w2, a0, a2, g2, k_k, k_a, r_k, ln_w, ln_b, b_f, qn_w, kn_w):
    rw_heads = w0.shape[0] // HEAD
    fox_heads = b_f.shape[0]
    Wr, Wf = rw_heads * HEAD, fox_heads * HEAD
    lora = w2.shape[0] + a2.shape[0] + g2.shape[0]
    rw_cols = 3 * Wr + lora
    o_w, o_k, o_v, o_a = Wr, Wr + w2.shape[0], 2 * Wr + w2.shape[0], 3 * Wr + w2.shape[0]
    perm = np.concatenate([np.arange(0, Wr), np.arange(o_k, o_k + Wr), np.arange(o_v, o_v + Wr),
                           np.arange(o_w, o_w + w2.shape[0]), np.arange(o_a, rw_cols)])
    w_rw = w_in[:, :rw_cols][:, perm].astype(BF16)
    w_qkv = w_in[:, rw_cols:rw_cols + 3 * Wf].astype(BF16)
    w_f = jnp.pad(w_in[:, rw_cols + 3 * Wf:], ((0, 0), (0, 128 - fox_heads))).astype(BF16)
    p_rw, p_qkv, p_f = _in_proj(x2, norm_w[None, :], w_rw, w_qkv, w_f)

    row = lambda a: a.reshape(1, -1)
    ops = _rw_prep(p_rw.reshape(B, T, rw_cols), row(mu[perm]), row(w0), w2.astype(BF16), row(a0),
                   a2.astype(BF16), g2.astype(BF16), row(k_k), row(k_a), row(r_k), rw_heads)
    y_rw = _rw_scan(*ops, row(ln_w), row(ln_b))

    tile_w = lambda w: row(jnp.tile(w, fox_heads))
    q_aug, k_aug, vt = _fox_prep(p_qkv.reshape(B, T, 3 * Wf), p_f.reshape(B, T, 128),
                                 row(jnp.pad(b_f, (0, 128 - fox_heads))), tile_w(qn_w), tile_w(kn_w), fox_heads)
    y_fox = _fox_attn(q_aug, k_aug, vt)
    return y_rw.reshape(B * T, Wr), y_fox.reshape(B * T, Wf)


def _moe(y_rw, y_fox, x2, w_out, norm_w, rg_w, rg_b, re_w, re_b, w_gate, w_up, w_down):
    M, D = x2.shape
    Wr = y_rw.shape[1]
    pad = ROUTER_LANES - N_GROUPS - N_EXPERTS
    w_r = jnp.pad(jnp.concatenate([rg_w, re_w], axis=1), ((0, 0), (0, pad)))
    b_r = jnp.pad(jnp.concatenate([rg_b, re_b]), (0, pad))[None, :]
    wr_hi = w_r.astype(BF16)
    wr_lo = (w_r - wr_hi.astype(F32)).astype(BF16)
    x1, h2, eid, gates = _out_route(y_rw, y_fox, x2, w_out[:Wr].astype(BF16), w_out[Wr:].astype(BF16),
                                    norm_w[None, :], wr_hi, wr_lo, b_r)
    rank, counts = _rank(eid)

    rb = ROW_BLOCK
    counts = counts[0, :N_EXPERTS]
    padded = (counts + rb - 1) // rb * rb
    pad_end = jnp.cumsum(padded)
    pad_start = pad_end - padded
    n_blocks = (2 * M + N_EXPERTS * (rb - 1) + rb - 1) // rb + 1
    block_start = jnp.arange(n_blocks, dtype=I32) * rb
    block_expert = jnp.minimum(jnp.sum(pad_end[None, :] <= block_start[:, None], axis=1), N_EXPERTS - 1).astype(I32)
    n_used = (pad_end[-1:] // rb).astype(I32)
    start_row = jnp.pad(pad_start.astype(F32), (0, ROUTER_LANES - N_EXPERTS))[None, :]
    dest = _dest(eid, rank, start_row)[:, :2].reshape(-1)

    tok = _invert(dest, counts, pad_start.astype(I32), pad_end.astype(I32), n_blocks * rb)
    yb = _experts(block_expert, n_used, tok, h2, w_gate, w_up, w_down, rb)
    return _combine(dest, gates, x1, yb)


def kernel(x, norm_mix_w, w_in, mu_shift, rw_w0, rw_w2, rw_a0, rw_a2, rw_g2, rw_k_k, rw_k_a, rw_r_k, rw_ln_w, rw_ln_b, fox_b_f, fox_q_norm_w, fox_k_norm_w, w_out, norm_ffn_w, router_group_w, router_group_b, router_expert_w, router_expert_b, exp_w_gate, exp_w_up, exp_w_down):
    B, T, D = x.shape
    x2 = x.reshape(B * T, D)
    for l in range(w_in.shape[0]):
        y_rw, y_fox = _mixer(x2, B, T, norm_mix_w[l], w_in[l], mu_shift[l], rw_w0[l], rw_w2[l], rw_a0[l],
                             rw_a2[l], rw_g2[l], rw_k_k[l], rw_k_a[l], rw_r_k[l].reshape(-1), rw_ln_w[l],
                             rw_ln_b[l], fox_b_f[l], fox_q_norm_w[l], fox_k_norm_w[l])
        x2 = _moe(y_rw, y_fox, x2, w_out[l], norm_ffn_w[l], router_group_w[l], router_group_b[l],
                  router_expert_w[l], router_expert_b[l], exp_w_gate[l], exp_w_up[l], exp_w_down[l])
    return x2.reshape(B, T, D)
```

```python
import functools

import jax
import jax.numpy as jnp
import numpy as np
from jax import lax
from jax.experimental import pallas as pl
from jax.experimental.pallas import tpu as pltpu

F32, BF16, I32 = jnp.float32, jnp.bfloat16, jnp.int32

HEAD = 64
CHUNK = 64
GROUP = 2
GW = GROUP * HEAD
SCAN_BATCH = 4
N_GROUPS = 8
EXPERTS_PER_GROUP = 8
N_EXPERTS = N_GROUPS * EXPERTS_PER_GROUP
ROUTER_LANES = 128
ROW_BLOCK = 256
NORM_EPS = 1e-6
GN_EPS = 64e-5
NEG_BIG = -1e30
LOG2E = 1.4426950408889634
VMEM_LIMIT = 56 * 1024 * 1024


def _cparams(sem):
    return pltpu.CompilerParams(dimension_semantics=sem, vmem_limit_bytes=VMEM_LIMIT)


def _dot(a, b):
    return jnp.dot(a, b, preferred_element_type=F32)


def _dot_nt(a, b):
    return lax.dot_general(a, b, (((1,), (1,)), ((), ())), preferred_element_type=F32)


def _dot_tn(a, b):
    return lax.dot_general(a, b, (((0,), (0,)), ((), ())), preferred_element_type=F32)


def _split3(x):
    hi = x.astype(BF16)
    r1 = x - hi.astype(F32)
    mid = r1.astype(BF16)
    lo = (r1 - mid.astype(F32)).astype(BF16)
    return hi, mid, lo


def _sigmoid(z):
    return 1.0 / (1.0 + jnp.exp(-z))


def _head_sum(x):
    cols = []
    for c in range(x.shape[1] // 128):
        xc = x[:, c * 128:(c + 1) * 128]
        first = lax.broadcasted_iota(I32, xc.shape, 1) < HEAD
        s0 = jnp.sum(jnp.where(first, xc, 0.0), axis=-1, keepdims=True)
        s1 = jnp.sum(jnp.where(first, 0.0, xc), axis=-1, keepdims=True)
        cols.append(jnp.where(first, s0, s1))
    return cols[0] if len(cols) == 1 else jnp.concatenate(cols, axis=1)


def _store_token_major(ref, x):
    rows, d = x.shape
    for s in range(d // 128):
        ref[pl.ds(s, rows, stride=d // 128), :] = x[:, s * 128:(s + 1) * 128]


def _load_token_major(ref, rows, d, s):
    return ref[pl.ds(s, rows, stride=d // 128), :]


def _in_proj_kernel(x_ref, nw_ref, wrw_ref, wqkv_ref, wf_ref, prw_ref, pqkv_ref, pf_ref):
    x = x_ref[...]
    h = x * lax.rsqrt(jnp.mean(x * x, axis=-1, keepdims=True) + NORM_EPS) * nw_ref[...]
    hb = h.astype(BF16)
    prw_ref[...] = _dot(hb, wrw_ref[...])
    pqkv_ref[...] = _dot(hb, wqkv_ref[...])
    pf_ref[...] = _dot(hb, wf_ref[...])


def _in_proj(x2, norm_w, w_rw, w_qkv, w_f):
    M, D = x2.shape
    tm = min(512, M)
    n_rw, n_qkv, n_f = w_rw.shape[1], w_qkv.shape[1], w_f.shape[1]
    full = lambda shape: pl.BlockSpec(shape, lambda i: (0, 0))
    return pl.pallas_call(
        _in_proj_kernel,
        grid=(M // tm,),
        in_specs=[pl.BlockSpec((tm, D), lambda i: (i, 0)), full((1, D)),
                  full((D, n_rw)), full((D, n_qkv)), full((D, n_f))],
        out_specs=[pl.BlockSpec((tm, n_rw), lambda i: (i, 0)),
                   pl.BlockSpec((tm, n_qkv), lambda i: (i, 0)),
                   pl.BlockSpec((tm, n_f), lambda i: (i, 0))],
        out_shape=[jax.ShapeDtypeStruct((M, n_rw), F32),
                   jax.ShapeDtypeStruct((M, n_qkv), F32),
                   jax.ShapeDtypeStruct((M, n_f), F32)],
        compiler_params=_cparams(("parallel",)),
        name="in_proj",
    )(x2, norm_w, w_rw, w_qkv, w_f)


def _rw_prep_kernel(n_heads, p_ref, pprev_ref, mu_ref, w0_ref, w2_ref, a0_ref, a2_ref, g2_ref,
                    kk_ref, ka_ref, rk_ref,
                    rt_ref, at_ref, bt_ref, kt_ref, bb_ref, kb_ref, v_ref, dl_ref, bonus_ref, g_ref):
    W = n_heads * HEAD
    j = pl.program_id(1)
    p = p_ref[0]
    tt = p.shape[0]
    last_prev = jnp.where(j > 0, pprev_ref[0, 7:8, :], 0.0)
    row = lax.broadcasted_iota(I32, p.shape, 0)
    prev = jnp.where(row == 0, last_prev, pltpu.roll(p, 1, axis=0))
    ps = p + (prev - p) * mu_ref[...]
    r, k, v = ps[:, 0:W], ps[:, W:2 * W], ps[:, 2 * W:3 * W]
    o = 3 * W
    pw, pa, pg = ps[:, o:o + 64], ps[:, o + 64:o + 128], ps[:, o + 128:o + 256]

    z = w0_ref[...] + _dot(jnp.tanh(pw).astype(BF16), w2_ref[...])
    lw = (-np.exp(-0.5)).astype(np.float32) * _sigmoid(z)
    a_sig = _sigmoid(a0_ref[...] + _dot(pa.astype(BF16), a2_ref[...]))
    g_ref[0] = _dot(_sigmoid(pg).astype(BF16), g2_ref[...])

    kk = k * kk_ref[...]
    kk = kk / jnp.maximum(jnp.sqrt(_head_sum(kk * kk)), 1e-12)
    km = k * (1.0 + (a_sig - 1.0) * ka_ref[...])
    a_vec = -kk
    b_vec = kk * a_sig
    bonus_ref[0] = _head_sum(r * km * rk_ref[...]) * v
    v_ref[0] = v.astype(BF16)

    ti = lax.broadcasted_iota(I32, (tt, tt), 0)
    si = lax.broadcasted_iota(I32, (tt, tt), 1)
    same = (ti // CHUNK) == (si // CHUNK)
    tri = jnp.where(same & (ti >= si), 1.0, 0.0).astype(BF16)
    ones = jnp.where(same, 1.0, 0.0).astype(BF16)
    nck = tt // CHUNK
    ci = lax.broadcasted_iota(I32, (nck, tt), 0)
    cs = lax.broadcasted_iota(I32, (nck, tt), 1)
    sel = jnp.where(ci == cs // CHUNK, 1.0, 0.0).astype(BF16)
    hi, mid, lo = _split3(lw)
    cum = _dot(tri, hi) + _dot(tri, mid) + _dot(tri, lo)
    tot = _dot(ones, hi) + _dot(ones, mid) + _dot(ones, lo)
    dl = jnp.exp(_dot(sel, hi) + _dot(sel, mid) + _dot(sel, lo))
    for ck in range(nck):
        dl_ref[0, ck] = dl[ck:ck + 1, :]

    e_in = jnp.exp(cum)
    e_ex = jnp.exp(cum - lw)
    e_inv = jnp.exp(-cum)
    e_bar = jnp.exp(tot - cum)
    rt_ref[0] = (r * e_in).astype(BF16)
    at_ref[0] = (a_vec * e_ex).astype(BF16)
    bt_ref[0] = (b_vec * e_inv).astype(BF16)
    kt_ref[0] = (km * e_inv).astype(BF16)
    bb_ref[0] = (b_vec * e_bar).astype(BF16)
    kb_ref[0] = (km * e_bar).astype(BF16)


def _rw_prep(p_rw, mu, w0, w2, a0, a2, g2, k_k, k_a, r_k, n_heads):
    B, T, P = p_rw.shape
    W = n_heads * HEAD
    tt = min(512, T)
    row = lambda a: pl.BlockSpec(a.shape, lambda b, j: (0, 0))
    tile = lambda w: pl.BlockSpec((1, tt, w), lambda b, j: (b, j, 0))
    bf = jax.ShapeDtypeStruct((B, T, W), BF16)
    f32 = jax.ShapeDtypeStruct((B, T, W), F32)
    return pl.pallas_call(
        functools.partial(_rw_prep_kernel, n_heads),
        grid=(B, T // tt),
        in_specs=[tile(P),
                  pl.BlockSpec((1, 8, P), lambda b, j: (b, jnp.maximum(j * (tt // 8) - 1, 0), 0)),
                  row(mu), row(w0), row(w2), row(a0), row(a2), row(g2), row(k_k), row(k_a), row(r_k)],
        out_specs=[tile(W)] * 7 + [pl.BlockSpec((1, tt // CHUNK, 1, W), lambda b, j: (b, j, 0, 0)), tile(W), tile(W)],
        out_shape=[bf] * 7 + [jax.ShapeDtypeStruct((B, T // CHUNK, 1, W), F32), f32, f32],
        compiler_params=_cparams(("parallel", "parallel")),
        name="rw_prep",
    )(p_rw, p_rw, mu, w0, w2, a0, a2, g2, k_k, k_a, r_k)


def _rw_scan_kernel(rt_ref, at_ref, bt_ref, kt_ref, bb_ref, kb_ref, v_ref, dl_ref,
                    bonus_ref, g_ref, lnw_ref, lnb_ref, y_ref, s_ref):
    C = CHUNK
    nb, tt, W = rt_ref.shape
    chains = [(b, g) for b in range(nb) for g in range(W // GW)]

    @pl.when(pl.program_id(1) == 0)
    def _():
        s_ref[...] = jnp.zeros_like(s_ref)

    ri = lax.broadcasted_iota(I32, (GW, GW), 0)
    ci = lax.broadcasted_iota(I32, (GW, GW), 1)
    tr, tc = ri & (C - 1), ci & (C - 1)
    strict, incl, eye = tr > tc, tr >= tc, ri == ci
    lane_head = lax.broadcasted_iota(I32, (C, GW), 1) // HEAD

    def bd(x):
        return jnp.concatenate([jnp.where(lane_head == h, x, jnp.zeros_like(x)) for h in range(GROUP)], axis=0)

    bf = lambda xs: [x.astype(BF16) for x in xs]

    def chunk(c, carry):
        rows = pl.ds(pl.multiple_of(c * C, C), C)
        ld = lambda ref: [bd(ref[b, rows, g * GW:(g + 1) * GW]) for b, g in chains]
        Rt, At, Bt, Kt, Bb, Kb, V = (ld(r) for r in (rt_ref, at_ref, bt_ref, kt_ref, bb_ref, kb_ref, v_ref))
        BK = [jnp.concatenate([b, k], axis=0) for b, k in zip(Bt, Kt)]
        XA = [_dot_nt(a, bk) for a, bk in zip(At, BK)]
        XR = [_dot_nt(r, bk) for r, bk in zip(Rt, BK)]
        Lab = [jnp.where(strict, x[:, :GW], 0.0) for x in XA]
        Lak = bf([jnp.where(strict, x[:, GW:], 0.0) for x in XA])
        Mrb = bf([jnp.where(incl, x[:, :GW], 0.0) for x in XR])
        Mrk = bf([jnp.where(incl, x[:, GW:], 0.0) for x in XR])
        P = [jnp.where(eye, 1.0, l) for l in Lab]
        Lp = Lab
        for _ in range(5):
            Lpb = bf(Lp)
            Lp = [_dot(x, x) for x in Lpb]
            P = [p + _dot(pb, lb) for p, pb, lb in zip(P, bf(P), bf(Lp))]
        Pb = bf(P)
        Wk = bf([_dot(l, v) for l, v in zip(Lak, V)])
        AU = bf([_dot(p, jnp.concatenate([a, w], axis=1)) for p, a, w in zip(Pb, At, Wk)])
        Ahb, U0b = [x[:, :GW] for x in AU], [x[:, GW:] for x in AU]
        MAU = [_dot(m, x) for m, x in zip(Mrb, AU)]
        Rh = bf([r.astype(F32) + x[:, :GW] for r, x in zip(Rt, MAU)])
        Y0 = [x[:, GW:] + _dot(mk, v) for x, mk, v in zip(MAU, Mrk, V)]
        GT = bf([_dot_tn(a, b) for a, b in zip(Ahb, Bb)])
        HT = [_dot_tn(u, b) + _dot_tn(v, k) for u, b, v, k in zip(U0b, Bb, V, Kb)]
        S0 = [s_ref[i] for i in range(len(chains))]
        S0b = bf(S0)
        Y = [_dot_nt(r, s) + y0 for r, s, y0 in zip(Rh, S0b, Y0)]
        for i, (b, g) in enumerate(chains):
            ls = slice(g * GW, (g + 1) * GW)
            s_ref[i] = S0[i] * dl_ref[b, c, :, ls] + _dot(S0b[i], GT[i]) + HT[i]
            y = sum(Y[i][h * C:(h + 1) * C] for h in range(GROUP))
            mean = _head_sum(y) * (1.0 / HEAD)
            yc = y - mean
            var = _head_sum(yc * yc) * (1.0 / HEAD)
            yn = yc * lax.rsqrt(var + GN_EPS) * lnw_ref[:, ls] + lnb_ref[:, ls]
            y_ref[b, rows, ls] = (yn + bonus_ref[b, rows, ls]) * g_ref[b, rows, ls]
        return carry

    lax.fori_loop(0, tt // C, chunk, 0)


def _rw_scan(rt, at, bt, kt, bb, kb, v, dl, bonus, g, ln_w, ln_b):
    B, T, W = rt.shape
    nb = min(SCAN_BATCH, B)
    tt = min(256, T)
    tile = pl.BlockSpec((nb, tt, W), lambda b, j: (b, j, 0))
    row = pl.BlockSpec((1, W), lambda b, j: (0, 0))
    return pl.pallas_call(
        _rw_scan_kernel,
        grid=(B // nb, T // tt),
        in_specs=[tile] * 7 + [pl.BlockSpec((nb, tt // CHUNK, 1, W), lambda b, j: (b, j, 0, 0)), tile, tile, row, row],
        out_specs=tile,
        out_shape=jax.ShapeDtypeStruct((B, T, W), F32),
        scratch_shapes=[pltpu.VMEM((nb * (W // GW), GW, GW), F32)],
        compiler_params=_cparams(("parallel", "arbitrary")),
        name="rw_scan",
    )(rt, at, bt, kt, bb, kb, v, dl, bonus, g, ln_w, ln_b)


def _fox_prep_kernel(n_heads, qkv_ref, f_ref, bf_ref, qnw_ref, knw_ref,
                     q_ref, k_ref, vt_ref, carry_ref):
    W = n_heads * HEAD
    tt = qkv_ref.shape[1]

    @pl.when(pl.program_id(1) == 0)
    def _():
        carry_ref[...] = jnp.zeros_like(carry_ref)

    qkv = qkv_ref[0]
    q, k, v = qkv[:, 0:W], qkv[:, W:2 * W], qkv[:, 2 * W:3 * W]
    inv_n = 1.0 / HEAD
    qn = q * lax.rsqrt(_head_sum(q * q) * inv_n + NORM_EPS) * (qnw_ref[...] * (HEAD ** -0.5 * LOG2E))
    kn = k * lax.rsqrt(_head_sum(k * k) * inv_n + NORM_EPS) * knw_ref[...]
    for p in range(n_heads // 2):
        vt_ref[0, p, 0] = v[:, p * 128:(p + 1) * 128].T.astype(BF16)

    zf = f_ref[0] + bf_ref[...]
    logf = jnp.minimum(zf, 0.0) - jnp.log(1.0 + jnp.exp(-jnp.abs(zf)))
    ti = lax.broadcasted_iota(I32, (tt, tt), 0)
    si = lax.broadcasted_iota(I32, (tt, tt), 1)
    tri = jnp.where(ti >= si, 1.0, 0.0).astype(BF16)
    hi, mid, lo = _split3(logf)
    c = carry_ref[...] + _dot(tri, hi) + _dot(tri, mid) + _dot(tri, lo)
    carry_ref[...] = c[tt - 1:tt, :]
    c_hi, c_mid, c_lo = _split3(c * (-LOG2E))
    lane = lax.broadcasted_iota(I32, (tt, 128), 1)
    c3 = jnp.where(lane < 8, c_hi.astype(F32),
                   jnp.where(lane < 16, pltpu.roll(c_mid.astype(F32), 8, axis=1),
                             pltpu.roll(c_lo.astype(F32), 16, axis=1))).astype(BF16)
    src = lax.broadcasted_iota(I32, (128, 128), 0)
    dst = lax.broadcasted_iota(I32, (128, 128), 1)
    ones = jnp.where((lane >= HEAD) & (lane < HEAD + 3), 1.0, 0.0)
    for h in range(n_heads):
        col = slice((h // 2) * 128, (h // 2 + 1) * 128)
        qh, kh = qn[:, col], kn[:, col]
        if h % 2:
            qh, kh = pltpu.roll(qh, HEAD, axis=1), pltpu.roll(kh, HEAD, axis=1)
        sel = jnp.where((dst >= HEAD) & (dst < HEAD + 3) & (src == h + 8 * (dst - HEAD)), 1.0, 0.0).astype(BF16)
        q_ref[0, h] = jnp.where(lane < HEAD, qh, ones).astype(BF16)
        k_ref[0, h] = jnp.where(lane < HEAD, kh, _dot(c3, sel)).astype(BF16)


def _fox_prep(qkv, f, b_f, qn_w, kn_w, n_heads):
    B, T, _ = qkv.shape
    W = n_heads * HEAD
    tt = min(256, T)
    row = lambda a: pl.BlockSpec(a.shape, lambda b, j: (0, 0))
    aug = jax.ShapeDtypeStruct((B, n_heads, T, 128), BF16)
    aug_spec = pl.BlockSpec((1, n_heads, tt, 128), lambda b, j: (b, 0, j, 0))
    return pl.pallas_call(
        functools.partial(_fox_prep_kernel, n_heads),
        grid=(B, T // tt),
        in_specs=[pl.BlockSpec((1, tt, 3 * W), lambda b, j: (b, j, 0)),
                  pl.BlockSpec((1, tt, 128), lambda b, j: (b, j, 0)),
                  row(b_f), row(qn_w), row(kn_w)],
        out_specs=[aug_spec, aug_spec,
                   pl.BlockSpec((1, n_heads // 2, 1, 128, tt), lambda b, j: (b, 0, j, 0, 0))],
        out_shape=[aug, aug, jax.ShapeDtypeStruct((B, n_heads // 2, T // tt, 128, tt), BF16)],
        scratch_shapes=[pltpu.VMEM((1, 128), F32)],
        compiler_params=_cparams(("parallel", "arbitrary")),
        name="fox_prep",
    )(qkv, f, b_f, qn_w, kn_w)


def _fox_attn_kernel(q_ref, k_ref, vt_ref, o_ref, sa_ref, sb_ref):
    i = pl.program_id(2)
    tq = q_ref.shape[2]
    sub = vt_ref.shape[4]
    q = [q_ref[0, hh] for hh in range(2)]
    ki = lax.broadcasted_iota(I32, (tq, tq), 0)
    qi = lax.broadcasted_iota(I32, (tq, tq), 1)

    def scores(j, s_ref, diag=False):
        k0 = pl.multiple_of(j * tq, tq)
        for hh in range(2):
            s = _dot_nt(k_ref[0, hh, pl.ds(k0, tq), :], q[hh])
            s_ref[hh] = jnp.where(ki <= qi, s, -jnp.inf) if diag else s

    def update(j, s_ref, carry, causal=False):
        read = lambda hh: jnp.where(ki <= qi, s_ref[hh], -jnp.inf) if causal else s_ref[hh]
        out = []
        for hh in range(2):
            m, l, acc = carry[hh]
            m_new = jnp.maximum(m, jnp.max(read(hh), axis=0, keepdims=True))
            alpha = jnp.exp2(m - m_new)
            p = jnp.exp2(read(hh) - m_new)
            l = alpha * l + jnp.sum(p, axis=0, keepdims=True)
            p = p.astype(BF16)
            pv = sum(_dot(vt_ref[0, 0, j * (tq // sub) + u], p[u * sub:(u + 1) * sub]) for u in range(tq // sub))
            out.append((m_new, l, alpha * acc + pv))
        return tuple(out)

    def two_blocks(jj, carry):
        scores(2 * jj + 1, sb_ref)
        carry = update(2 * jj, sa_ref, carry)
        scores(jnp.minimum(2 * jj + 2, i), sa_ref)
        return update(2 * jj + 1, sb_ref, carry)

    init = tuple((jnp.full((1, tq), NEG_BIG, F32), jnp.zeros((1, tq), F32), jnp.zeros((128, tq), F32))
                 for _ in range(2))
    scores(0, sa_ref)
    carry = lax.fori_loop(0, i // 2, two_blocks, init)

    def even_tail(_, c):
        return update(i, sa_ref, c, causal=True)

    def odd_tail(_, c):
        scores(i, sb_ref, diag=True)
        return update(i, sb_ref, update(i - 1, sa_ref, c))

    carry = lax.fori_loop(0, 1 - (i & 1), even_tail, carry)
    (_, l0, acc0), (_, l1, acc1) = lax.fori_loop(0, i & 1, odd_tail, carry)
    row = lax.broadcasted_iota(I32, (128, tq), 0)
    o_ref[0] = jnp.where(row < HEAD, acc0 / l0, acc1 / l1).T


def _fox_attn(q_aug, k_aug, vt):
    B, H, T, _ = q_aug.shape
    tq = min(512, T)
    n_sub, sub = vt.shape[2], vt.shape[4]
    return pl.pallas_call(
        _fox_attn_kernel,
        grid=(B, H // 2, T // tq),
        in_specs=[pl.BlockSpec((1, 2, tq, 128), lambda b, p, i: (b, p, i, 0)),
                  pl.BlockSpec((1, 2, T, 128), lambda b, p, i: (b, p, 0, 0)),
                  pl.BlockSpec((1, 1, n_sub, 128, sub), lambda b, p, i: (b, p, 0, 0, 0))],
        out_specs=pl.BlockSpec((1, tq, 128), lambda b, p, i: (b, i, p)),
        out_shape=jax.ShapeDtypeStruct((B, T, H * HEAD), F32),
        scratch_shapes=[pltpu.VMEM((2, tq, tq), F32), pltpu.VMEM((2, tq, tq), F32)],
        compiler_params=_cparams(("parallel", "parallel", "arbitrary")),
        name="fox_attn",
    )(q_aug, k_aug, vt)


def _out_route_kernel(yrw_ref, yfox_ref, x_ref, wo_rw_ref, wo_fox_ref, nw_ref, wr_hi_ref, wr_lo_ref, br_ref,
                      x1_ref, h2_ref, eid_ref, gate_ref):
    x1 = (x_ref[...] + _dot(yrw_ref[...].astype(BF16), wo_rw_ref[...])
          + _dot(yfox_ref[...].astype(BF16), wo_fox_ref[...]))
    x1_ref[...] = x1
    h2 = x1 * lax.rsqrt(jnp.mean(x1 * x1, axis=-1, keepdims=True) + NORM_EPS) * nw_ref[...]
    _store_token_major(h2_ref, h2)
    h_hi = h2.astype(BF16)
    h_lo = (h2 - h_hi.astype(F32)).astype(BF16)
    logits = (_dot(h_hi, wr_hi_ref[...]) + _dot(h_hi, wr_lo_ref[...]) + _dot(h_lo, wr_hi_ref[...])) + br_ref[...]

    lane_i = lax.broadcasted_iota(I32, logits.shape, 1)
    lane = lane_i.astype(F32)
    first = lambda mask: jnp.min(jnp.where(mask, lane, 1e9), axis=-1, keepdims=True)
    gl = jnp.where(lane_i < N_GROUPS, logits, -jnp.inf)
    gmax = jnp.max(gl, axis=-1, keepdims=True)
    g_sel = first(gl == gmax)
    g_gate = 1.0 / jnp.sum(jnp.exp(gl - gmax), axis=-1, keepdims=True)
    e_lane = lane_i - N_GROUPS
    lane_grp = jnp.right_shift(e_lane, 3).astype(F32)
    in_grp = (e_lane >= 0) & (e_lane < N_EXPERTS) & (lane_grp == g_sel)
    el = jnp.where(in_grp, logits, -jnp.inf)
    m1 = jnp.max(el, axis=-1, keepdims=True)
    i1 = first(el == m1)
    el2 = jnp.where(lane == i1, -jnp.inf, el)
    m2 = jnp.max(el2, axis=-1, keepdims=True)
    i2 = first(el2 == m2)
    e2 = jnp.exp(m2 - m1)
    g1 = g_gate / (1.0 + e2)
    g2 = g_gate * e2 / (1.0 + e2)
    eid_ref[...] = jnp.where(lane_i == 0, i1 - N_GROUPS, jnp.where(lane_i == 1, i2 - N_GROUPS, 0.0)).astype(I32)
    gate_ref[...] = jnp.where(lane_i == 0, g1, jnp.where(lane_i == 1, g2, 0.0))


def _out_route(y_rw, y_fox, x2, wo_rw, wo_fox, norm_w, wr_hi, wr_lo, b_r):
    M, D = x2.shape
    W = y_rw.shape[1]
    tm = min(512, M)
    full = lambda a: pl.BlockSpec(a.shape, lambda i: (0, 0))
    tile = lambda w: pl.BlockSpec((tm, w), lambda i: (i, 0))
    return pl.pallas_call(
        _out_route_kernel,
        grid=(M // tm,),
        in_specs=[tile(W), tile(W), tile(D), full(wo_rw), full(wo_fox), full(norm_w),
                  full(wr_hi), full(wr_lo), full(b_r)],
        out_specs=[tile(D), pl.BlockSpec((tm * (D // 128), 128), lambda i: (i, 0)),
                   tile(ROUTER_LANES), tile(ROUTER_LANES)],
        out_shape=[jax.ShapeDtypeStruct((M, D), F32), jax.ShapeDtypeStruct((M * (D // 128), 128), F32),
                   jax.ShapeDtypeStruct((M, ROUTER_LANES), I32), jax.ShapeDtypeStruct((M, ROUTER_LANES), F32)],
        compiler_params=_cparams(("parallel",)),
        name="out_route",
    )(y_rw, y_fox, x2, wo_rw, wo_fox, norm_w, wr_hi, wr_lo, b_r)


def _rank_kernel(eid_ref, rank_ref, cnt_ref, carry_ref):
    @pl.when(pl.program_id(0) == 0)
    def _():
        carry_ref[...] = jnp.zeros_like(carry_ref)

    eid = eid_ref[...].astype(F32)
    tm = eid.shape[0]
    lane = lax.broadcasted_iota(I32, eid.shape, 1)
    lane_f = lane.astype(F32)
    pick = lambda l: jnp.sum(jnp.where(lane == l, eid, 0.0), axis=-1, keepdims=True)
    e0, e1 = pick(0), pick(1)
    oh0 = (lane_f == e0).astype(F32)
    oh1 = (lane_f == e1).astype(F32)
    both = oh0 + oh1
    ri = lax.broadcasted_iota(I32, (tm, tm), 0)
    ci = lax.broadcasted_iota(I32, (tm, tm), 1)
    before = _dot(jnp.where(ri > ci, 1.0, 0.0).astype(BF16), both.astype(BF16)) + carry_ref[...]
    r0 = jnp.sum(oh0 * before, axis=-1, keepdims=True)
    r1 = jnp.sum(oh1 * (before + oh0), axis=-1, keepdims=True)
    rank_ref[...] = jnp.where(lane == 0, r0, jnp.where(lane == 1, r1, 0.0)).astype(I32)
    total = carry_ref[...] + jnp.sum(both, axis=0, keepdims=True)
    carry_ref[...] = total
    cnt_ref[...] = jnp.broadcast_to(total, cnt_ref.shape).astype(I32)


def _rank(eid):
    M = eid.shape[0]
    tm = min(512, M)
    return pl.pallas_call(
        _rank_kernel,
        grid=(M // tm,),
        in_specs=[pl.BlockSpec((tm, ROUTER_LANES), lambda i: (i, 0))],
        out_specs=[pl.BlockSpec((tm, ROUTER_LANES), lambda i: (i, 0)),
                   pl.BlockSpec((8, ROUTER_LANES), lambda i: (0, 0))],
        out_shape=[jax.ShapeDtypeStruct((M, ROUTER_LANES), I32), jax.ShapeDtypeStruct((8, ROUTER_LANES), I32)],
        scratch_shapes=[pltpu.VMEM((1, ROUTER_LANES), F32)],
        compiler_params=_cparams(("arbitrary",)),
        name="rank",
    )(eid)


def _dest_kernel(eid_ref, rank_ref, start_ref, dest_ref):
    eid = eid_ref[...].astype(F32)
    rank = rank_ref[...].astype(F32)
    lane = lax.broadcasted_iota(I32, eid.shape, 1)
    lane_f = lane.astype(F32)
    pick = lambda x, l: jnp.sum(jnp.where(lane == l, x, 0.0), axis=-1, keepdims=True)
    base = lambda e: jnp.sum(jnp.where(lane_f == e, start_ref[...], 0.0), axis=-1, keepdims=True)
    d0 = base(pick(eid, 0)) + pick(rank, 0)
    d1 = base(pick(eid, 1)) + pick(rank, 1)
    dest_ref[...] = jnp.where(lane == 0, d0, jnp.where(lane == 1, d1, 0.0)).astype(I32)


def _dest(eid, rank, pad_start_row):
    M = eid.shape[0]
    tm = min(1024, M)
    tile = pl.BlockSpec((tm, ROUTER_LANES), lambda i: (i, 0))
    return pl.pallas_call(
        _dest_kernel,
        grid=(M // tm,),
        in_specs=[tile, tile, pl.BlockSpec((1, ROUTER_LANES), lambda i: (0, 0))],
        out_specs=tile,
        out_shape=jax.ShapeDtypeStruct((M, ROUTER_LANES), I32),
        compiler_params=_cparams(("parallel",)),
        name="dest",
    )(eid, rank, pad_start_row)


def _invert_kernel(dest_ref, cnt_ref, start_ref, end_ref, tok_ref):
    def clear(r, carry):
        tok_ref[r] = 0
        return carry

    def clear_expert(e, carry):
        lax.fori_loop(start_ref[e] + cnt_ref[e], end_ref[e], clear, 0)
        return carry

    def put(a, carry):
        tok_ref[dest_ref[a]] = lax.shift_right_logical(a, 1)
        return carry

    lax.fori_loop(0, cnt_ref.shape[0], clear_expert, 0)
    lax.fori_loop(end_ref[end_ref.shape[0] - 1], tok_ref.shape[0], clear, 0)
    lax.fori_loop(0, dest_ref.shape[0], put, 0, unroll=8)


def _invert(dest_flat, counts, pad_start, pad_end, n_rows):
    smem = pl.BlockSpec(memory_space=pltpu.SMEM)
    return pl.pallas_call(
        _invert_kernel,
        in_specs=[smem] * 4,
        out_specs=smem,
        out_shape=jax.ShapeDtypeStruct((n_rows,), I32),
        name="invert",
    )(dest_flat, counts, pad_start, pad_end)


def _experts_kernel(be_ref, nused_ref, tok_ref, h_hbm, wg_ref, wu_ref, wd_ref, yb_ref,
                    xbuf, sem, wg_bf, wu_bf, wd_bf):
    i = pl.program_id(0)
    D, F = wg_bf.shape
    nt = D // 128
    rb = yb_ref.shape[0] // nt
    n_used = nused_ref[0]
    slot = i & 1

    def copy(blk, r, s):
        src = pl.multiple_of(tok_ref[blk * rb + r] * nt, nt)
        return pltpu.make_async_copy(h_hbm.at[pl.ds(src, nt)], xbuf.at[s, pl.ds(r * nt, nt)], sem.at[s])

    def wait_block(blk, s):
        def body(r, carry):
            copy(blk, r, s).wait()
            return carry
        lax.fori_loop(0, rb, body, 0, unroll=8)

    @pl.when(i == 0)
    def _():
        def body(r, carry):
            copy(0, r, 0).start()
            return carry
        lax.fori_loop(0, rb, body, 0, unroll=8)

    prev = be_ref[jnp.maximum(i - 1, 0)]

    @pl.when((i == 0) | (be_ref[i] != prev))
    def _():
        wg_bf[...] = wg_ref[0].astype(BF16)
        wu_bf[...] = wu_ref[0].astype(BF16)
        wd_bf[...] = wd_ref[0].astype(BF16)

    @pl.when(i < n_used)
    def _():
        wait_block(i, slot)

        def start_next(r, carry):
            copy(i + 1, 2 * r, 1 - slot).start(priority=0)
            copy(i + 1, 2 * r + 1, 1 - slot).start(priority=1)
            return carry
        lax.fori_loop(0, rb // 2, start_next, 0, unroll=4)

        xb = jnp.concatenate([_load_token_major(xbuf.at[slot], rb, D, s).astype(BF16) for s in range(nt)], axis=1)
        gate = _dot(xb, wg_bf[...])
        up = _dot(xb, wu_bf[...])
        hid = gate * _sigmoid(gate) * up
        _store_token_major(yb_ref, _dot(hid.astype(BF16), wd_bf[...]))

    @pl.when(i == n_used)
    def _():
        wait_block(i, slot)

    @pl.when(i >= n_used)
    def _():
        yb_ref[...] = jnp.zeros_like(yb_ref)


def _experts(block_expert, n_used, tok, h2, w_gate, w_up, w_down, rb):
    E, D, F = w_gate.shape
    nt = D // 128
    nb = tok.shape[0] // rb
    grid_spec = pltpu.PrefetchScalarGridSpec(
        num_scalar_prefetch=3,
        grid=(nb,),
        in_specs=[pl.BlockSpec(memory_space=pl.ANY),
                  pl.BlockSpec((1, D, F), lambda i, be, nu, tk: (be[i], 0, 0)),
                  pl.BlockSpec((1, D, F), lambda i, be, nu, tk: (be[i], 0, 0)),
                  pl.BlockSpec((1, F, D), lambda i, be, nu, tk: (be[i], 0, 0))],
        out_specs=pl.BlockSpec((rb * nt, 128), lambda i, be, nu, tk: (i, 0)),
        scratch_shapes=[pltpu.VMEM((2, rb * nt, 128), F32), pltpu.SemaphoreType.DMA((2,)),
                        pltpu.VMEM((D, F), BF16), pltpu.VMEM((D, F), BF16), pltpu.VMEM((F, D), BF16)],
    )
    return pl.pallas_call(
        _experts_kernel,
        grid_spec=grid_spec,
        out_shape=jax.ShapeDtypeStruct((nb * rb * nt, 128), F32),
        compiler_params=_cparams(("arbitrary",)),
        name="experts",
    )(block_expert, n_used, tok, h2, w_gate, w_up, w_down)


def _combine_kernel(dest_ref, gate_ref, x1_ref, yb_hbm, out_ref, buf, sem):
    i = pl.program_id(0)
    tmc, D = x1_ref.shape
    nt = D // 128
    slot = i & 1

    def copy(tile, t, k, s):
        src = pl.multiple_of(dest_ref[2 * (tile * tmc + t) + k] * nt, nt)
        return pltpu.make_async_copy(yb_hbm.at[pl.ds(src, nt)], buf.at[s, k, pl.ds(t * nt, nt)], sem.at[s])

    def start_tile(tile, s):
        def body(t, carry):
            copy(tile, t, 0, s).start(priority=0)
            copy(tile, t, 1, s).start(priority=1)
            return carry
        lax.fori_loop(0, tmc, body, 0, unroll=8)

    @pl.when(i == 0)
    def _():
        start_tile(0, 0)

    @pl.when(i + 1 < pl.num_programs(0))
    def _():
        start_tile(i + 1, 1 - slot)

    def wait(t, carry):
        copy(i, t, 0, slot).wait()
        copy(i, t, 1, slot).wait()
        return carry

    lax.fori_loop(0, tmc, wait, 0, unroll=8)
    gate = gate_ref[...]
    g0, g1 = gate[:, 0:1], gate[:, 1:2]
    for s in range(nt):
        cols = slice(s * 128, (s + 1) * 128)
        out_ref[:, cols] = (x1_ref[:, cols] + g0 * _load_token_major(buf.at[slot, 0], tmc, D, s)
                            + g1 * _load_token_major(buf.at[slot, 1], tmc, D, s))


def _combine(dest_flat, gates, x1, yb):
    M, D = x1.shape
    tmc = min(256, M)
    grid_spec = pltpu.PrefetchScalarGridSpec(
        num_scalar_prefetch=1,
        grid=(M // tmc,),
        in_specs=[pl.BlockSpec((tmc, ROUTER_LANES), lambda i, d: (i, 0)),
                  pl.BlockSpec((tmc, D), lambda i, d: (i, 0)),
                  pl.BlockSpec(memory_space=pl.ANY)],
        out_specs=pl.BlockSpec((tmc, D), lambda i, d: (i, 0)),
        scratch_shapes=[pltpu.VMEM((2, 2, tmc * (D // 128), 128), F32), pltpu.SemaphoreType.DMA((2,))],
    )
    return pl.pallas_call(
        _combine_kernel,
        grid_spec=grid_spec,
        out_shape=jax.ShapeDtypeStruct((M, D), F32),
        compiler_params=_cparams(("arbitrary",)),
        name="combine",
    )(dest_flat, gates, x1, yb)


def _mixer(x2, B, T, norm_w, w_in, mu, w0, w2, a0, a2, g2, k_k, k_a, r_k, ln_w, ln_b, b_f, qn_w, kn_w):
    rw_heads = w0.shape[0] // HEAD
    fox_heads = b_f.shape[0]
    Wr, Wf = rw_heads * HEAD, fox_heads * HEAD
    lora = w2.shape[0] + a2.shape[0] + g2.shape[0]
    rw_cols = 3 * Wr + lora
    o_w, o_k, o_v, o_a = Wr, Wr + w2.shape[0], 2 * Wr + w2.shape[0], 3 * Wr + w2.shape[0]
    perm = np.concatenate([np.arange(0, Wr), np.arange(o_k, o_k + Wr), np.arange(o_v, o_v + Wr),
                           np.arange(o_w, o_w + w2.shape[0]), np.arange(o_a, rw_cols)])
    w_rw = w_in[:, :rw_cols][:, perm].astype(BF16)
    w_qkv = w_in[:, rw_cols:rw_cols + 3 * Wf].astype(BF16)
    w_f = jnp.pad(w_in[:, rw_cols + 3 * Wf:], ((0, 0), (0, 128 - fox_heads))).astype(BF16)
    p_rw, p_qkv, p_f = _in_proj(x2, norm_w[None, :], w_rw, w_qkv, w_f)

    row = lambda a: a.reshape(1, -1)
    ops = _rw_prep(p_rw.reshape(B, T, rw_cols), row(mu[perm]), row(w0), w2.astype(BF16), row(a0),
                   a2.astype(BF16), g2.astype(BF16), row(k_k), row(k_a), row(r_k), rw_heads)
    y_rw = _rw_scan(*ops, row(ln_w), row(ln_b))

    tile_w = lambda w: row(jnp.tile(w, fox_heads))
    q_aug, k_aug, vt = _fox_prep(p_qkv.reshape(B, T, 3 * Wf), p_f.reshape(B, T, 128),
                                 row(jnp.pad(b_f, (0, 128 - fox_heads))), tile_w(qn_w), tile_w(kn_w), fox_heads)
    y_fox = _fox_attn(q_aug, k_aug, vt)
    return y_rw.reshape(B * T, Wr), y_fox.reshape(B * T, Wf)


def _moe(y_rw, y_fox, x2, w_out, norm_w, rg_w, rg_b, re_w, re_b, w_gate, w_up, w_down):
    M, D = x2.shape
    Wr = y_rw.shape[1]
    pad = ROUTER_LANES - N_GROUPS - N_EXPERTS
    w_r = jnp.pad(jnp.concatenate([rg_w, re_w], axis=1), ((0, 0), (0, pad)))
    b_r = jnp.pad(jnp.concatenate([rg_b, re_b]), (0, pad))[None, :]
    wr_hi = w_r.astype(BF16)
    wr_lo = (w_r - wr_hi.astype(F32)).astype(BF16)
    x1, h2, eid, gates = _out_route(y_rw, y_fox, x2, w_out[:Wr].astype(BF16), w_out[Wr:].astype(BF16),
                                    norm_w[None, :], wr_hi, wr_lo, b_r)
    rank, counts = _rank(eid)

    rb = ROW_BLOCK
    counts = counts[0, :N_EXPERTS]
    padded = (counts + rb - 1) // rb * rb
    pad_end = jnp.cumsum(padded)
    pad_start = pad_end - padded
    n_blocks = (2 * M + N_EXPERTS * (rb - 1) + rb - 1) // rb + 1
    block_start = jnp.arange(n_blocks, dtype=I32) * rb
    block_expert = jnp.minimum(jnp.sum(pad_end[None, :] <= block_start[:, None], axis=1), N_EXPERTS - 1).astype(I32)
    n_used = (pad_end[-1:] // rb).astype(I32)
    start_row = jnp.pad(pad_start.astype(F32), (0, ROUTER_LANES - N_EXPERTS))[None, :]
    dest = _dest(eid, rank, start_row)[:, :2].reshape(-1)

    tok = _invert(dest, counts, pad_start.astype(I32), pad_end.astype(I32), n_blocks * rb)
    yb = _experts(block_expert, n_used, tok, h2, w_gate, w_up, w_down, rb)
    return _combine(dest, gates, x1, yb)


def kernel(x, norm_mix_w, w_in, mu_shift, rw_w0, rw_w2, rw_a0, rw_a2, rw_g2, rw_k_k, rw_k_a, rw_r_k, rw_ln_w, rw_ln_b, fox_b_f, fox_q_norm_w, fox_k_norm_w, w_out, norm_ffn_w, router_group_w, router_group_b, router_expert_w, router_expert_b, exp_w_gate, exp_w_up, exp_w_down):
    B, T, D = x.shape
    x2 = x.reshape(B * T, D)
    for l in range(w_in.shape[0]):
        y_rw, y_fox = _mixer(x2, B, T, norm_mix_w[l], w_in[l], mu_shift[l], rw_w0[l], rw_w2[l], rw_a0[l],
                             rw_a2[l], rw_g2[l], rw_k_k[l], rw_k_a[l], rw_r_k[l].reshape(-1), rw_ln_w[l],
                             rw_ln_b[l], fox_b_f[l], fox_q_norm_w[l], fox_k_norm_w[l])
        x2 = _moe(y_rw, y_fox, x2, w_out[l], norm_ffn_w[l], router_group_w[l], router_group_b[l],
                  router_expert_w[l], router_expert_b[l], exp_w_gate[l], exp_w_up[l], exp_w_down[l])
    return x2.reshape(B, T, D)
```

```python
import functools

import jax
import jax.numpy as jnp
import numpy as np
from jax import lax
from jax.experimental import pallas as pl
from jax.experimental.pallas import tpu as pltpu

F32, BF16, I32 = jnp.float32, jnp.bfloat16, jnp.int32

HEAD = 64
CHUNK = 64
GROUP = 2
GW = GROUP * HEAD
SCAN_BATCH = 4
ATTN_HEADS = 4
N_GROUPS = 8
EXPERTS_PER_GROUP = 8
N_EXPERTS = N_GROUPS * EXPERTS_PER_GROUP
ROUTER_LANES = 128
ROW_BLOCK = 256
NORM_EPS = 1e-6
GN_EPS = 64e-5
NEG_BIG = -1e30
LOG2E = 1.4426950408889634
VMEM_LIMIT = 56 * 1024 * 1024


def _cparams(sem):
    return pltpu.CompilerParams(dimension_semantics=sem, vmem_limit_bytes=VMEM_LIMIT)


def _dot(a, b):
    return jnp.dot(a, b, preferred_element_type=F32)


def _dot_nt(a, b):
    return lax.dot_general(a, b, (((1,), (1,)), ((), ())), preferred_element_type=F32)


def _dot_tn(a, b):
    return lax.dot_general(a, b, (((0,), (0,)), ((), ())), preferred_element_type=F32)


def _split3(x):
    hi = x.astype(BF16)
    r1 = x - hi.astype(F32)
    mid = r1.astype(BF16)
    lo = (r1 - mid.astype(F32)).astype(BF16)
    return hi, mid, lo


def _sigmoid(z):
    return 1.0 / (1.0 + jnp.exp(-z))


def _head_sum(x):
    cols = []
    for c in range(x.shape[1] // 128):
        xc = x[:, c * 128:(c + 1) * 128]
        first = lax.broadcasted_iota(I32, xc.shape, 1) < HEAD
        s0 = jnp.sum(jnp.where(first, xc, 0.0), axis=-1, keepdims=True)
        s1 = jnp.sum(jnp.where(first, 0.0, xc), axis=-1, keepdims=True)
        cols.append(jnp.where(first, s0, s1))
    return cols[0] if len(cols) == 1 else jnp.concatenate(cols, axis=1)


def _store_token_major(ref, x):
    rows, d = x.shape
    for s in range(d // 128):
        ref[pl.ds(s, rows, stride=d // 128), :] = x[:, s * 128:(s + 1) * 128]


def _load_token_major(ref, rows, d, s):
    return ref[pl.ds(s, rows, stride=d // 128), :]


def _in_proj_kernel(x_ref, nw_ref, wrw_ref, wqkv_ref, wf_ref, prw_ref, pqkv_ref, pf_ref):
    x = x_ref[...]
    h = x * lax.rsqrt(jnp.mean(x * x, axis=-1, keepdims=True) + NORM_EPS) * nw_ref[...]
    hb = h.astype(BF16)
    prw_ref[...] = _dot(hb, wrw_ref[...])
    pqkv_ref[...] = _dot(hb, wqkv_ref[...])
    pf_ref[...] = _dot(hb, wf_ref[...])


def _in_proj(x2, norm_w, w_rw, w_qkv, w_f):
    M, D = x2.shape
    tm = min(512, M)
    n_rw, n_qkv, n_f = w_rw.shape[1], w_qkv.shape[1], w_f.shape[1]
    full = lambda shape: pl.BlockSpec(shape, lambda i: (0, 0))
    return pl.pallas_call(
        _in_proj_kernel,
        grid=(M // tm,),
        in_specs=[pl.BlockSpec((tm, D), lambda i: (i, 0)), full((1, D)),
                  full((D, n_rw)), full((D, n_qkv)), full((D, n_f))],
        out_specs=[pl.BlockSpec((tm, n_rw), lambda i: (i, 0)),
                   pl.BlockSpec((tm, n_qkv), lambda i: (i, 0)),
                   pl.BlockSpec((tm, n_f), lambda i: (i, 0))],
        out_shape=[jax.ShapeDtypeStruct((M, n_rw), F32),
                   jax.ShapeDtypeStruct((M, n_qkv), F32),
                   jax.ShapeDtypeStruct((M, n_f), F32)],
        compiler_params=_cparams(("parallel",)),
        name="in_proj",
    )(x2, norm_w, w_rw, w_qkv, w_f)


def _rw_prep_kernel(n_heads, p_ref, pprev_ref, mu_ref, w0_ref, w2_ref, a0_ref, a2_ref, g2_ref,
                    kk_ref, ka_ref, rk_ref,
                    rt_ref, at_ref, bt_ref, kt_ref, bb_ref, kb_ref, v_ref, dl_ref, bonus_ref, g_ref):
    W = n_heads * HEAD
    j = pl.program_id(1)
    p = p_ref[0]
    tt = p.shape[0]
    last_prev = jnp.where(j > 0, pprev_ref[0, 7:8, :], 0.0)
    row = lax.broadcasted_iota(I32, p.shape, 0)
    prev = jnp.where(row == 0, last_prev, pltpu.roll(p, 1, axis=0))
    ps = p + (prev - p) * mu_ref[...]
    r, k, v = ps[:, 0:W], ps[:, W:2 * W], ps[:, 2 * W:3 * W]
    o = 3 * W
    pw, pa, pg = ps[:, o:o + 64], ps[:, o + 64:o + 128], ps[:, o + 128:o + 256]

    z = w0_ref[...] + _dot(jnp.tanh(pw).astype(BF16), w2_ref[...])
    lw = (-np.exp(-0.5)).astype(np.float32) * _sigmoid(z)
    a_sig = _sigmoid(a0_ref[...] + _dot(pa.astype(BF16), a2_ref[...]))
    g_ref[0] = _dot(_sigmoid(pg).astype(BF16), g2_ref[...])

    kk = k * kk_ref[...]
    kk = kk / jnp.maximum(jnp.sqrt(_head_sum(kk * kk)), 1e-12)
    km = k * (1.0 + (a_sig - 1.0) * ka_ref[...])
    a_vec = -kk
    b_vec = kk * a_sig
    bonus_ref[0] = _head_sum(r * km * rk_ref[...]) * v
    v_ref[0] = v.astype(BF16)

    ti = lax.broadcasted_iota(I32, (tt, tt), 0)
    si = lax.broadcasted_iota(I32, (tt, tt), 1)
    same = (ti // CHUNK) == (si // CHUNK)
    tri = jnp.where(same & (ti >= si), 1.0, 0.0).astype(BF16)
    ones = jnp.where(same, 1.0, 0.0).astype(BF16)
    nck = tt // CHUNK
    ci = lax.broadcasted_iota(I32, (nck, tt), 0)
    cs = lax.broadcasted_iota(I32, (nck, tt), 1)
    sel = jnp.where(ci == cs // CHUNK, 1.0, 0.0).astype(BF16)
    hi, mid, lo = _split3(lw)
    cum = _dot(tri, hi) + _dot(tri, mid) + _dot(tri, lo)
    tot = _dot(ones, hi) + _dot(ones, mid) + _dot(ones, lo)
    dl = jnp.exp(_dot(sel, hi) + _dot(sel, mid) + _dot(sel, lo))
    for ck in range(nck):
        dl_ref[0, ck] = dl[ck:ck + 1, :]

    e_in = jnp.exp(cum)
    e_ex = jnp.exp(cum - lw)
    e_inv = jnp.exp(-cum)
    e_bar = jnp.exp(tot - cum)
    rt_ref[0] = (r * e_in).astype(BF16)
    at_ref[0] = (a_vec * e_ex).astype(BF16)
    bt_ref[0] = (b_vec * e_inv).astype(BF16)
    kt_ref[0] = (km * e_inv).astype(BF16)
    bb_ref[0] = (b_vec * e_bar).astype(BF16)
    kb_ref[0] = (km * e_bar).astype(BF16)


def _rw_prep(p_rw, mu, w0, w2, a0, a2, g2, k_k, k_a, r_k, n_heads):
    B, T, P = p_rw.shape
    W = n_heads * HEAD
    tt = min(512, T)
    row = lambda a: pl.BlockSpec(a.shape, lambda b, j: (0, 0))
    tile = lambda w: pl.BlockSpec((1, tt, w), lambda b, j: (b, j, 0))
    bf = jax.ShapeDtypeStruct((B, T, W), BF16)
    f32 = jax.ShapeDtypeStruct((B, T, W), F32)
    return pl.pallas_call(
        functools.partial(_rw_prep_kernel, n_heads),
        grid=(B, T // tt),
        in_specs=[tile(P),
                  pl.BlockSpec((1, 8, P), lambda b, j: (b, jnp.maximum(j * (tt // 8) - 1, 0), 0)),
                  row(mu), row(w0), row(w2), row(a0), row(a2), row(g2), row(k_k), row(k_a), row(r_k)],
        out_specs=[tile(W)] * 7 + [pl.BlockSpec((1, tt // CHUNK, 1, W), lambda b, j: (b, j, 0, 0)), tile(W), tile(W)],
        out_shape=[bf] * 7 + [jax.ShapeDtypeStruct((B, T // CHUNK, 1, W), F32), f32, f32],
        compiler_params=_cparams(("parallel", "parallel")),
        name="rw_prep",
    )(p_rw, p_rw, mu, w0, w2, a0, a2, g2, k_k, k_a, r_k)


def _rw_scan_kernel(rt_ref, at_ref, bt_ref, kt_ref, bb_ref, kb_ref, v_ref, dl_ref,
                    bonus_ref, g_ref, lnw_ref, lnb_ref, y_ref, s_ref):
    C = CHUNK
    nb, tt, W = rt_ref.shape
    chains = [(b, g) for b in range(nb) for g in range(W // GW)]

    @pl.when(pl.program_id(1) == 0)
    def _():
        s_ref[...] = jnp.zeros_like(s_ref)

    ri = lax.broadcasted_iota(I32, (GW, GW), 0)
    ci = lax.broadcasted_iota(I32, (GW, GW), 1)
    tr, tc = ri & (C - 1), ci & (C - 1)
    strict, incl, eye = tr > tc, tr >= tc, ri == ci
    lane_head = lax.broadcasted_iota(I32, (C, GW), 1) // HEAD

    def bd(x):
        return jnp.concatenate([jnp.where(lane_head == h, x, jnp.zeros_like(x)) for h in range(GROUP)], axis=0)

    bf = lambda xs: [x.astype(BF16) for x in xs]

    def chunk(c, carry):
        rows = pl.ds(pl.multiple_of(c * C, C), C)
        ld = lambda ref: [bd(ref[b, rows, g * GW:(g + 1) * GW]) for b, g in chains]
        Rt, At, Bt, Kt, Bb, Kb, V = (ld(r) for r in (rt_ref, at_ref, bt_ref, kt_ref, bb_ref, kb_ref, v_ref))
        BK = [jnp.concatenate([b, k], axis=0) for b, k in zip(Bt, Kt)]
        XA = [_dot_nt(a, bk) for a, bk in zip(At, BK)]
        XR = [_dot_nt(r, bk) for r, bk in zip(Rt, BK)]
        Lab = [jnp.where(strict, x[:, :GW], 0.0) for x in XA]
        Lak = bf([jnp.where(strict, x[:, GW:], 0.0) for x in XA])
        Mrb = bf([jnp.where(incl, x[:, :GW], 0.0) for x in XR])
        Mrk = bf([jnp.where(incl, x[:, GW:], 0.0) for x in XR])
        P = [jnp.where(eye, 1.0, l) for l in Lab]
        Lp = Lab
        for _ in range(5):
            Lpb = bf(Lp)
            Lp = [_dot(x, x) for x in Lpb]
            P = [p + _dot(pb, lb) for p, pb, lb in zip(P, bf(P), bf(Lp))]
        Pb = bf(P)
        Wk = bf([_dot(l, v) for l, v in zip(Lak, V)])
        AU = bf([_dot(p, jnp.concatenate([a, w], axis=1)) for p, a, w in zip(Pb, At, Wk)])
        Ahb, U0b = [x[:, :GW] for x in AU], [x[:, GW:] for x in AU]
        MAU = [_dot(m, x) for m, x in zip(Mrb, AU)]
        Rh = bf([r.astype(F32) + x[:, :GW] for r, x in zip(Rt, MAU)])
        Y0 = [x[:, GW:] + _dot(mk, v) for x, mk, v in zip(MAU, Mrk, V)]
        GT = bf([_dot_tn(a, b) for a, b in zip(Ahb, Bb)])
        HT = [_dot_tn(u, b) + _dot_tn(v, k) for u, b, v, k in zip(U0b, Bb, V, Kb)]
        S0 = [s_ref[i] for i in range(len(chains))]
        S0b = bf(S0)
        Y = [_dot_nt(r, s) + y0 for r, s, y0 in zip(Rh, S0b, Y0)]
        for i, (b, g) in enumerate(chains):
            ls = slice(g * GW, (g + 1) * GW)
            s_ref[i] = S0[i] * dl_ref[b, c, :, ls] + _dot(S0b[i], GT[i]) + HT[i]
            y = sum(Y[i][h * C:(h + 1) * C] for h in range(GROUP))
            mean = _head_sum(y) * (1.0 / HEAD)
            yc = y - mean
            var = _head_sum(yc * yc) * (1.0 / HEAD)
            yn = yc * lax.rsqrt(var + GN_EPS) * lnw_ref[:, ls] + lnb_ref[:, ls]
            y_ref[b, rows, ls] = (yn + bonus_ref[b, rows, ls]) * g_ref[b, rows, ls]
        return carry

    lax.fori_loop(0, tt // C, chunk, 0)


def _rw_scan(rt, at, bt, kt, bb, kb, v, dl, bonus, g, ln_w, ln_b):
    B, T, W = rt.shape
    nb = min(SCAN_BATCH, B)
    tt = min(256, T)
    tile = pl.BlockSpec((nb, tt, W), lambda b, j: (b, j, 0))
    row = pl.BlockSpec((1, W), lambda b, j: (0, 0))
    return pl.pallas_call(
        _rw_scan_kernel,
        grid=(B // nb, T // tt),
        in_specs=[tile] * 7 + [pl.BlockSpec((nb, tt // CHUNK, 1, W), lambda b, j: (b, j, 0, 0)), tile, tile, row, row],
        out_specs=tile,
        out_shape=jax.ShapeDtypeStruct((B, T, W), F32),
        scratch_shapes=[pltpu.VMEM((nb * (W // GW), GW, GW), F32)],
        compiler_params=_cparams(("parallel", "arbitrary")),
        name="rw_scan",
    )(rt, at, bt, kt, bb, kb, v, dl, bonus, g, ln_w, ln_b)


def _fox_prep_kernel(n_heads, qkv_ref, f_ref, bf_ref, qnw_ref, knw_ref,
                     q_ref, k_ref, vt_ref, carry_ref):
    W = n_heads * HEAD
    tt = qkv_ref.shape[1]

    @pl.when(pl.program_id(1) == 0)
    def _():
        carry_ref[...] = jnp.zeros_like(carry_ref)

    qkv = qkv_ref[0]
    q, k, v = qkv[:, 0:W], qkv[:, W:2 * W], qkv[:, 2 * W:3 * W]
    inv_n = 1.0 / HEAD
    qn = q * lax.rsqrt(_head_sum(q * q) * inv_n + NORM_EPS) * (qnw_ref[...] * (HEAD ** -0.5 * LOG2E))
    kn = k * lax.rsqrt(_head_sum(k * k) * inv_n + NORM_EPS) * knw_ref[...]
    for p in range(n_heads // 2):
        vt_ref[0, p, 0] = v[:, p * 128:(p + 1) * 128].T.astype(BF16)

    zf = f_ref[0] + bf_ref[...]
    logf = jnp.minimum(zf, 0.0) - jnp.log(1.0 + jnp.exp(-jnp.abs(zf)))
    ti = lax.broadcasted_iota(I32, (tt, tt), 0)
    si = lax.broadcasted_iota(I32, (tt, tt), 1)
    tri = jnp.where(ti >= si, 1.0, 0.0).astype(BF16)
    hi, mid, lo = _split3(logf)
    c = carry_ref[...] + _dot(tri, hi) + _dot(tri, mid) + _dot(tri, lo)
    carry_ref[...] = c[tt - 1:tt, :]
    c_hi, c_mid, c_lo = _split3(c * (-LOG2E))
    lane = lax.broadcasted_iota(I32, (tt, 128), 1)
    c3 = jnp.where(lane < 8, c_hi.astype(F32),
                   jnp.where(lane < 16, pltpu.roll(c_mid.astype(F32), 8, axis=1),
                             pltpu.roll(c_lo.astype(F32), 16, axis=1))).astype(BF16)
    src = lax.broadcasted_iota(I32, (128, 128), 0)
    dst = lax.broadcasted_iota(I32, (128, 128), 1)
    ones = jnp.where((lane >= HEAD) & (lane < HEAD + 3), 1.0, 0.0)
    for h in range(n_heads):
        col = slice((h // 2) * 128, (h // 2 + 1) * 128)
        qh, kh = qn[:, col], kn[:, col]
        if h % 2:
            qh, kh = pltpu.roll(qh, HEAD, axis=1), pltpu.roll(kh, HEAD, axis=1)
        sel = jnp.where((dst >= HEAD) & (dst < HEAD + 3) & (src == h + 8 * (dst - HEAD)), 1.0, 0.0).astype(BF16)
        q_ref[0, h] = jnp.where(lane < HEAD, qh, ones).astype(BF16)
        k_ref[0, h] = jnp.where(lane < HEAD, kh, _dot(c3, sel)).astype(BF16)


def _fox_prep(qkv, f, b_f, qn_w, kn_w, n_heads):
    B, T, _ = qkv.shape
    W = n_heads * HEAD
    tt = min(256, T)
    row = lambda a: pl.BlockSpec(a.shape, lambda b, j: (0, 0))
    aug = jax.ShapeDtypeStruct((B, n_heads, T, 128), BF16)
    aug_spec = pl.BlockSpec((1, n_heads, tt, 128), lambda b, j: (b, 0, j, 0))
    return pl.pallas_call(
        functools.partial(_fox_prep_kernel, n_heads),
        grid=(B, T // tt),
        in_specs=[pl.BlockSpec((1, tt, 3 * W), lambda b, j: (b, j, 0)),
                  pl.BlockSpec((1, tt, 128), lambda b, j: (b, j, 0)),
                  row(b_f), row(qn_w), row(kn_w)],
        out_specs=[aug_spec, aug_spec,
                   pl.BlockSpec((1, n_heads // 2, 1, 128, tt), lambda b, j: (b, 0, j, 0, 0))],
        out_shape=[aug, aug, jax.ShapeDtypeStruct((B, n_heads // 2, T // tt, 128, tt), BF16)],
        scratch_shapes=[pltpu.VMEM((1, 128), F32)],
        compiler_params=_cparams(("parallel", "arbitrary")),
        name="fox_prep",
    )(qkv, f, b_f, qn_w, kn_w)


def _fox_attn_kernel(q_ref, k_ref, vt_ref, o_ref, sa_ref, sb_ref):
    i = pl.program_id(2)
    nh, tq = q_ref.shape[1], q_ref.shape[2]
    sub = vt_ref.shape[4]
    q = [q_ref[0, hh] for hh in range(nh)]
    ki = lax.broadcasted_iota(I32, (tq, tq), 0)
    qi = lax.broadcasted_iota(I32, (tq, tq), 1)

    def scores(j, s_ref, diag=False):
        k0 = pl.multiple_of(j * tq, tq)
        for hh in range(nh):
            s = _dot_nt(k_ref[0, hh, pl.ds(k0, tq), :], q[hh])
            s_ref[hh] = jnp.where(ki <= qi, s, -jnp.inf) if diag else s

    def update(j, s_ref, carry):
        out = []
        for hh in range(nh):
            m, l, acc = carry[hh]
            m_new = jnp.maximum(m, jnp.max(s_ref[hh], axis=0, keepdims=True))
            alpha = jnp.exp2(m - m_new)
            p = jnp.exp2(s_ref[hh] - m_new)
            l = alpha * l + jnp.sum(p, axis=0, keepdims=True)
            p = p.astype(BF16)
            pv = sum(_dot(vt_ref[0, hh // 2, j * (tq // sub) + u], p[u * sub:(u + 1) * sub])
                     for u in range(tq // sub))
            out.append((m_new, l, alpha * acc + pv))
        return tuple(out)

    def two_blocks(jj, carry):
        scores(2 * jj + 1, sb_ref)
        carry = update(2 * jj, sa_ref, carry)
        scores(jnp.minimum(2 * jj + 2, i), sa_ref)
        return update(2 * jj + 1, sb_ref, carry)

    init = tuple((jnp.full((1, tq), NEG_BIG, F32), jnp.zeros((1, tq), F32), jnp.zeros((128, tq), F32))
                 for _ in range(nh))
    scores(0, sa_ref)
    carry = lax.fori_loop(0, i // 2, two_blocks, init)
    carry = lax.fori_loop(0, i & 1, lambda _, c: update(i - 1, sa_ref, c), carry)
    scores(i, sb_ref, diag=True)
    final = update(i, sb_ref, carry)
    row = lax.broadcasted_iota(I32, (128, tq), 0)
    for p in range(nh // 2):
        (_, l0, acc0), (_, l1, acc1) = final[2 * p], final[2 * p + 1]
        o_ref[0, :, p * 128:(p + 1) * 128] = jnp.where(row < HEAD, acc0 / l0, acc1 / l1).T


def _fox_attn(q_aug, k_aug, vt):
    B, H, T, _ = q_aug.shape
    tq = min(512, T)
    nh = min(ATTN_HEADS, H)
    n_sub, sub = vt.shape[2], vt.shape[4]
    return pl.pallas_call(
        _fox_attn_kernel,
        grid=(B, H // nh, T // tq),
        in_specs=[pl.BlockSpec((1, nh, tq, 128), lambda b, p, i: (b, p, i, 0)),
                  pl.BlockSpec((1, nh, T, 128), lambda b, p, i: (b, p, 0, 0)),
                  pl.BlockSpec((1, nh // 2, n_sub, 128, sub), lambda b, p, i: (b, p, 0, 0, 0))],
        out_specs=pl.BlockSpec((1, tq, nh * HEAD), lambda b, p, i: (b, i, p)),
        out_shape=jax.ShapeDtypeStruct((B, T, H * HEAD), F32),
        scratch_shapes=[pltpu.VMEM((nh, tq, tq), F32), pltpu.VMEM((nh, tq, tq), F32)],
        compiler_params=_cparams(("parallel", "parallel", "arbitrary")),
        name="fox_attn",
    )(q_aug, k_aug, vt)


def _out_route_kernel(yrw_ref, yfox_ref, x_ref, wo_rw_ref, wo_fox_ref, nw_ref, wr_hi_ref, wr_lo_ref, br_ref,
                      x1_ref, h2_ref, eid_ref, gate_ref):
    x1 = (x_ref[...] + _dot(yrw_ref[...].astype(BF16), wo_rw_ref[...])
          + _dot(yfox_ref[...].astype(BF16), wo_fox_ref[...]))
    x1_ref[...] = x1
    h2 = x1 * lax.rsqrt(jnp.mean(x1 * x1, axis=-1, keepdims=True) + NORM_EPS) * nw_ref[...]
    _store_token_major(h2_ref, h2)
    h_hi = h2.astype(BF16)
    h_lo = (h2 - h_hi.astype(F32)).astype(BF16)
    logits = (_dot(h_hi, wr_hi_ref[...]) + _dot(h_hi, wr_lo_ref[...]) + _dot(h_lo, wr_hi_ref[...])) + br_ref[...]

    lane_i = lax.broadcasted_iota(I32, logits.shape, 1)
    lane = lane_i.astype(F32)
    first = lambda mask: jnp.min(jnp.where(mask, lane, 1e9), axis=-1, keepdims=True)
    gl = jnp.where(lane_i < N_GROUPS, logits, -jnp.inf)
    gmax = jnp.max(gl, axis=-1, keepdims=True)
    g_sel = first(gl == gmax)
    g_gate = 1.0 / jnp.sum(jnp.exp(gl - gmax), axis=-1, keepdims=True)
    e_lane = lane_i - N_GROUPS
    lane_grp = jnp.right_shift(e_lane, 3).astype(F32)
    in_grp = (e_lane >= 0) & (e_lane < N_EXPERTS) & (lane_grp == g_sel)
    el = jnp.where(in_grp, logits, -jnp.inf)
    m1 = jnp.max(el, axis=-1, keepdims=True)
    i1 = first(el == m1)
    el2 = jnp.where(lane == i1, -jnp.inf, el)
    m2 = jnp.max(el2, axis=-1, keepdims=True)
    i2 = first(el2 == m2)
    e2 = jnp.exp(m2 - m1)
    g1 = g_gate / (1.0 + e2)
    g2 = g_gate * e2 / (1.0 + e2)
    eid_ref[...] = jnp.where(lane_i == 0, i1 - N_GROUPS, jnp.where(lane_i == 1, i2 - N_GROUPS, 0.0)).astype(I32)
    gate_ref[...] = jnp.where(lane_i == 0, g1, jnp.where(lane_i == 1, g2, 0.0))


def _out_route(y_rw, y_fox, x2, wo_rw, wo_fox, norm_w, wr_hi, wr_lo, b_r):
    M, D = x2.shape
    W = y_rw.shape[1]
    tm = min(512, M)
    full = lambda a: pl.BlockSpec(a.shape, lambda i: (0, 0))
    tile = lambda w: pl.BlockSpec((tm, w), lambda i: (i, 0))
    return pl.pallas_call(
        _out_route_kernel,
        grid=(M // tm,),
        in_specs=[tile(W), tile(W), tile(D), full(wo_rw), full(wo_fox), full(norm_w),
                  full(wr_hi), full(wr_lo), full(b_r)],
        out_specs=[tile(D), pl.BlockSpec((tm * (D // 128), 128), lambda i: (i, 0)),
                   tile(ROUTER_LANES), tile(ROUTER_LANES)],
        out_shape=[jax.ShapeDtypeStruct((M, D), F32), jax.ShapeDtypeStruct((M * (D // 128), 128), F32),
                   jax.ShapeDtypeStruct((M, ROUTER_LANES), I32), jax.ShapeDtypeStruct((M, ROUTER_LANES), F32)],
        compiler_params=_cparams(("parallel",)),
        name="out_route",
    )(y_rw, y_fox, x2, wo_rw, wo_fox, norm_w, wr_hi, wr_lo, b_r)


def _rank_kernel(eid_ref, rank_ref, cnt_ref, carry_ref):
    @pl.when(pl.program_id(0) == 0)
    def _():
        carry_ref[...] = jnp.zeros_like(carry_ref)

    eid = eid_ref[...].astype(F32)
    tm = eid.shape[0]
    lane = lax.broadcasted_iota(I32, eid.shape, 1)
    lane_f = lane.astype(F32)
    pick = lambda l: jnp.sum(jnp.where(lane == l, eid, 0.0), axis=-1, keepdims=True)
    e0, e1 = pick(0), pick(1)
    oh0 = (lane_f == e0).astype(F32)
    oh1 = (lane_f == e1).astype(F32)
    both = oh0 + oh1
    ri = lax.broadcasted_iota(I32, (tm, tm), 0)
    ci = lax.broadcasted_iota(I32, (tm, tm), 1)
    before = _dot(jnp.where(ri > ci, 1.0, 0.0).astype(BF16), both.astype(BF16)) + carry_ref[...]
    r0 = jnp.sum(oh0 * before, axis=-1, keepdims=True)
    r1 = jnp.sum(oh1 * (before + oh0), axis=-1, keepdims=True)
    rank_ref[...] = jnp.where(lane == 0, r0, jnp.where(lane == 1, r1, 0.0)).astype(I32)
    total = carry_ref[...] + jnp.sum(both, axis=0, keepdims=True)
    carry_ref[...] = total
    cnt_ref[...] = jnp.broadcast_to(total, cnt_ref.shape).astype(I32)


def _rank(eid):
    M = eid.shape[0]
    tm = min(512, M)
    return pl.pallas_call(
        _rank_kernel,
        grid=(M // tm,),
        in_specs=[pl.BlockSpec((tm, ROUTER_LANES), lambda i: (i, 0))],
        out_specs=[pl.BlockSpec((tm, ROUTER_LANES), lambda i: (i, 0)),
                   pl.BlockSpec((8, ROUTER_LANES), lambda i: (0, 0))],
        out_shape=[jax.ShapeDtypeStruct((M, ROUTER_LANES), I32), jax.ShapeDtypeStruct((8, ROUTER_LANES), I32)],
        scratch_shapes=[pltpu.VMEM((1, ROUTER_LANES), F32)],
        compiler_params=_cparams(("arbitrary",)),
        name="rank",
    )(eid)


def _dest_kernel(eid_ref, rank_ref, start_ref, dest_ref):
    eid = eid_ref[...].astype(F32)
    rank = rank_ref[...].astype(F32)
    lane = lax.broadcasted_iota(I32, eid.shape, 1)
    lane_f = lane.astype(F32)
    pick = lambda x, l: jnp.sum(jnp.where(lane == l, x, 0.0), axis=-1, keepdims=True)
    base = lambda e: jnp.sum(jnp.where(lane_f == e, start_ref[...], 0.0), axis=-1, keepdims=True)
    d0 = base(pick(eid, 0)) + pick(rank, 0)
    d1 = base(pick(eid, 1)) + pick(rank, 1)
    dest_ref[...] = jnp.where(lane == 0, d0, jnp.where(lane == 1, d1, 0.0)).astype(I32)


def _dest(eid, rank, pad_start_row):
    M = eid.shape[0]
    tm = min(1024, M)
    tile = pl.BlockSpec((tm, ROUTER_LANES), lambda i: (i, 0))
    return pl.pallas_call(
        _dest_kernel,
        grid=(M // tm,),
        in_specs=[tile, tile, pl.BlockSpec((1, ROUTER_LANES), lambda i: (0, 0))],
        out_specs=tile,
        out_shape=jax.ShapeDtypeStruct((M, ROUTER_LANES), I32),
        compiler_params=_cparams(("parallel",)),
        name="dest",
    )(eid, rank, pad_start_row)


def _invert_kernel(dest_ref, cnt_ref, start_ref, end_ref, tok_ref):
    def clear(r, carry):
        tok_ref[r] = 0
        return carry

    def clear_expert(e, carry):
        lax.fori_loop(start_ref[e] + cnt_ref[e], end_ref[e], clear, 0)
        return carry

    def put(a, carry):
        tok_ref[dest_ref[a]] = lax.shift_right_logical(a, 1)
        return carry

    lax.fori_loop(0, cnt_ref.shape[0], clear_expert, 0)
    lax.fori_loop(end_ref[end_ref.shape[0] - 1], tok_ref.shape[0], clear, 0)
    lax.fori_loop(0, dest_ref.shape[0], put, 0, unroll=16)


def _invert(dest_flat, counts, pad_start, pad_end, n_rows):
    smem = pl.BlockSpec(memory_space=pltpu.SMEM)
    return pl.pallas_call(
        _invert_kernel,
        in_specs=[smem] * 4,
        out_specs=smem,
        out_shape=jax.ShapeDtypeStruct((n_rows,), I32),
        name="invert",
    )(dest_flat, counts, pad_start, pad_end)


def _experts_kernel(be_ref, nused_ref, tok_ref, h_hbm, wg_ref, wu_ref, wd_ref, yb_ref,
                    xbuf, sem, wg_bf, wu_bf, wd_bf):
    i = pl.program_id(0)
    D, F = wg_bf.shape
    nt = D // 128
    rb = yb_ref.shape[0] // nt
    n_used = nused_ref[0]
    slot = i & 1

    def copy(blk, r, s):
        src = pl.multiple_of(tok_ref[blk * rb + r] * nt, nt)
        return pltpu.make_async_copy(h_hbm.at[pl.ds(src, nt)], xbuf.at[s, pl.ds(r * nt, nt)], sem.at[s])

    def wait_block(blk, s):
        def body(r, carry):
            copy(blk, r, s).wait()
            return carry
        lax.fori_loop(0, rb, body, 0, unroll=8)

    @pl.when(i == 0)
    def _():
        def body(r, carry):
            copy(0, r, 0).start()
            return carry
        lax.fori_loop(0, rb, body, 0, unroll=8)

    prev = be_ref[jnp.maximum(i - 1, 0)]

    @pl.when((i == 0) | (be_ref[i] != prev))
    def _():
        wg_bf[...] = wg_ref[0].astype(BF16)
        wu_bf[...] = wu_ref[0].astype(BF16)
        wd_bf[...] = wd_ref[0].astype(BF16)

    @pl.when(i < n_used)
    def _():
        wait_block(i, slot)

        def start_next(r, carry):
            copy(i + 1, 2 * r, 1 - slot).start(priority=0)
            copy(i + 1, 2 * r + 1, 1 - slot).start(priority=1)
            return carry
        lax.fori_loop(0, rb // 2, start_next, 0, unroll=4)

        xb = jnp.concatenate([_load_token_major(xbuf.at[slot], rb, D, s).astype(BF16) for s in range(nt)], axis=1)
        gate = _dot(xb, wg_bf[...])
        up = _dot(xb, wu_bf[...])
        hid = gate * _sigmoid(gate) * up
        _store_token_major(yb_ref, _dot(hid.astype(BF16), wd_bf[...]))

    @pl.when(i == n_used)
    def _():
        wait_block(i, slot)

    @pl.when(i >= n_used)
    def _():
        yb_ref[...] = jnp.zeros_like(yb_ref)


def _experts(block_expert, n_used, tok, h2, w_gate, w_up, w_down, rb):
    E, D, F = w_gate.shape
    nt = D // 128
    nb = tok.shape[0] // rb
    grid_spec = pltpu.PrefetchScalarGridSpec(
        num_scalar_prefetch=3,
        grid=(nb,),
        in_specs=[pl.BlockSpec(memory_space=pl.ANY),
                  pl.BlockSpec((1, D, F), lambda i, be, nu, tk: (be[i], 0, 0)),
                  pl.BlockSpec((1, D, F), lambda i, be, nu, tk: (be[i], 0, 0)),
                  pl.BlockSpec((1, F, D), lambda i, be, nu, tk: (be[i], 0, 0))],
        out_specs=pl.BlockSpec((rb * nt, 128), lambda i, be, nu, tk: (i, 0)),
        scratch_shapes=[pltpu.VMEM((2, rb * nt, 128), F32), pltpu.SemaphoreType.DMA((2,)),
                        pltpu.VMEM((D, F), BF16), pltpu.VMEM((D, F), BF16), pltpu.VMEM((F, D), BF16)],
    )
    return pl.pallas_call(
        _experts_kernel,
        grid_spec=grid_spec,
        out_shape=jax.ShapeDtypeStruct((nb * rb * nt, 128), F32),
        compiler_params=_cparams(("arbitrary",)),
        name="experts",
    )(block_expert, n_used, tok, h2, w_gate, w_up, w_down)


def _combine_kernel(dest_ref, gate_ref, x1_ref, yb_hbm, out_ref, buf, sem):
    i = pl.program_id(0)
    tmc, D = x1_ref.shape
    nt = D // 128
    slot = i & 1

    def copy(tile, t, k, s):
        src = pl.multiple_of(dest_ref[2 * (tile * tmc + t) + k] * nt, nt)
        return pltpu.make_async_copy(yb_hbm.at[pl.ds(src, nt)], buf.at[s, k, pl.ds(t * nt, nt)], sem.at[s])

    def start_tile(tile, s):
        def body(t, carry):
            copy(tile, t, 0, s).start(priority=0)
            copy(tile, t, 1, s).start(priority=1)
            return carry
        lax.fori_loop(0, tmc, body, 0, unroll=8)

    @pl.when(i == 0)
    def _():
        start_tile(0, 0)

    @pl.when(i + 1 < pl.num_programs(0))
    def _():
        start_tile(i + 1, 1 - slot)

    def wait(t, carry):
        copy(i, t, 0, slot).wait()
        copy(i, t, 1, slot).wait()
        return carry

    lax.fori_loop(0, tmc, wait, 0, unroll=8)
    gate = gate_ref[...]
    g0, g1 = gate[:, 0:1], gate[:, 1:2]
    for s in range(nt):
        cols = slice(s * 128, (s + 1) * 128)
        out_ref[:, cols] = (x1_ref[:, cols] + g0 * _load_token_major(buf.at[slot, 0], tmc, D, s)
                            + g1 * _load_token_major(buf.at[slot, 1], tmc, D, s))


def _combine(dest_flat, gates, x1, yb):
    M, D = x1.shape
    tmc = min(256, M)
    grid_spec = pltpu.PrefetchScalarGridSpec(
        num_scalar_prefetch=1,
        grid=(M // tmc,),
        in_specs=[pl.BlockSpec((tmc, ROUTER_LANES), lambda i, d: (i, 0)),
                  pl.BlockSpec((tmc, D), lambda i, d: (i, 0)),
                  pl.BlockSpec(memory_space=pl.ANY)],
        out_specs=pl.BlockSpec((tmc, D), lambda i, d: (i, 0)),
        scratch_shapes=[pltpu.VMEM((2, 2, tmc * (D // 128), 128), F32), pltpu.SemaphoreType.DMA((2,))],
    )
    return pl.pallas_call(
        _combine_kernel,
        grid_spec=grid_spec,
        out_shape=jax.ShapeDtypeStruct((M, D), F32),
        compiler_params=_cparams(("arbitrary",)),
        name="combine",
    )(dest_flat, gates, x1, yb)


def _mixer(x2, B, T, norm_w, w_in, mu, w0, w2, a0, a2, g2, k_k, k_a, r_k, ln_w, ln_b, b_f, qn_w, kn_w):
    rw_heads = w0.shape[0] // HEAD
    fox_heads = b_f.shape[0]
    Wr, Wf = rw_heads * HEAD, fox_heads * HEAD
    lora = w2.shape[0] + a2.shape[0] + g2.shape[0]
    rw_cols = 3 * Wr + lora
    o_w, o_k, o_v, o_a = Wr, Wr + w2.shape[0], 2 * Wr + w2.shape[0], 3 * Wr + w2.shape[0]
    perm = np.concatenate([np.arange(0, Wr), np.arange(o_k, o_k + Wr), np.arange(o_v, o_v + Wr),
                           np.arange(o_w, o_w + w2.shape[0]), np.arange(o_a, rw_cols)])
    w_rw = w_in[:, :rw_cols][:, perm].astype(BF16)
    w_qkv = w_in[:, rw_cols:rw_cols + 3 * Wf].astype(BF16)
    w_f = jnp.pad(w_in[:, rw_cols + 3 * Wf:], ((0, 0), (0, 128 - fox_heads))).astype(BF16)
    p_rw, p_qkv, p_f = _in_proj(x2, norm_w[None, :], w_rw, w_qkv, w_f)

    row = lambda a: a.reshape(1, -1)
    ops = _rw_prep(p_rw.reshape(B, T, rw_cols), row(mu[perm]), row(w0), w2.astype(BF16), row(a0),
                   a2.astype(BF16), g2.astype(BF16), row(k_k), row(k_a), row(r_k), rw_heads)
    y_rw = _rw_scan(*ops, row(ln_w), row(ln_b))

    tile_w = lambda w: row(jnp.tile(w, fox_heads))
    q_aug, k_aug, vt = _fox_prep(p_qkv.reshape(B, T, 3 * Wf), p_f.reshape(B, T, 128),
                                 row(jnp.pad(b_f, (0, 128 - fox_heads))), tile_w(qn_w), tile_w(kn_w), fox_heads)
    y_fox = _fox_attn(q_aug, k_aug, vt)
    return y_rw.reshape(B * T, Wr), y_fox.reshape(B * T, Wf)


def _moe(y_rw, y_fox, x2, w_out, norm_w, rg_w, rg_b, re_w, re_b, w_gate, w_up, w_down):
    M, D = x2.shape
    Wr = y_rw.shape[1]
    pad = ROUTER_LANES - N_GROUPS - N_EXPERTS
    w_r = jnp.pad(jnp.concatenate([rg_w, re_w], axis=1), ((0, 0), (0, pad)))
    b_r = jnp.pad(jnp.concatenate([rg_b, re_b]), (0, pad))[None, :]
    wr_hi = w_r.astype(BF16)
    wr_lo = (w_r - wr_hi.astype(F32)).astype(BF16)
    x1, h2, eid, gates = _out_route(y_rw, y_fox, x2, w_out[:Wr].astype(BF16), w_out[Wr:].astype(BF16),
                                    norm_w[None, :], wr_hi, wr_lo, b_r)
    rank, counts = _rank(eid)

    rb = ROW_BLOCK
    counts = counts[0, :N_EXPERTS]
    padded = (counts + rb - 1) // rb * rb
    pad_end = jnp.cumsum(padded)
    pad_start = pad_end - padded
    n_blocks = (2 * M + N_EXPERTS * (rb - 1) + rb - 1) // rb + 1
    block_start = jnp.arange(n_blocks, dtype=I32) * rb
    block_expert = jnp.minimum(jnp.sum(pad_end[None, :] <= block_start[:, None], axis=1), N_EXPERTS - 1).astype(I32)
    n_used = (pad_end[-1:] // rb).astype(I32)
    start_row = jnp.pad(pad_start.astype(F32), (0, ROUTER_LANES - N_EXPERTS))[None, :]
    dest = _dest(eid, rank, start_row)[:, :2].reshape(-1)

    tok = _invert(dest, counts, pad_start.astype(I32), pad_end.astype(I32), n_blocks * rb)
    yb = _experts(block_expert, n_used, tok, h2, w_gate, w_up, w_down, rb)
    return _combine(dest, gates, x1, yb)


def kernel(x, norm_mix_w, w_in, mu_shift, rw_w0, rw_w2, rw_a0, rw_a2, rw_g2, rw_k_k, rw_k_a, rw_r_k, rw_ln_w, rw_ln_b, fox_b_f, fox_q_norm_w, fox_k_norm_w, w_out, norm_ffn_w, router_group_w, router_group_b, router_expert_w, router_expert_b, exp_w_gate, exp_w_up, exp_w_down):
    B, T, D = x.shape
    x2 = x.reshape(B * T, D)
    for l in range(w_in.shape[0]):
        y_rw, y_fox = _mixer(x2, B, T, norm_mix_w[l], w_in[l], mu_shift[l], rw_w0[l], rw_w2[l], rw_a0[l],
                             rw_a2[l], rw_g2[l], rw_k_k[l], rw_k_a[l], rw_r_k[l].reshape(-1), rw_ln_w[l],
                             rw_ln_b[l], fox_b_f[l], fox_q_norm_w[l], fox_k_norm_w[l])
        x2 = _moe(y_rw, y_fox, x2, w_out[l], norm_ffn_w[l], router_group_w[l], router_group_b[l],
                  router_expert_w[l], router_expert_b[l], exp_w_gate[l], exp_w_up[l], exp_w_down[l])
    return x2.reshape(B, T, D)
```

```python
import functools

import jax
import jax.numpy as jnp
import numpy as np
from jax import lax
from jax.experimental import pallas as pl
from jax.experimental.pallas import tpu as pltpu

F32, BF16, I32 = jnp.float32, jnp.bfloat16, jnp.int32

HEAD = 64
CHUNK = 64
GROUP = 2
GW = GROUP * HEAD
SCAN_BATCH = 4
ATTN_HEADS = 4
N_GROUPS = 8
EXPERTS_PER_GROUP = 8
N_EXPERTS = N_GROUPS * EXPERTS_PER_GROUP
ROUTER_LANES = 128
ROW_BLOCK = 256
NORM_EPS = 1e-6
GN_EPS = 64e-5
NEG_BIG = -1e30
LOG2E = 1.4426950408889634
VMEM_LIMIT = 56 * 1024 * 1024


def _cparams(sem):
    return pltpu.CompilerParams(dimension_semantics=sem, vmem_limit_bytes=VMEM_LIMIT)


def _dot(a, b):
    return jnp.dot(a, b, preferred_element_type=F32)


def _dot_nt(a, b):
    return lax.dot_general(a, b, (((1,), (1,)), ((), ())), preferred_element_type=F32)


def _dot_tn(a, b):
    return lax.dot_general(a, b, (((0,), (0,)), ((), ())), preferred_element_type=F32)


def _split3(x):
    hi = x.astype(BF16)
    r1 = x - hi.astype(F32)
    mid = r1.astype(BF16)
    lo = (r1 - mid.astype(F32)).astype(BF16)
    return hi, mid, lo


def _sigmoid(z):
    return 1.0 / (1.0 + jnp.exp(-z))


def _head_sum(x):
    cols = []
    for c in range(x.shape[1] // 128):
        xc = x[:, c * 128:(c + 1) * 128]
        first = lax.broadcasted_iota(I32, xc.shape, 1) < HEAD
        s0 = jnp.sum(jnp.where(first, xc, 0.0), axis=-1, keepdims=True)
        s1 = jnp.sum(jnp.where(first, 0.0, xc), axis=-1, keepdims=True)
        cols.append(jnp.where(first, s0, s1))
    return cols[0] if len(cols) == 1 else jnp.concatenate(cols, axis=1)


def _store_token_major(ref, x):
    rows, d = x.shape
    for s in range(d // 128):
        ref[pl.ds(s, rows, stride=d // 128), :] = x[:, s * 128:(s + 1) * 128]


def _load_token_major(ref, rows, d, s):
    return ref[pl.ds(s, rows, stride=d // 128), :]


def _in_proj_kernel(x_ref, nw_ref, wrw_ref, wqkv_ref, wf_ref, prw_ref, pqkv_ref, pf_ref):
    x = x_ref[...]
    h = x * lax.rsqrt(jnp.mean(x * x, axis=-1, keepdims=True) + NORM_EPS) * nw_ref[...]
    hb = h.astype(BF16)
    prw_ref[...] = _dot(hb, wrw_ref[...])
    pqkv_ref[...] = _dot(hb, wqkv_ref[...])
    pf_ref[...] = _dot(hb, wf_ref[...])


def _in_proj(x2, norm_w, w_rw, w_qkv, w_f):
    M, D = x2.shape
    tm = min(512, M)
    n_rw, n_qkv, n_f = w_rw.shape[1], w_qkv.shape[1], w_f.shape[1]
    full = lambda shape: pl.BlockSpec(shape, lambda i: (0, 0))
    return pl.pallas_call(
        _in_proj_kernel,
        grid=(M // tm,),
        in_specs=[pl.BlockSpec((tm, D), lambda i: (i, 0)), full((1, D)),
                  full((D, n_rw)), full((D, n_qkv)), full((D, n_f))],
        out_specs=[pl.BlockSpec((tm, n_rw), lambda i: (i, 0)),
                   pl.BlockSpec((tm, n_qkv), lambda i: (i, 0)),
                   pl.BlockSpec((tm, n_f), lambda i: (i, 0))],
        out_shape=[jax.ShapeDtypeStruct((M, n_rw), F32),
                   jax.ShapeDtypeStruct((M, n_qkv), F32),
                   jax.ShapeDtypeStruct((M, n_f), F32)],
        compiler_params=_cparams(("parallel",)),
        name="in_proj",
    )(x2, norm_w, w_rw, w_qkv, w_f)


def _rw_prep_kernel(n_heads, p_ref, pprev_ref, mu_ref, w0_ref, w2_ref, a0_ref, a2_ref, g2_ref,
                    kk_ref, ka_ref, rk_ref,
                    rt_ref, at_ref, bt_ref, kt_ref, bb_ref, kb_ref, v_ref, dl_ref, bonus_ref, g_ref):
    W = n_heads * HEAD
    j = pl.program_id(1)
    p = p_ref[0]
    tt = p.shape[0]
    last_prev = jnp.where(j > 0, pprev_ref[0, 7:8, :], 0.0)
    row = lax.broadcasted_iota(I32, p.shape, 0)
    prev = jnp.where(row == 0, last_prev, pltpu.roll(p, 1, axis=0))
    ps = p + (prev - p) * mu_ref[...]
    r, k, v = ps[:, 0:W], ps[:, W:2 * W], ps[:, 2 * W:3 * W]
    o = 3 * W
    pw, pa, pg = ps[:, o:o + 64], ps[:, o + 64:o + 128], ps[:, o + 128:o + 256]

    z = w0_ref[...] + _dot(jnp.tanh(pw).astype(BF16), w2_ref[...])
    lw = (-np.exp(-0.5)).astype(np.float32) * _sigmoid(z)
    a_sig = _sigmoid(a0_ref[...] + _dot(pa.astype(BF16), a2_ref[...]))
    g_ref[0] = _dot(_sigmoid(pg).astype(BF16), g2_ref[...])

    kk = k * kk_ref[...]
    kk = kk / jnp.maximum(jnp.sqrt(_head_sum(kk * kk)), 1e-12)
    km = k * (1.0 + (a_sig - 1.0) * ka_ref[...])
    a_vec = -kk
    b_vec = kk * a_sig
    bonus_ref[0] = _head_sum(r * km * rk_ref[...]) * v
    v_ref[0] = v.astype(BF16)

    ti = lax.broadcasted_iota(I32, (tt, tt), 0)
    si = lax.broadcasted_iota(I32, (tt, tt), 1)
    same = (ti // CHUNK) == (si // CHUNK)
    tri = jnp.where(same & (ti >= si), 1.0, 0.0).astype(BF16)
    ones = jnp.where(same, 1.0, 0.0).astype(BF16)
    nck = tt // CHUNK
    ci = lax.broadcasted_iota(I32, (nck, tt), 0)
    cs = lax.broadcasted_iota(I32, (nck, tt), 1)
    sel = jnp.where(ci == cs // CHUNK, 1.0, 0.0).astype(BF16)
    hi, mid, lo = _split3(lw)
    cum = _dot(tri, hi) + _dot(tri, mid) + _dot(tri, lo)
    tot = _dot(ones, hi) + _dot(ones, mid) + _dot(ones, lo)
    dl = jnp.exp(_dot(sel, hi) + _dot(sel, mid) + _dot(sel, lo))
    for ck in range(nck):
        dl_ref[0, ck] = dl[ck:ck + 1, :]

    e_in = jnp.exp(cum)
    e_ex = jnp.exp(cum - lw)
    e_inv = jnp.exp(-cum)
    e_bar = jnp.exp(tot - cum)
    rt_ref[0] = (r * e_in).astype(BF16)
    at_ref[0] = (a_vec * e_ex).astype(BF16)
    bt_ref[0] = (b_vec * e_inv).astype(BF16)
    kt_ref[0] = (km * e_inv).astype(BF16)
    bb_ref[0] = (b_vec * e_bar).astype(BF16)
    kb_ref[0] = (km * e_bar).astype(BF16)


def _rw_prep(p_rw, mu, w0, w2, a0, a2, g2, k_k, k_a, r_k, n_heads):
    B, T, P = p_rw.shape
    W = n_heads * HEAD
    tt = min(512, T)
    row = lambda a: pl.BlockSpec(a.shape, lambda b, j: (0, 0))
    tile = lambda w: pl.BlockSpec((1, tt, w), lambda b, j: (b, j, 0))
    bf = jax.ShapeDtypeStruct((B, T, W), BF16)
    f32 = jax.ShapeDtypeStruct((B, T, W), F32)
    return pl.pallas_call(
        functools.partial(_rw_prep_kernel, n_heads),
        grid=(B, T // tt),
        in_specs=[tile(P),
                  pl.BlockSpec((1, 8, P), lambda b, j: (b, jnp.maximum(j * (tt // 8) - 1, 0), 0)),
                  row(mu), row(w0), row(w2), row(a0), row(a2), row(g2), row(k_k), row(k_a), row(r_k)],
        out_specs=[tile(W)] * 7 + [pl.BlockSpec((1, tt // CHUNK, 1, W), lambda b, j: (b, j, 0, 0)), tile(W), tile(W)],
        out_shape=[bf] * 7 + [jax.ShapeDtypeStruct((B, T // CHUNK, 1, W), F32), f32, f32],
        compiler_params=_cparams(("parallel", "parallel")),
        name="rw_prep",
    )(p_rw, p_rw, mu, w0, w2, a0, a2, g2, k_k, k_a, r_k)


def _rw_scan_kernel(rt_ref, at_ref, bt_ref, kt_ref, bb_ref, kb_ref, v_ref, dl_ref,
                    bonus_ref, g_ref, lnw_ref, lnb_ref, y_ref, s_ref):
    C = CHUNK
    nb, tt, W = rt_ref.shape
    chains = [(b, g) for b in range(nb) for g in range(W // GW)]

    @pl.when(pl.program_id(1) == 0)
    def _():
        s_ref[...] = jnp.zeros_like(s_ref)

    ri = lax.broadcasted_iota(I32, (GW, GW), 0)
    ci = lax.broadcasted_iota(I32, (GW, GW), 1)
    tr, tc = ri & (C - 1), ci & (C - 1)
    strict, incl, eye = tr > tc, tr >= tc, ri == ci
    lane_head = lax.broadcasted_iota(I32, (C, GW), 1) // HEAD

    def bd(x):
        return jnp.concatenate([jnp.where(lane_head == h, x, jnp.zeros_like(x)) for h in range(GROUP)], axis=0)

    bf = lambda xs: [x.astype(BF16) for x in xs]

    def chunk(c, carry):
        rows = pl.ds(pl.multiple_of(c * C, C), C)
        ld = lambda ref: [bd(ref[b, rows, g * GW:(g + 1) * GW]) for b, g in chains]
        Rt, At, Bt, Kt, Bb, Kb, V = (ld(r) for r in (rt_ref, at_ref, bt_ref, kt_ref, bb_ref, kb_ref, v_ref))
        BK = [jnp.concatenate([b, k], axis=0) for b, k in zip(Bt, Kt)]
        XA = [_dot_nt(a, bk) for a, bk in zip(At, BK)]
        XR = [_dot_nt(r, bk) for r, bk in zip(Rt, BK)]
        Lab = [jnp.where(strict, x[:, :GW], 0.0) for x in XA]
        Lak = bf([jnp.where(strict, x[:, GW:], 0.0) for x in XA])
        Mrb = bf([jnp.where(incl, x[:, :GW], 0.0) for x in XR])
        Mrk = bf([jnp.where(incl, x[:, GW:], 0.0) for x in XR])
        P = [jnp.where(eye, 1.0, l) for l in Lab]
        Lp = Lab
        for _ in range(5):
            Lpb = bf(Lp)
            Lp = [_dot(x, x) for x in Lpb]
            P = [p + _dot(pb, lb) for p, pb, lb in zip(P, bf(P), bf(Lp))]
        Pb = bf(P)
        Wk = bf([_dot(l, v) for l, v in zip(Lak, V)])
        AU = bf([_dot(p, jnp.concatenate([a, w], axis=1)) for p, a, w in zip(Pb, At, Wk)])
        Ahb, U0b = [x[:, :GW] for x in AU], [x[:, GW:] for x in AU]
        MAU = [_dot(m, x) for m, x in zip(Mrb, AU)]
        Rh = bf([r.astype(F32) + x[:, :GW] for r, x in zip(Rt, MAU)])
        Y0 = [x[:, GW:] + _dot(mk, v) for x, mk, v in zip(MAU, Mrk, V)]
        GT = bf([_dot_tn(a, b) for a, b in zip(Ahb, Bb)])
        HT = [_dot_tn(u, b) + _dot_tn(v, k) for u, b, v, k in zip(U0b, Bb, V, Kb)]
        S0 = [s_ref[i] for i in range(len(chains))]
        S0b = bf(S0)
        Y = [_dot_nt(r, s) + y0 for r, s, y0 in zip(Rh, S0b, Y0)]
        for i, (b, g) in enumerate(chains):
            ls = slice(g * GW, (g + 1) * GW)
            s_ref[i] = S0[i] * dl_ref[b, c, :, ls] + _dot(S0b[i], GT[i]) + HT[i]
            y = sum(Y[i][h * C:(h + 1) * C] for h in range(GROUP))
            mean = _head_sum(y) * (1.0 / HEAD)
            yc = y - mean
            var = _head_sum(yc * yc) * (1.0 / HEAD)
            yn = yc * lax.rsqrt(var + GN_EPS) * lnw_ref[:, ls] + lnb_ref[:, ls]
            y_ref[b, rows, ls] = (yn + bonus_ref[b, rows, ls]) * g_ref[b, rows, ls]
        return carry

    lax.fori_loop(0, tt // C, chunk, 0)


def _rw_scan(rt, at, bt, kt, bb, kb, v, dl, bonus, g, ln_w, ln_b):
    B, T, W = rt.shape
    nb = min(SCAN_BATCH, B)
    tt = min(256, T)
    tile = pl.BlockSpec((nb, tt, W), lambda b, j: (b, j, 0))
    row = pl.BlockSpec((1, W), lambda b, j: (0, 0))
    return pl.pallas_call(
        _rw_scan_kernel,
        grid=(B // nb, T // tt),
        in_specs=[tile] * 7 + [pl.BlockSpec((nb, tt // CHUNK, 1, W), lambda b, j: (b, j, 0, 0)), tile, tile, row, row],
        out_specs=tile,
        out_shape=jax.ShapeDtypeStruct((B, T, W), F32),
        scratch_shapes=[pltpu.VMEM((nb * (W // GW), GW, GW), F32)],
        compiler_params=_cparams(("parallel", "arbitrary")),
        name="rw_scan",
    )(rt, at, bt, kt, bb, kb, v, dl, bonus, g, ln_w, ln_b)


def _fox_prep_kernel(n_heads, qkv_ref, f_ref, bf_ref, qnw_ref, knw_ref,
                     q_ref, k_ref, vt_ref, carry_ref):
    W = n_heads * HEAD
    tt = qkv_ref.shape[1]

    @pl.when(pl.program_id(1) == 0)
    def _():
        carry_ref[...] = jnp.zeros_like(carry_ref)

    qkv = qkv_ref[0]
    q, k, v = qkv[:, 0:W], qkv[:, W:2 * W], qkv[:, 2 * W:3 * W]
    inv_n = 1.0 / HEAD
    qn = q * lax.rsqrt(_head_sum(q * q) * inv_n + NORM_EPS) * (qnw_ref[...] * (HEAD ** -0.5 * LOG2E))
    kn = k * lax.rsqrt(_head_sum(k * k) * inv_n + NORM_EPS) * knw_ref[...]
    for p in range(n_heads // 2):
        vt_ref[0, p, 0] = v[:, p * 128:(p + 1) * 128].T.astype(BF16)

    zf = f_ref[0] + bf_ref[...]
    logf = jnp.minimum(zf, 0.0) - jnp.log(1.0 + jnp.exp(-jnp.abs(zf)))
    ti = lax.broadcasted_iota(I32, (tt, tt), 0)
    si = lax.broadcasted_iota(I32, (tt, tt), 1)
    tri = jnp.where(ti >= si, 1.0, 0.0).astype(BF16)
    hi, mid, lo = _split3(logf)
    c = carry_ref[...] + _dot(tri, hi) + _dot(tri, mid) + _dot(tri, lo)
    carry_ref[...] = c[tt - 1:tt, :]
    c_hi, c_mid, c_lo = _split3(c * (-LOG2E))
    lane = lax.broadcasted_iota(I32, (tt, 128), 1)
    c3 = jnp.where(lane < 8, c_hi.astype(F32),
                   jnp.where(lane < 16, pltpu.roll(c_mid.astype(F32), 8, axis=1),
                             pltpu.roll(c_lo.astype(F32), 16, axis=1))).astype(BF16)
    src = lax.broadcasted_iota(I32, (128, 128), 0)
    dst = lax.broadcasted_iota(I32, (128, 128), 1)
    ones = jnp.where((lane >= HEAD) & (lane < HEAD + 3), 1.0, 0.0)
    for h in range(n_heads):
        col = slice((h // 2) * 128, (h // 2 + 1) * 128)
        qh, kh = qn[:, col], kn[:, col]
        if h % 2:
            qh, kh = pltpu.roll(qh, HEAD, axis=1), pltpu.roll(kh, HEAD, axis=1)
        sel = jnp.where((dst >= HEAD) & (dst < HEAD + 3) & (src == h + 8 * (dst - HEAD)), 1.0, 0.0).astype(BF16)
        q_ref[0, h] = jnp.where(lane < HEAD, qh, ones).astype(BF16)
        k_ref[0, h] = jnp.where(lane < HEAD, kh, _dot(c3, sel)).astype(BF16)


def _fox_prep(qkv, f, b_f, qn_w, kn_w, n_heads):
    B, T, _ = qkv.shape
    W = n_heads * HEAD
    tt = min(256, T)
    row = lambda a: pl.BlockSpec(a.shape, lambda b, j: (0, 0))
    aug = jax.ShapeDtypeStruct((B, n_heads, T, 128), BF16)
    aug_spec = pl.BlockSpec((1, n_heads, tt, 128), lambda b, j: (b, 0, j, 0))
    return pl.pallas_call(
        functools.partial(_fox_prep_kernel, n_heads),
        grid=(B, T // tt),
        in_specs=[pl.BlockSpec((1, tt, 3 * W), lambda b, j: (b, j, 0)),
                  pl.BlockSpec((1, tt, 128), lambda b, j: (b, j, 0)),
                  row(b_f), row(qn_w), row(kn_w)],
        out_specs=[aug_spec, aug_spec,
                   pl.BlockSpec((1, n_heads // 2, 1, 128, tt), lambda b, j: (b, 0, j, 0, 0))],
        out_shape=[aug, aug, jax.ShapeDtypeStruct((B, n_heads // 2, T // tt, 128, tt), BF16)],
        scratch_shapes=[pltpu.VMEM((1, 128), F32)],
        compiler_params=_cparams(("parallel", "arbitrary")),
        name="fox_prep",
    )(qkv, f, b_f, qn_w, kn_w)


def _fox_attn_kernel(q_ref, k_ref, vt_ref, o_ref, sa_ref, sb_ref):
    i = pl.program_id(2)
    nh, tq = q_ref.shape[1], q_ref.shape[2]
    sub = vt_ref.shape[4]
    q = [q_ref[0, hh] for hh in range(nh)]
    ki = lax.broadcasted_iota(I32, (tq, tq), 0)
    qi = lax.broadcasted_iota(I32, (tq, tq), 1)

    def scores(j, s_ref, diag=False):
        k0 = pl.multiple_of(j * tq, tq)
        for hh in range(nh):
            s = _dot_nt(k_ref[0, hh, pl.ds(k0, tq), :], q[hh])
            s_ref[hh] = jnp.where(ki <= qi, s, -jnp.inf) if diag else s

    def update(j, s_ref, carry):
        out = []
        for hh in range(nh):
            m, l, acc = carry[hh]
            m_new = jnp.maximum(m, jnp.max(s_ref[hh], axis=0, keepdims=True))
            alpha = jnp.exp2(m - m_new)
            p = jnp.exp2(s_ref[hh] - m_new)
            l = alpha * l + jnp.sum(p, axis=0, keepdims=True)
            p = p.astype(BF16)
            pv = sum(_dot(vt_ref[0, hh // 2, j * (tq // sub) + u], p[u * sub:(u + 1) * sub])
                     for u in range(tq // sub))
            out.append((m_new, l, alpha * acc + pv))
        return tuple(out)

    def two_blocks(jj, carry):
        scores(2 * jj + 1, sb_ref)
        carry = update(2 * jj, sa_ref, carry)
        scores(jnp.minimum(2 * jj + 2, i), sa_ref)
        return update(2 * jj + 1, sb_ref, carry)

    init = tuple((jnp.full((1, tq), NEG_BIG, F32), jnp.zeros((1, tq), F32), jnp.zeros((128, tq), F32))
                 for _ in range(nh))
    scores(0, sa_ref)
    carry = lax.fori_loop(0, i // 2, two_blocks, init)
    carry = lax.fori_loop(0, i & 1, lambda _, c: update(i - 1, sa_ref, c), carry)
    scores(i, sb_ref, diag=True)
    final = update(i, sb_ref, carry)
    row = lax.broadcasted_iota(I32, (128, tq), 0)
    for p in range(nh // 2):
        (_, l0, acc0), (_, l1, acc1) = final[2 * p], final[2 * p + 1]
        o_ref[0, :, p * 128:(p + 1) * 128] = jnp.where(row < HEAD, acc0 / l0, acc1 / l1).T


def _fox_attn(q_aug, k_aug, vt):
    B, H, T, _ = q_aug.shape
    tq = min(512, T)
    nh = min(ATTN_HEADS, H)
    n_sub, sub = vt.shape[2], vt.shape[4]
    return pl.pallas_call(
        _fox_attn_kernel,
        grid=(B, H // nh, T // tq),
        in_specs=[pl.BlockSpec((1, nh, tq, 128), lambda b, p, i: (b, p, i, 0)),
                  pl.BlockSpec((1, nh, T, 128), lambda b, p, i: (b, p, 0, 0)),
                  pl.BlockSpec((1, nh // 2, n_sub, 128, sub), lambda b, p, i: (b, p, 0, 0, 0))],
        out_specs=pl.BlockSpec((1, tq, nh * HEAD), lambda b, p, i: (b, i, p)),
        out_shape=jax.ShapeDtypeStruct((B, T, H * HEAD), F32),
        scratch_shapes=[pltpu.VMEM((nh, tq, tq), F32), pltpu.VMEM((nh, tq, tq), F32)],
        compiler_params=_cparams(("parallel", "parallel", "arbitrary")),
        name="fox_attn",
    )(q_aug, k_aug, vt)


def _out_route_kernel(yrw_ref, yfox_ref, x_ref, wo_rw_ref, wo_fox_ref, nw_ref, wr_hi_ref, wr_lo_ref, br_ref,
                      x1_ref, h2_ref, eid_ref, gate_ref):
    x1 = (x_ref[...] + _dot(yrw_ref[...].astype(BF16), wo_rw_ref[...])
          + _dot(yfox_ref[...].astype(BF16), wo_fox_ref[...]))
    x1_ref[...] = x1
    h2 = x1 * lax.rsqrt(jnp.mean(x1 * x1, axis=-1, keepdims=True) + NORM_EPS) * nw_ref[...]
    _store_token_major(h2_ref, h2)
    h_hi = h2.astype(BF16)
    h_lo = (h2 - h_hi.astype(F32)).astype(BF16)
    logits = (_dot(h_hi, wr_hi_ref[...]) + _dot(h_hi, wr_lo_ref[...]) + _dot(h_lo, wr_hi_ref[...])) + br_ref[...]

    lane_i = lax.broadcasted_iota(I32, logits.shape, 1)
    lane = lane_i.astype(F32)
    first = lambda mask: jnp.min(jnp.where(mask, lane, 1e9), axis=-1, keepdims=True)
    gl = jnp.where(lane_i < N_GROUPS, logits, -jnp.inf)
    gmax = jnp.max(gl, axis=-1, keepdims=True)
    g_sel = first(gl == gmax)
    g_gate = 1.0 / jnp.sum(jnp.exp(gl - gmax), axis=-1, keepdims=True)
    e_lane = lane_i - N_GROUPS
    lane_grp = jnp.right_shift(e_lane, 3).astype(F32)
    in_grp = (e_lane >= 0) & (e_lane < N_EXPERTS) & (lane_grp == g_sel)
    el = jnp.where(in_grp, logits, -jnp.inf)
    m1 = jnp.max(el, axis=-1, keepdims=True)
    i1 = first(el == m1)
    el2 = jnp.where(lane == i1, -jnp.inf, el)
    m2 = jnp.max(el2, axis=-1, keepdims=True)
    i2 = first(el2 == m2)
    e2 = jnp.exp(m2 - m1)
    g1 = g_gate / (1.0 + e2)
    g2 = g_gate * e2 / (1.0 + e2)
    eid_ref[...] = jnp.where(lane_i == 0, i1 - N_GROUPS, jnp.where(lane_i == 1, i2 - N_GROUPS, 0.0)).astype(I32)
    gate_ref[...] = jnp.where(lane_i == 0, g1, jnp.where(lane_i == 1, g2, 0.0))


def _out_route(y_rw, y_fox, x2, wo_rw, wo_fox, norm_w, wr_hi, wr_lo, b_r):
    M, D = x2.shape
    W = y_rw.shape[1]
    tm = min(512, M)
    full = lambda a: pl.BlockSpec(a.shape, lambda i: (0, 0))
    tile = lambda w: pl.BlockSpec((tm, w), lambda i: (i, 0))
    return pl.pallas_call(
        _out_route_kernel,
        grid=(M // tm,),
        in_specs=[tile(W), tile(W), tile(D), full(wo_rw), full(wo_fox), full(norm_w),
                  full(wr_hi), full(wr_lo), full(b_r)],
        out_specs=[tile(D), pl.BlockSpec((tm * (D // 128), 128), lambda i: (i, 0)),
                   tile(ROUTER_LANES), tile(ROUTER_LANES)],
        out_shape=[jax.ShapeDtypeStruct((M, D), F32), jax.ShapeDtypeStruct((M * (D // 128), 128), F32),
                   jax.ShapeDtypeStruct((M, ROUTER_LANES), I32), jax.ShapeDtypeStruct((M, ROUTER_LANES), F32)],
        compiler_params=_cparams(("parallel",)),
        name="out_route",
    )(y_rw, y_fox, x2, wo_rw, wo_fox, norm_w, wr_hi, wr_lo, b_r)


def _rank_kernel(eid_ref, rank_ref, cnt_ref, carry_ref):
    @pl.when(pl.program_id(0) == 0)
    def _():
        carry_ref[...] = jnp.zeros_like(carry_ref)

    eid = eid_ref[...].astype(F32)
    tm = eid.shape[0]
    lane = lax.broadcasted_iota(I32, eid.shape, 1)
    lane_f = lane.astype(F32)
    pick = lambda l: jnp.sum(jnp.where(lane == l, eid, 0.0), axis=-1, keepdims=True)
    e0, e1 = pick(0), pick(1)
    oh0 = (lane_f == e0).astype(F32)
    oh1 = (lane_f == e1).astype(F32)
    both = oh0 + oh1
    ri = lax.broadcasted_iota(I32, (tm, tm), 0)
    ci = lax.broadcasted_iota(I32, (tm, tm), 1)
    before = _dot(jnp.where(ri > ci, 1.0, 0.0).astype(BF16), both.astype(BF16)) + carry_ref[...]
    r0 = jnp.sum(oh0 * before, axis=-1, keepdims=True)
    r1 = jnp.sum(oh1 * (before + oh0), axis=-1, keepdims=True)
    rank_ref[...] = jnp.where(lane == 0, r0, jnp.where(lane == 1, r1, 0.0)).astype(I32)
    total = carry_ref[...] + jnp.sum(both, axis=0, keepdims=True)
    carry_ref[...] = total
    cnt_ref[...] = jnp.broadcast_to(total, cnt_ref.shape).astype(I32)


def _rank(eid):
    M = eid.shape[0]
    tm = min(512, M)
    return pl.pallas_call(
        _rank_kernel,
        grid=(M // tm,),
        in_specs=[pl.BlockSpec((tm, ROUTER_LANES), lambda i: (i, 0))],
        out_specs=[pl.BlockSpec((tm, ROUTER_LANES), lambda i: (i, 0)),
                   pl.BlockSpec((8, ROUTER_LANES), lambda i: (0, 0))],
        out_shape=[jax.ShapeDtypeStruct((M, ROUTER_LANES), I32), jax.ShapeDtypeStruct((8, ROUTER_LANES), I32)],
        scratch_shapes=[pltpu.VMEM((1, ROUTER_LANES), F32)],
        compiler_params=_cparams(("arbitrary",)),
        name="rank",
    )(eid)


def _dest_kernel(eid_ref, rank_ref, start_ref, dest_ref):
    eid = eid_ref[...].astype(F32)
    rank = rank_ref[...].astype(F32)
    lane = lax.broadcasted_iota(I32, eid.shape, 1)
    lane_f = lane.astype(F32)
    pick = lambda x, l: jnp.sum(jnp.where(lane == l, x, 0.0), axis=-1, keepdims=True)
    base = lambda e: jnp.sum(jnp.where(lane_f == e, start_ref[...], 0.0), axis=-1, keepdims=True)
    d0 = base(pick(eid, 0)) + pick(rank, 0)
    d1 = base(pick(eid, 1)) + pick(rank, 1)
    dest_ref[...] = jnp.where(lane == 0, d0, jnp.where(lane == 1, d1, 0.0)).astype(I32)


def _dest(eid, rank, pad_start_row):
    M = eid.shape[0]
    tm = min(1024, M)
    tile = pl.BlockSpec((tm, ROUTER_LANES), lambda i: (i, 0))
    return pl.pallas_call(
        _dest_kernel,
        grid=(M // tm,),
        in_specs=[tile, tile, pl.BlockSpec((1, ROUTER_LANES), lambda i: (0, 0))],
        out_specs=tile,
        out_shape=jax.ShapeDtypeStruct((M, ROUTER_LANES), I32),
        compiler_params=_cparams(("parallel",)),
        name="dest",
    )(eid, rank, pad_start_row)


def _invert_kernel(dest_ref, cnt_ref, start_ref, end_ref, tok_ref):
    def clear(r, carry):
        tok_ref[r] = 0
        return carry

    def clear_expert(e, carry):
        lax.fori_loop(start_ref[e] + cnt_ref[e], end_ref[e], clear, 0)
        return carry

    def put(a, carry):
        tok_ref[dest_ref[a]] = lax.shift_right_logical(a, 1)
        return carry

    lax.fori_loop(0, cnt_ref.shape[0], clear_expert, 0)
    lax.fori_loop(end_ref[end_ref.shape[0] - 1], tok_ref.shape[0], clear, 0)
    lax.fori_loop(0, dest_ref.shape[0], put, 0, unroll=16)


def _invert(dest_flat, counts, pad_start, pad_end, n_rows):
    smem = pl.BlockSpec(memory_space=pltpu.SMEM)
    return pl.pallas_call(
        _invert_kernel,
        in_specs=[smem] * 4,
        out_specs=smem,
        out_shape=jax.ShapeDtypeStruct((n_rows,), I32),
        name="invert",
    )(dest_flat, counts, pad_start, pad_end)


def _experts_kernel(be_ref, nused_ref, tok_ref, h_hbm, wg_ref, wu_ref, wd_ref, yb_ref,
                    xbuf, sem, wg_bf, wu_bf, wd_bf):
    i = pl.program_id(0)
    D, F = wg_bf.shape
    nt = D // 128
    rb = yb_ref.shape[0] // nt
    n_used = nused_ref[0]
    slot = i & 1

    def copy(blk, r, s):
        src = pl.multiple_of(tok_ref[blk * rb + r] * nt, nt)
        return pltpu.make_async_copy(h_hbm.at[pl.ds(src, nt)], xbuf.at[s, pl.ds(r * nt, nt)], sem.at[s])

    def wait_block(blk, s):
        def body(r, carry):
            copy(blk, r, s).wait()
            return carry
        lax.fori_loop(0, rb, body, 0, unroll=8)

    @pl.when(i == 0)
    def _():
        def body(r, carry):
            copy(0, r, 0).start()
            return carry
        lax.fori_loop(0, rb, body, 0, unroll=8)

    prev = be_ref[jnp.maximum(i - 1, 0)]

    @pl.when((i == 0) | (be_ref[i] != prev))
    def _():
        wg_bf[...] = wg_ref[0].astype(BF16)
        wu_bf[...] = wu_ref[0].astype(BF16)
        wd_bf[...] = wd_ref[0].astype(BF16)

    @pl.when(i < n_used)
    def _():
        def start_next(r, carry):
            copy(i + 1, 2 * r, 1 - slot).start(priority=0)
            copy(i + 1, 2 * r + 1, 1 - slot).start(priority=1)
            return carry
        lax.fori_loop(0, rb // 2, start_next, 0, unroll=4)
        wait_block(i, slot)

        xb = jnp.concatenate([_load_token_major(xbuf.at[slot], rb, D, s).astype(BF16) for s in range(nt)], axis=1)
        gate = _dot(xb, wg_bf[...])
        up = _dot(xb, wu_bf[...])
        hid = gate * _sigmoid(gate) * up
        _store_token_major(yb_ref, _dot(hid.astype(BF16), wd_bf[...]))

    @pl.when(i == n_used)
    def _():
        wait_block(i, slot)

    @pl.when(i >= n_used)
    def _():
        yb_ref[...] = jnp.zeros_like(yb_ref)


def _experts(block_expert, n_used, tok, h2, w_gate, w_up, w_down, rb):
    E, D, F = w_gate.shape
    nt = D // 128
    nb = tok.shape[0] // rb
    grid_spec = pltpu.PrefetchScalarGridSpec(
        num_scalar_prefetch=3,
        grid=(nb,),
        in_specs=[pl.BlockSpec(memory_space=pl.ANY),
                  pl.BlockSpec((1, D, F), lambda i, be, nu, tk: (be[i], 0, 0)),
                  pl.BlockSpec((1, D, F), lambda i, be, nu, tk: (be[i], 0, 0)),
                  pl.BlockSpec((1, F, D), lambda i, be, nu, tk: (be[i], 0, 0))],
        out_specs=pl.BlockSpec((rb * nt, 128), lambda i, be, nu, tk: (i, 0)),
        scratch_shapes=[pltpu.VMEM((2, rb * nt, 128), F32), pltpu.SemaphoreType.DMA((2,)),
                        pltpu.VMEM((D, F), BF16), pltpu.VMEM((D, F), BF16), pltpu.VMEM((F, D), BF16)],
    )
    return pl.pallas_call(
        _experts_kernel,
        grid_spec=grid_spec,
        out_shape=jax.ShapeDtypeStruct((nb * rb * nt, 128), F32),
        compiler_params=_cparams(("arbitrary",)),
        name="experts",
    )(block_expert, n_used, tok, h2, w_gate, w_up, w_down)


def _combine_kernel(dest_ref, gate_ref, x1_ref, yb_hbm, out_ref, buf, sem):
    i = pl.program_id(0)
    tmc, D = x1_ref.shape
    nt = D // 128
    slot = i & 1

    def copy(tile, t, k, s):
        src = pl.multiple_of(dest_ref[2 * (tile * tmc + t) + k] * nt, nt)
        return pltpu.make_async_copy(yb_hbm.at[pl.ds(src, nt)], buf.at[s, k, pl.ds(t * nt, nt)], sem.at[s])

    def start_tile(tile, s):
        def body(t, carry):
            copy(tile, t, 0, s).start(priority=0)
            copy(tile, t, 1, s).start(priority=1)
            return carry
        lax.fori_loop(0, tmc, body, 0, unroll=8)

    @pl.when(i == 0)
    def _():
        start_tile(0, 0)

    @pl.when(i + 1 < pl.num_programs(0))
    def _():
        start_tile(i + 1, 1 - slot)

    def wait(t, carry):
        copy(i, t, 0, slot).wait()
        copy(i, t, 1, slot).wait()
        return carry

    lax.fori_loop(0, tmc, wait, 0, unroll=8)
    gate = gate_ref[...]
    g0, g1 = gate[:, 0:1], gate[:, 1:2]
    for s in range(nt):
        cols = slice(s * 128, (s + 1) * 128)
        out_ref[:, cols] = (x1_ref[:, cols] + g0 * _load_token_major(buf.at[slot, 0], tmc, D, s)
                            + g1 * _load_token_major(buf.at[slot, 1], tmc, D, s))


def _combine(dest_flat, gates, x1, yb):
    M, D = x1.shape
    tmc = min(256, M)
    grid_spec = pltpu.PrefetchScalarGridSpec(
        num_scalar_prefetch=1,
        grid=(M // tmc,),
        in_specs=[pl.BlockSpec((tmc, ROUTER_LANES), lambda i, d: (i, 0)),
                  pl.BlockSpec((tmc, D), lambda i, d: (i, 0)),
                  pl.BlockSpec(memory_space=pl.ANY)],
        out_specs=pl.BlockSpec((tmc, D), lambda i, d: (i, 0)),
        scratch_shapes=[pltpu.VMEM((2, 2, tmc * (D // 128), 128), F32), pltpu.SemaphoreType.DMA((2,))],
    )
    return pl.pallas_call(
        _combine_kernel,
        grid_spec=grid_spec,
        out_shape=jax.ShapeDtypeStruct((M, D), F32),
        compiler_params=_cparams(("arbitrary",)),
        name="combine",
    )(dest_flat, gates, x1, yb)


def _mixer(x2, B, T, norm_w, w_in, mu, w0, w2, a0, a2, g2, k_k, k_a, r_k, ln_w, ln_b, b_f, qn_w, kn_w):
    rw_heads = w0.shape[0] // HEAD
    fox_heads = b_f.shape[0]
    Wr, Wf = rw_heads * HEAD, fox_heads * HEAD
    lora = w2.shape[0] + a2.shape[0] + g2.shape[0]
    rw_cols = 3 * Wr + lora
    o_w, o_k, o_v, o_a = Wr, Wr + w2.shape[0], 2 * Wr + w2.shape[0], 3 * Wr + w2.shape[0]
    perm = np.concatenate([np.arange(0, Wr), np.arange(o_k, o_k + Wr), np.arange(o_v, o_v + Wr),
                           np.arange(o_w, o_w + w2.shape[0]), np.arange(o_a, rw_cols)])
    w_rw = w_in[:, :rw_cols][:, perm].astype(BF16)
    w_qkv = w_in[:, rw_cols:rw_cols + 3 * Wf].astype(BF16)
    w_f = jnp.pad(w_in[:, rw_cols + 3 * Wf:], ((0, 0), (0, 128 - fox_heads))).astype(BF16)
    p_rw, p_qkv, p_f = _in_proj(x2, norm_w[None, :], w_rw, w_qkv, w_f)

    row = lambda a: a.reshape(1, -1)
    ops = _rw_prep(p_rw.reshape(B, T, rw_cols), row(mu[perm]), row(w0), w2.astype(BF16), row(a0),
                   a2.astype(BF16), g2.astype(BF16), row(k_k), row(k_a), row(r_k), rw_heads)
    y_rw = _rw_scan(*ops, row(ln_w), row(ln_b))

    tile_w = lambda w: row(jnp.tile(w, fox_heads))
    q_aug, k_aug, vt = _fox_prep(p_qkv.reshape(B, T, 3 * Wf), p_f.reshape(B, T, 128),
                                 row(jnp.pad(b_f, (0, 128 - fox_heads))), tile_w(qn_w), tile_w(kn_w), fox_heads)
    y_fox = _fox_attn(q_aug, k_aug, vt)
    return y_rw.reshape(B * T, Wr), y_fox.reshape(B * T, Wf)


def _moe(y_rw, y_fox, x2, w_out, norm_w, rg_w, rg_b, re_w, re_b, w_gate, w_up, w_down):
    M, D = x2.shape
    Wr = y_rw.shape[1]
    pad = ROUTER_LANES - N_GROUPS - N_EXPERTS
    w_r = jnp.pad(jnp.concatenate([rg_w, re_w], axis=1), ((0, 0), (0, pad)))
    b_r = jnp.pad(jnp.concatenate([rg_b, re_b]), (0, pad))[None, :]
    wr_hi = w_r.astype(BF16)
    wr_lo = (w_r - wr_hi.astype(F32)).astype(BF16)
    x1, h2, eid, gates = _out_route(y_rw, y_fox, x2, w_out[:Wr].astype(BF16), w_out[Wr:].astype(BF16),
                                    norm_w[None, :], wr_hi, wr_lo, b_r)
    rank, counts = _rank(eid)

    rb = ROW_BLOCK
    counts = counts[0, :N_EXPERTS]
    padded = (counts + rb - 1) // rb * rb
    pad_end = jnp.cumsum(padded)
    pad_start = pad_end - padded
    n_blocks = (2 * M + N_EXPERTS * (rb - 1) + rb - 1) // rb + 1
    block_start = jnp.arange(n_blocks, dtype=I32) * rb
    block_expert = jnp.minimum(jnp.sum(pad_end[None, :] <= block_start[:, None], axis=1), N_EXPERTS - 1).astype(I32)
    n_used = (pad_end[-1:] // rb).astype(I32)
    start_row = jnp.pad(pad_start.astype(F32), (0, ROUTER_LANES - N_EXPERTS))[None, :]
    dest = _dest(eid, rank, start_row)[:, :2].reshape(-1)

    tok = _invert(dest, counts, pad_start.astype(I32), pad_end.astype(I32), n_blocks * rb)
    yb = _experts(block_expert, n_used, tok, h2, w_gate, w_up, w_down, rb)
    return _combine(dest, gates, x1, yb)


def kernel(x, norm_mix_w, w_in, mu_shift, rw_w0, rw_w2, rw_a0, rw_a2, rw_g2, rw_k_k, rw_k_a, rw_r_k, rw_ln_w, rw_ln_b, fox_b_f, fox_q_norm_w, fox_k_norm_w, w_out, norm_ffn_w, router_group_w, router_group_b, router_expert_w, router_expert_b, exp_w_gate, exp_w_up, exp_w_down):
    B, T, D = x.shape
    x2 = x.reshape(B * T, D)
    for l in range(w_in.shape[0]):
        y_rw, y_fox = _mixer(x2, B, T, norm_mix_w[l], w_in[l], mu_shift[l], rw_w0[l], rw_w2[l], rw_a0[l],
                             rw_a2[l], rw_g2[l], rw_k_k[l], rw_k_a[l], rw_r_k[l].reshape(-1), rw_ln_w[l],
                             rw_ln_b[l], fox_b_f[l], fox_q_norm_w[l], fox_k_norm_w[l])
        x2 = _moe(y_rw, y_fox, x2, w_out[l], norm_ffn_w[l], router_group_w[l], router_group_b[l],
                  router_expert_w[l], router_expert_b[l], exp_w_gate[l], exp_w_up[l], exp_w_down[l])
    return x2.reshape(B, T, D)
```
